```python
import math
import jax, jax.numpy as jnp
from jax import lax
import numpy as np

D_MODEL = 1024
BATCH = 8
SEQ = 4096
DEPTH = 2

HEAD_DIM = 64
CONV_CH = D_MODEL // 4
MOBA_DIM = 3 * D_MODEL // 8
FOX_DIM = D_MODEL - CONV_CH - MOBA_DIM
MOBA_HEADS = MOBA_DIM // HEAD_DIM
FOX_HEADS = FOX_DIM // HEAD_DIM
MIX_DIM = CONV_CH + MOBA_DIM + FOX_DIM
CONV_WIDTH = 31
LN_EPS = 1e-5
RMS_EPS = 1e-6
MOBA_BLOCK = 256
MOBA_TOPK = 3
MOBA_Q_BLOCK = 64
FOX_Q_BLOCK = 128
NUM_BUCKETS = 32
MAX_DISTANCE = 128
D_FF = -(-8 * D_MODEL // (3 * 256)) * 256
NEG = -1e30

IN_WIDTHS = (CONV_CH, CONV_CH, MOBA_DIM, MOBA_DIM, MOBA_DIM, FOX_DIM, FOX_DIM, FOX_DIM, FOX_HEADS)
IN_DIM = 2 * CONV_CH + 3 * MOBA_DIM + 3 * FOX_DIM + FOX_HEADS
IN_SPLITS = tuple(int(s) for s in np.cumsum(IN_WIDTHS)[:-1])

kernel_name = "hybrid_conv_moba_fox_block"


def rms_norm(x, g):
    xf = x.astype(jnp.float32)
    y = xf * lax.rsqrt(jnp.mean(xf * xf, axis=-1, keepdims=True) + RMS_EPS)
    return (y * g.astype(jnp.float32)).astype(x.dtype)


def layer_norm(x, g, b):
    xf = x.astype(jnp.float32)
    mu = jnp.mean(xf, axis=-1, keepdims=True)
    var = jnp.mean(jnp.square(xf - mu), axis=-1, keepdims=True)
    y = (xf - mu) * lax.rsqrt(var + LN_EPS)
    return (y * g.astype(jnp.float32) + b.astype(jnp.float32)).astype(x.dtype)


def to_heads(t, n_heads):
    b, s, _ = t.shape
    return t.reshape(b, s, n_heads, HEAD_DIM).transpose(0, 2, 1, 3)


def from_heads(t):
    b, h, s, d = t.shape
    return t.transpose(0, 2, 1, 3).reshape(b, s, h * d)


def t5_bucket(dist):
    n = jnp.maximum(dist, 0)
    max_exact = NUM_BUCKETS // 2
    nf = jnp.maximum(n, 1).astype(jnp.float32)
    large = max_exact + (jnp.log(nf / max_exact) / math.log(MAX_DISTANCE / max_exact)
                         * (NUM_BUCKETS - max_exact)).astype(jnp.int32)
    large = jnp.minimum(large, NUM_BUCKETS - 1)
    return jnp.where(n < max_exact, n, large)


def conv_module(a_val, a_gate, w_dw, b_dw, ln_g, ln_b):
    h = a_val * jax.nn.sigmoid(a_gate)
    hp = jnp.pad(h, ((0, 0), (CONV_WIDTH - 1, 0), (0, 0)))
    y = lax.conv_general_dilated(hp, w_dw[:, None, :].astype(h.dtype), window_strides=(1,),
                                 padding='VALID', dimension_numbers=('NWC', 'WIO', 'NWC'),
                                 feature_group_count=h.shape[-1]) + b_dw
    y = layer_norm(y, ln_g, ln_b)
    return jax.nn.silu(y)


def moba_attention(q, k, v, bias_table):
    B, H, T, Dh = q.shape
    L = MOBA_BLOCK
    nb = -(-T // L)
    k_sel_n = min(MOBA_TOPK, nb)
    pad = nb * L - T
    kp = jnp.pad(k, ((0, 0), (0, 0), (0, pad), (0, 0)))
    vp = jnp.pad(v, ((0, 0), (0, 0), (0, pad), (0, 0)))
    kb = kp.reshape(B, H, nb, L, Dh)
    vb = vp.reshape(B, H, nb, L, Dh)
    kmean = jnp.mean(kb.astype(jnp.float32), axis=3).astype(q.dtype)
    scale = Dh ** -0.5
    table_h = bias_table.T.astype(jnp.float32)
    h_idx = jnp.arange(H)[None, :, None, None, None]
    offs = jnp.arange(L)
    n_chunks = T // MOBA_Q_BLOCK
    qc = q.reshape(B, H, n_chunks, MOBA_Q_BLOCK, Dh).transpose(2, 0, 1, 3, 4)
    gather_blocks = jax.vmap(jax.vmap(lambda blocks, ids: blocks[ids]))

    def chunk(args):
        qi, c = args
        t = c * MOBA_Q_BLOCK + jnp.arange(MOBA_Q_BLOCK)
        blk = (c * MOBA_Q_BLOCK) // L
        gate = jnp.einsum('bhqd,bhnd->bhqn', qi, kmean).astype(jnp.float32)
        gate = jnp.where(jnp.arange(nb) < blk, gate, -jnp.inf)
        _, idx = lax.top_k(gate, k_sel_n)
        slot_ok = jnp.arange(k_sel_n) < blk
        ksel = gather_blocks(kb, idx)
        vsel = gather_blocks(vb, idx)
        s_sel = jnp.einsum('bhqd,bhqkld->bhqkl', qi, ksel).astype(jnp.float32) * scale
        pos_sel = idx[..., None] * L + offs
        s_sel = s_sel + table_h[h_idx, t5_bucket(t[None, None, :, None, None] - pos_sel)]
        s_sel = jnp.where(slot_ok[:, None], s_sel, NEG)
        kown = lax.dynamic_index_in_dim(kb, blk, axis=2, keepdims=False)
        vown = lax.dynamic_index_in_dim(vb, blk, axis=2, keepdims=False)
        dist_own = t[:, None] - (blk * L + offs)[None, :]
        s_own = jnp.einsum('bhqd,bhld->bhql', qi, kown).astype(jnp.float32) * scale
        s_own = s_own + table_h[:, t5_bucket(dist_own)][None]
        s_own = jnp.where(dist_own >= 0, s_own, NEG)
        scores = jnp.concatenate([s_sel.reshape(B, H, MOBA_Q_BLOCK, k_sel_n * L), s_own], axis=-1)
        p = jax.nn.softmax(scores, axis=-1).astype(v.dtype)
        p_sel = p[..., :k_sel_n * L].reshape(B, H, MOBA_Q_BLOCK, k_sel_n, L)
        p_own = p[..., k_sel_n * L:]
        return (jnp.einsum('bhqkl,bhqkld->bhqd', p_sel, vsel)
                + jnp.einsum('bhql,bhld->bhqd', p_own, vown))

    out = lax.map(chunk, (qc, jnp.arange(n_chunks)))
    return out.transpose(1, 2, 0, 3, 4).reshape(B, H, T, Dh)


def fox_attention(q, k, v, log_f):
    B, H, T, Dh = q.shape
    scale = Dh ** -0.5
    csum = jnp.cumsum(log_f, axis=-1)
    n_chunks = T // FOX_Q_BLOCK
    qc = q.reshape(B, H, n_chunks, FOX_Q_BLOCK, Dh).transpose(2, 0, 1, 3, 4)
    cc = csum.reshape(B, H, n_chunks, FOX_Q_BLOCK).transpose(2, 0, 1, 3)
    s_pos = jnp.arange(T)

    def chunk(args):
        qi, ci, c = args
        t = c * FOX_Q_BLOCK + jnp.arange(FOX_Q_BLOCK)
        s = jnp.einsum('bhqd,bhsd->bhqs', qi, k).astype(jnp.float32) * scale
        s = s + ci[..., None] - csum[:, :, None, :]
        s = jnp.where(t[:, None] >= s_pos[None, :], s, NEG)
        p = jax.nn.softmax(s, axis=-1).astype(v.dtype)
        return jnp.einsum('bhqs,bhsd->bhqd', p, v)

    out = lax.map(chunk, (qc, cc, jnp.arange(n_chunks)))
    return out.transpose(1, 2, 0, 3, 4).reshape(B, H, T, Dh)


def setup_inputs(seed: int = 0) -> dict:
    key = jax.random.key(seed)
    ks = jax.random.split(key, 20)
    f32 = jnp.float32
    nrm = lambda k, shape, s: (jax.random.normal(k, shape, f32) * s)
    return {
        "x": nrm(ks[0], (BATCH, SEQ, D_MODEL), 1.0),
        "w_in": nrm(ks[1], (DEPTH, D_MODEL, IN_DIM), D_MODEL ** -0.5),
        "b_forget": 1.0 + nrm(ks[2], (DEPTH, FOX_HEADS), 0.1),
        "conv_w": nrm(ks[3], (DEPTH, CONV_WIDTH, CONV_CH), CONV_WIDTH ** -0.5),
        "conv_b": nrm(ks[4], (DEPTH, CONV_CH), 0.02),
        "conv_ln_g": 1.0 + nrm(ks[5], (DEPTH, CONV_CH), 0.02),
        "conv_ln_b": nrm(ks[6], (DEPTH, CONV_CH), 0.02),
        "moba_qn_g": 1.0 + nrm(ks[7], (DEPTH, HEAD_DIM), 0.02),
        "moba_kn_g": 1.0 + nrm(ks[8], (DEPTH, HEAD_DIM), 0.02),
        "fox_qn_g": 1.0 + nrm(ks[9], (DEPTH, HEAD_DIM), 0.02),
        "fox_kn_g": 1.0 + nrm(ks[10], (DEPTH, HEAD_DIM), 0.02),
        "rel_bias": nrm(ks[11], (NUM_BUCKETS, MOBA_HEADS), 0.5),
        "w_out": nrm(ks[12], (DEPTH, MIX_DIM, D_MODEL), MIX_DIM ** -0.5),
        "norm1_g": 1.0 + nrm(ks[13], (DEPTH, D_MODEL), 0.02),
        "norm2_g": 1.0 + nrm(ks[14], (DEPTH, D_MODEL), 0.02),
        "w_gate_up": nrm(ks[15], (DEPTH, D_MODEL, 2 * D_FF), D_MODEL ** -0.5),
        "w_down": nrm(ks[16], (DEPTH, D_FF, D_MODEL), D_FF ** -0.5),
    }


def reference(x, w_in, b_forget, conv_w, conv_b, conv_ln_g, conv_ln_b, moba_qn_g, moba_kn_g,
              fox_qn_g, fox_kn_g, rel_bias, w_out, norm1_g, norm2_g, w_gate_up, w_down):
    for l in range(DEPTH):
        h = rms_norm(x, norm1_g[l])
        proj = h @ w_in[l]
        a_val, a_gate, qb, kb, vb, qc, kc, vc, f_logit = jnp.split(proj, IN_SPLITS, axis=-1)
        y_a = conv_module(a_val, a_gate, conv_w[l], conv_b[l], conv_ln_g[l], conv_ln_b[l])
        qb_h = rms_norm(to_heads(qb, MOBA_HEADS), moba_qn_g[l])
        kb_h = rms_norm(to_heads(kb, MOBA_HEADS), moba_kn_g[l])
        y_b = from_heads(moba_attention(qb_h, kb_h, to_heads(vb, MOBA_HEADS), rel_bias))
        qc_h = rms_norm(to_heads(qc, FOX_HEADS), fox_qn_g[l])
        kc_h = rms_norm(to_heads(kc, FOX_HEADS), fox_kn_g[l])
        log_f = jax.nn.log_sigmoid(f_logit.astype(jnp.float32) + b_forget[l].astype(jnp.float32))
        y_c = from_heads(fox_attention(qc_h, kc_h, to_heads(vc, FOX_HEADS), log_f.transpose(0, 2, 1)))
        mixed = jnp.concatenate([y_a, y_b, y_c], axis=-1)
        x = x + mixed @ w_out[l]
        h2 = rms_norm(x, norm2_g[l])
        g, u = jnp.split(h2 @ w_gate_up[l], 2, axis=-1)
        x = x + (jax.nn.silu(g) * u) @ w_down[l]
    return x
```

```python
import functools
import math

import jax
import jax.numpy as jnp
from jax import lax
from jax.experimental import pallas as pl
from jax.experimental.pallas import tpu as pltpu

F32 = jnp.float32
BF16 = jnp.bfloat16

D_MODEL = 1024
HEAD_DIM = 64
CONV_CH = D_MODEL // 4
MOBA_DIM = 3 * D_MODEL // 8
FOX_DIM = D_MODEL - CONV_CH - MOBA_DIM
MOBA_HEADS = MOBA_DIM // HEAD_DIM
FOX_HEADS = FOX_DIM // HEAD_DIM
CONV_WIDTH = 31
LN_EPS = 1e-5
RMS_EPS = 1e-6
MOBA_BLOCK = 256
MOBA_TOPK = 3
NUM_BUCKETS = 32
MAX_DISTANCE = 128
D_FF = -(-8 * D_MODEL // (3 * 256)) * 256
NEG = -1e30

LANES = 128
HEAD_PAIRS = MOBA_DIM // LANES
IN_MAIN = 2 * CONV_CH + 3 * MOBA_DIM + 3 * FOX_DIM
IN_PAD = IN_MAIN + LANES

TM = 512
TQ = 256
TK = 256
CONV_TC = 256
CONV_HALO = 32
CONV_SUB = 64
FF_CHUNKS = ((0, 768), (768, 1536), (1536, 2304), (2304, 2816))
VMEM_LIMIT = 56 * 1024 * 1024


def _split3(x):
    hi = x.astype(BF16)
    r1 = x - hi.astype(F32)
    mid = r1.astype(BF16)
    lo = (r1 - mid.astype(F32)).astype(BF16)
    return hi, mid, lo


def _dot(a, b):
    return jnp.dot(a, b, preferred_element_type=F32)


def _dot_nt(a, b):
    return lax.dot_general(a, b, (((1,), (1,)), ((), ())), preferred_element_type=F32)


def _sigmoid(x):
    return 1.0 / (1.0 + jnp.exp(-x))


def _in_proj_kernel(nt, x_ref, g1_ref, w_ref, bf_ref, gains_ref, hm_ref, tri_ref,
                    glu_ref, qb_ref, kb_ref, vbT_ref, qc_ref, kc_ref, vcT_ref, ccol_ref, crow_ref,
                    h_scr, carry_scr):
    r = pl.program_id(0)
    x = x_ref[...]
    ms = jnp.mean(x * x, axis=-1, keepdims=True)
    h_scr[...] = (x * lax.rsqrt(ms + RMS_EPS) * g1_ref[...]).astype(BF16)

    def seg(c0, c1):
        return _dot(h_scr[...], w_ref[:, c0:c1])

    def head_rms(t, gain):
        sq = t * t
        hi = sq.astype(BF16)
        lo = (sq - hi.astype(F32)).astype(BF16)
        msq = _dot(hi, hm_ref[...]) + _dot(lo, hm_ref[...])
        return t * lax.rsqrt(msq + RMS_EPS) * gain

    a = seg(0, 2 * CONV_CH)
    glu_ref[...] = a[:, :CONV_CH] * _sigmoid(a[:, CONV_CH:])

    c0 = 2 * CONV_CH
    qk = seg(c0, c0 + 2 * MOBA_DIM)
    qb_ref[...] = (head_rms(qk[:, :MOBA_DIM], gains_ref[0:1, :]) * (HEAD_DIM ** -0.5)).astype(BF16)
    kb_ref[...] = head_rms(qk[:, MOBA_DIM:], gains_ref[1:2, :]).astype(BF16)

    c0 += 2 * MOBA_DIM
    vq = seg(c0, c0 + MOBA_DIM + FOX_DIM)
    vbT_ref[0] = vq[:, :MOBA_DIM].T.astype(BF16)
    qc_ref[...] = (head_rms(vq[:, MOBA_DIM:], gains_ref[2:3, :]) * (HEAD_DIM ** -0.5)).astype(BF16)

    c0 += MOBA_DIM + FOX_DIM
    kv = seg(c0, c0 + 2 * FOX_DIM)
    kc_ref[...] = head_rms(kv[:, :FOX_DIM], gains_ref[3:4, :]).astype(BF16)
    vcT_ref[0] = kv[:, FOX_DIM:].T.astype(BF16)

    z = seg(IN_MAIN, IN_PAD) + bf_ref[...]
    logf = jnp.minimum(z, 0.0) - jnp.log1p(jnp.exp(-jnp.abs(z)))
    hi, mid, lo = _split3(logf)
    tri = tri_ref[...]
    csum = _dot(tri, hi) + _dot(tri, mid) + _dot(tri, lo)

    @pl.when(r % nt == 0)
    def _():
        carry_scr[...] = jnp.zeros_like(carry_scr)

    c = csum + carry_scr[...]
    carry_scr[...] = c[TM - 1:TM, :]
    ccol_ref[...] = c
    crow_ref[0] = c.T[:8, :]


def _in_proj(x2, g1, w, bf, gains, hm, tri, batch, seq):
    n = x2.shape[0]
    nt = seq // TM
    const = lambda shape: pl.BlockSpec(shape, lambda r: (0,) * len(shape), pipeline_mode=pl.Buffered(1))
    rows = lambda width: pl.BlockSpec((TM, width), lambda r: (r, 0))
    tposed = lambda height: pl.BlockSpec((1, height, TM), lambda r: (r // nt, 0, r % nt))
    out_shape = (
        jax.ShapeDtypeStruct((n, CONV_CH), F32),
        jax.ShapeDtypeStruct((n, MOBA_DIM), BF16),
        jax.ShapeDtypeStruct((n, MOBA_DIM), BF16),
        jax.ShapeDtypeStruct((batch, MOBA_DIM, seq), BF16),
        jax.ShapeDtypeStruct((n, FOX_DIM), BF16),
        jax.ShapeDtypeStruct((n, FOX_DIM), BF16),
        jax.ShapeDtypeStruct((batch, FOX_DIM, seq), BF16),
        jax.ShapeDtypeStruct((n, LANES), F32),
        jax.ShapeDtypeStruct((batch, 8, seq), F32),
    )
    return pl.pallas_call(
        functools.partial(_in_proj_kernel, nt),
        grid=(n // TM,),
        in_specs=[rows(D_MODEL), const((1, D_MODEL)), const((D_MODEL, IN_PAD)), const((1, LANES)),
                  const((4, MOBA_DIM)), const((MOBA_DIM, MOBA_DIM)), const((TM, TM))],
        out_specs=(rows(CONV_CH), rows(MOBA_DIM), rows(MOBA_DIM), tposed(MOBA_DIM),
                   rows(FOX_DIM), rows(FOX_DIM), tposed(FOX_DIM), rows(LANES), tposed(8)),
        out_shape=out_shape,
        scratch_shapes=[pltpu.VMEM((TM, D_MODEL), BF16), pltpu.VMEM((1, LANES), F32)],
        compiler_params=pltpu.CompilerParams(dimension_semantics=("arbitrary",),
                                             vmem_limit_bytes=VMEM_LIMIT),
        name="in_proj",
    )(x2, g1, w, bf, gains, hm, tri)


def _conv_kernel(glu_ref, w_ref, b_ref, lg_ref, lb_ref, o_ref, hbuf):
    t = pl.program_id(1)

    @pl.when(t == 0)
    def _():
        hbuf[0:CONV_HALO, :] = jnp.zeros((CONV_HALO, CONV_CH), F32)

    @pl.when(t > 0)
    def _():
        hbuf[0:CONV_HALO, :] = hbuf[CONV_TC:CONV_TC + CONV_HALO, :]

    hbuf[CONV_HALO:, :] = glu_ref[...]
    shift = CONV_HALO - (CONV_WIDTH - 1)
    for r0 in range(0, CONV_TC, CONV_SUB):
        acc = jnp.zeros((CONV_SUB, CONV_CH), F32) + b_ref[...]
        for j in range(CONV_WIDTH):
            acc = acc + w_ref[j:j + 1, :] * hbuf[r0 + j + shift:r0 + j + shift + CONV_SUB, :]
        mu = jnp.mean(acc, axis=-1, keepdims=True)
        d = acc - mu
        var = jnp.mean(d * d, axis=-1, keepdims=True)
        y = d * lax.rsqrt(var + LN_EPS) * lg_ref[...] + lb_ref[...]
        o_ref[r0:r0 + CONV_SUB, :] = (y * _sigmoid(y)).astype(BF16)


def _conv(glu, w, b, lg, lb, batch, seq):
    n = glu.shape[0]
    nt = seq // CONV_TC
    const = lambda shape: pl.BlockSpec(shape, lambda bb, t: (0,) * len(shape))
    return pl.pallas_call(
        _conv_kernel,
        grid=(batch, nt),
        in_specs=[pl.BlockSpec((CONV_TC, CONV_CH), lambda bb, t: (bb * nt + t, 0)),
                  const((CONV_WIDTH, CONV_CH)), const((1, CONV_CH)), const((1, CONV_CH)), const((1, CONV_CH))],
        out_specs=pl.BlockSpec((CONV_TC, CONV_CH), lambda bb, t: (bb * nt + t, 0)),
        out_shape=jax.ShapeDtypeStruct((n, CONV_CH), BF16),
        scratch_shapes=[pltpu.VMEM((CONV_HALO + CONV_TC, CONV_CH), F32)],
        compiler_params=pltpu.CompilerParams(dimension_semantics=("arbitrary", "arbitrary")),
        name="conv_module",
    )(glu, w, b, lg, lb)


def _softmax_tile(sT, vT, m, l, acc):
    m_new = jnp.maximum(m, jnp.max(sT, axis=0, keepdims=True))
    p = jnp.exp(sT - m_new)
    alpha = jnp.exp(m - m_new)
    l = alpha * l + jnp.sum(p, axis=0, keepdims=True)
    acc = alpha * acc + _dot(vT, p.astype(BF16))
    return m_new, l, acc


def _head_lane_mask(hh, rows):
    lane = lax.broadcasted_iota(jnp.int32, (rows, LANES), 1)
    return lane < HEAD_DIM if hh == 0 else lane >= HEAD_DIM


def _init_state():
    return (jnp.full((1, TQ), NEG, F32), jnp.zeros((1, TQ), F32), jnp.zeros((HEAD_DIM, TQ), F32))


def _t5_bucket(dist):
    n = jnp.maximum(dist, 0)
    max_exact = NUM_BUCKETS // 2
    nf = jnp.maximum(n, 1).astype(F32)
    large = max_exact + (jnp.log(nf / max_exact) / math.log(MAX_DISTANCE / max_exact)
                         * (NUM_BUCKETS - max_exact)).astype(jnp.int32)
    large = jnp.minimum(large, NUM_BUCKETS - 1)
    return jnp.where(n < max_exact, n, large)


def _bias_kernel(bucket_ref, rb_ref, o_ref):
    h = pl.program_id(0)
    bucket = bucket_ref[...]
    acc = jnp.zeros(bucket.shape, F32)
    for b in range(NUM_BUCKETS):
        acc = jnp.where(bucket == b, rb_ref[b, h], acc)
    key = lax.broadcasted_iota(jnp.int32, bucket.shape, 0)
    qry = lax.broadcasted_iota(jnp.int32, bucket.shape, 1)
    o_ref[0, 0] = jnp.where((key < MOBA_BLOCK) | (key - MOBA_BLOCK <= qry), acc, NEG)


def _moba_bias(rel_bias):
    key = lax.broadcasted_iota(jnp.int32, (2 * MOBA_BLOCK, TQ), 0)
    qry = lax.broadcasted_iota(jnp.int32, (2 * MOBA_BLOCK, TQ), 1)
    bucket = _t5_bucket(qry + MOBA_BLOCK - key)
    return pl.pallas_call(
        _bias_kernel,
        grid=(MOBA_HEADS,),
        in_specs=[pl.BlockSpec((2 * MOBA_BLOCK, TQ), lambda h: (0, 0)),
                  pl.BlockSpec(memory_space=pltpu.SMEM)],
        out_specs=pl.BlockSpec((1, 1, 2 * MOBA_BLOCK, TQ), lambda h: (h // 2, h % 2, 0, 0)),
        out_shape=jax.ShapeDtypeStruct((HEAD_PAIRS, 2, 2 * MOBA_BLOCK, TQ), F32),
        compiler_params=pltpu.CompilerParams(dimension_semantics=("arbitrary",)),
        name="moba_bias",
    )(bucket, rel_bias)


def _moba_kernel(nblk, q_ref, k_ref, vT_ref, bias_ref, rb_ref, o_ref, kmean_scr, sel_scr):
    p = pl.program_id(1)
    i = pl.program_id(2)

    @pl.when(i == 0)
    def _():
        for n in range(nblk):
            blk = k_ref[n * MOBA_BLOCK:(n + 1) * MOBA_BLOCK, :].astype(F32)
            kmean_scr[n:n + 1, :] = jnp.mean(blk, axis=0, keepdims=True)

    q = q_ref[...]
    q0 = pl.multiple_of(i * TQ, TQ)
    blk_row = lax.broadcasted_iota(jnp.int32, (nblk, TQ), 0)
    outs = []
    for hh in range(2):
        h = 2 * p + hh
        qm = jnp.where(_head_lane_mask(hh, TQ), q, jnp.zeros_like(q))

        km = jnp.where(_head_lane_mask(hh, nblk), kmean_scr[...], 0.0)
        km_hi, km_mid, km_lo = _split3(km)
        gate = _dot_nt(km_hi, qm) + _dot_nt(km_mid, qm) + _dot_nt(km_lo, qm)
        cnt = jnp.zeros((nblk, TQ), jnp.int32)
        for mblk in range(nblk):
            gm = gate[mblk:mblk + 1, :]
            beats = (gm > gate) | ((gm == gate) & (mblk < blk_row))
            cnt = cnt + jnp.where(mblk < i, beats.astype(jnp.int32), 0)
        chosen = (cnt < MOBA_TOPK) & (blk_row < i)
        far_bias = rb_ref[NUM_BUCKETS - 1, h]
        sel_scr[hh] = jnp.where(chosen, far_bias, NEG)

        def tile(k0, add):
            kj = k_ref[pl.ds(k0, TK), :]
            vTj = vT_ref[0, hh * HEAD_DIM:(hh + 1) * HEAD_DIM, pl.ds(k0, TK)]
            return _dot_nt(kj, qm) + add, vTj

        sT, vTj = tile(q0, bias_ref[0, hh, MOBA_BLOCK:, :])
        state = _softmax_tile(sT, vTj, *_init_state())

        def prev_tile(state):
            k0 = pl.multiple_of((i - 1) * TK, TK)
            near = bias_ref[0, hh, :MOBA_BLOCK, :] - far_bias
            sT, vTj = tile(k0, near + sel_scr[hh, pl.ds(i - 1, 1), :])
            return _softmax_tile(sT, vTj, *state)

        state = lax.cond(i >= 1, prev_tile, lambda s: s, state)

        def far_tile(j, state):
            k0 = pl.multiple_of(j * TK, TK)
            sT, vTj = tile(k0, sel_scr[hh, pl.ds(j, 1), :])
            return _softmax_tile(sT, vTj, *state)

        m, l, acc = lax.fori_loop(0, i - 1, far_tile, state)
        outs.append(acc / l)
    o_ref[...] = jnp.concatenate(outs, axis=0).T.astype(BF16)


def _moba(q, k, vT, bias, rel_bias, batch, seq):
    n = q.shape[0]
    nq = seq // TQ
    nblk = seq // MOBA_BLOCK
    return pl.pallas_call(
        functools.partial(_moba_kernel, nblk),
        grid=(batch, HEAD_PAIRS, nq),
        in_specs=[pl.BlockSpec((TQ, LANES), lambda b, p, i: (b * nq + i, p)),
                  pl.BlockSpec((seq, LANES), lambda b, p, i: (b, p)),
                  pl.BlockSpec((1, LANES, seq), lambda b, p, i: (b, p, 0)),
                  pl.BlockSpec((1, 2, 2 * MOBA_BLOCK, TQ), lambda b, p, i: (p, 0, 0, 0)),
                  pl.BlockSpec(memory_space=pltpu.SMEM)],
        out_specs=pl.BlockSpec((TQ, LANES), lambda b, p, i: (b * nq + i, p)),
        out_shape=jax.ShapeDtypeStruct((n, MOBA_DIM), BF16),
        scratch_shapes=[pltpu.VMEM((nblk, LANES), F32), pltpu.VMEM((2, nblk, TQ), F32)],
        compiler_params=pltpu.CompilerParams(dimension_semantics=("arbitrary", "arbitrary", "arbitrary"),
                                             vmem_limit_bytes=VMEM_LIMIT),
        name="moba_attention",
    )(q, k, vT, bias, rel_bias)


def _fox_kernel(q_ref, k_ref, vT_ref, ccol_ref, crow_ref, o_ref):
    p = pl.program_id(1)
    i = pl.program_id(2)
    q = q_ref[...]
    q0 = pl.multiple_of(i * TQ, TQ)
    key = lax.broadcasted_iota(jnp.int32, (TK, TQ), 0)
    qry = lax.broadcasted_iota(jnp.int32, (TK, TQ), 1)
    lane = lax.broadcasted_iota(jnp.int32, (TK, LANES), 1)
    outs = []
    for hh in range(2):
        h = 2 * p + hh
        qm = jnp.where(_head_lane_mask(hh, TQ), q, jnp.zeros_like(q))
        c_q = crow_ref[0, pl.ds(h, 1), pl.ds(q0, TQ)]

        def tile(k0):
            kj = k_ref[pl.ds(k0, TK), :]
            vTj = vT_ref[0, hh * HEAD_DIM:(hh + 1) * HEAD_DIM, pl.ds(k0, TK)]
            c_k = jnp.sum(jnp.where(lane == h, ccol_ref[pl.ds(k0, TK), :], 0.0), axis=1, keepdims=True)
            return _dot_nt(kj, qm) + c_q - c_k, vTj

        sT, vTj = tile(q0)
        state = _softmax_tile(jnp.where(key <= qry, sT, NEG), vTj, *_init_state())

        def past_tile(j, state):
            sT, vTj = tile(pl.multiple_of(j * TK, TK))
            return _softmax_tile(sT, vTj, *state)

        m, l, acc = lax.fori_loop(0, i, past_tile, state)
        outs.append(acc / l)
    o_ref[...] = jnp.concatenate(outs, axis=0).T.astype(BF16)


def _fox(q, k, vT, ccol, crow, batch, seq):
    n = q.shape[0]
    nq = seq // TQ
    return pl.pallas_call(
        _fox_kernel,
        grid=(batch, HEAD_PAIRS, nq),
        in_specs=[pl.BlockSpec((TQ, LANES), lambda b, p, i: (b * nq + i, p)),
                  pl.BlockSpec((seq, LANES), lambda b, p, i: (b, p)),
                  pl.BlockSpec((1, LANES, seq), lambda b, p, i: (b, p, 0)),
                  pl.BlockSpec((seq, LANES), lambda b, p, i: (b, 0)),
                  pl.BlockSpec((1, 8, seq), lambda b, p, i: (b, 0, 0))],
        out_specs=pl.BlockSpec((TQ, LANES), lambda b, p, i: (b * nq + i, p)),
        out_shape=jax.ShapeDtypeStruct((n, FOX_DIM), BF16),
        compiler_params=pltpu.CompilerParams(dimension_semantics=("arbitrary", "arbitrary", "arbitrary"),
                                             vmem_limit_bytes=VMEM_LIMIT),
        name="fox_attention",
    )(q, k, vT, ccol, crow)


def _out_ffn_kernel(x_ref, ya_ref, yb_ref, yc_ref, wo_ref, g2_ref, wgu_ref, wd_ref, o_ref,
                    mix_scr, h2_scr, acc_scr):
    mix_scr[:, 0:CONV_CH] = ya_ref[...]
    mix_scr[:, CONV_CH:CONV_CH + MOBA_DIM] = yb_ref[...]
    mix_scr[:, CONV_CH + MOBA_DIM:] = yc_ref[...]
    x1 = x_ref[...] + _dot(mix_scr[...], wo_ref[...])
    ms = jnp.mean(x1 * x1, axis=-1, keepdims=True)
    h2_scr[...] = (x1 * lax.rsqrt(ms + RMS_EPS) * g2_ref[...]).astype(BF16)
    acc_scr[...] = x1
    for c0, c1 in FF_CHUNKS:
        g = _dot(h2_scr[...], wgu_ref[:, c0:c1])
        u = _dot(h2_scr[...], wgu_ref[:, D_FF + c0:D_FF + c1])
        act = (g * _sigmoid(g) * u).astype(BF16)
        acc_scr[...] += _dot(act, wd_ref[c0:c1, :])
    o_ref[...] = acc_scr[...]


def _out_ffn(x2, ya, yb, yc, wo, g2, wgu, wd):
    n = x2.shape[0]
    const = lambda shape: pl.BlockSpec(shape, lambda r: (0,) * len(shape), pipeline_mode=pl.Buffered(1))
    rows = lambda width: pl.BlockSpec((TM, width), lambda r: (r, 0))
    return pl.pallas_call(
        _out_ffn_kernel,
        grid=(n // TM,),
        in_specs=[rows(D_MODEL), rows(CONV_CH), rows(MOBA_DIM), rows(FOX_DIM),
                  const((D_MODEL, D_MODEL)), const((1, D_MODEL)), const((D_MODEL, 2 * D_FF)),
                  const((D_FF, D_MODEL))],
        out_specs=rows(D_MODEL),
        out_shape=jax.ShapeDtypeStruct((n, D_MODEL), F32),
        scratch_shapes=[pltpu.VMEM((TM, D_MODEL), BF16), pltpu.VMEM((TM, D_MODEL), BF16),
                        pltpu.VMEM((TM, D_MODEL), F32)],
        compiler_params=pltpu.CompilerParams(dimension_semantics=("arbitrary",),
                                             vmem_limit_bytes=VMEM_LIMIT),
        name="out_ffn",
    )(x2, ya, yb, yc, wo, g2, wgu, wd)


def kernel(x, w_in, b_forget, conv_w, conv_b, conv_ln_g, conv_ln_b, moba_qn_g, moba_kn_g,
           fox_qn_g, fox_kn_g, rel_bias, w_out, norm1_g, norm2_g, w_gate_up, w_down):
    batch, seq, d_model = x.shape
    depth = w_in.shape[0]
    assert d_model == D_MODEL and seq % TM == 0 and seq % MOBA_BLOCK == 0 and TQ == MOBA_BLOCK
    assert w_in.shape[2] == IN_MAIN + FOX_HEADS

    hd = jnp.arange(MOBA_DIM) // HEAD_DIM
    head_mean = jnp.where(hd[:, None] == hd[None, :], 1.0 / HEAD_DIM, 0.0).astype(BF16)
    tri = (jnp.arange(TM)[:, None] >= jnp.arange(TM)[None, :]).astype(BF16)

    bias = _moba_bias(rel_bias.astype(F32))
    x2 = x.reshape(batch * seq, D_MODEL)
    for l in range(depth):
        w = jnp.pad(w_in[l], ((0, 0), (0, IN_PAD - w_in.shape[2]))).astype(BF16)
        bf = jnp.pad(b_forget[l].astype(F32), (0, LANES - FOX_HEADS)).reshape(1, LANES)
        gains = jnp.stack([jnp.tile(g[l].astype(F32), MOBA_HEADS)
                           for g in (moba_qn_g, moba_kn_g, fox_qn_g, fox_kn_g)])
        glu, qb, kb, vbT, qc, kc, vcT, ccol, crow = _in_proj(
            x2, norm1_g[l].reshape(1, D_MODEL), w, bf, gains, head_mean, tri, batch, seq)
        ya = _conv(glu, conv_w[l], conv_b[l].reshape(1, CONV_CH), conv_ln_g[l].reshape(1, CONV_CH),
                   conv_ln_b[l].reshape(1, CONV_CH), batch, seq)
        yb = _moba(qb, kb, vbT, bias, rel_bias.astype(F32), batch, seq)
        yc = _fox(qc, kc, vcT, ccol, crow, batch, seq)
        x2 = _out_ffn(x2, ya, yb, yc, w_out[l].astype(BF16), norm2_g[l].reshape(1, D_MODEL),
                      w_gate_up[l].astype(BF16), w_down[l].astype(BF16))
    return x2.reshape(batch, seq, D_MODEL)
```

```python
import functools
import math

import jax
import jax.numpy as jnp
from jax import lax
from jax.experimental import pallas as pl
from jax.experimental.pallas import tpu as pltpu

F32 = jnp.float32
BF16 = jnp.bfloat16

D_MODEL = 1024
HEAD_DIM = 64
CONV_CH = D_MODEL // 4
MOBA_DIM = 3 * D_MODEL // 8
FOX_DIM = D_MODEL - CONV_CH - MOBA_DIM
MOBA_HEADS = MOBA_DIM // HEAD_DIM
FOX_HEADS = FOX_DIM // HEAD_DIM
CONV_WIDTH = 31
LN_EPS = 1e-5
RMS_EPS = 1e-6
MOBA_BLOCK = 256
MOBA_TOPK = 3
NUM_BUCKETS = 32
MAX_DISTANCE = 128
D_FF = -(-8 * D_MODEL // (3 * 256)) * 256
NEG = -1e30

LANES = 128
SUBLANES = 8
IN_MAIN = 2 * CONV_CH + 3 * MOBA_DIM + 3 * FOX_DIM
IN_PAD = IN_MAIN + LANES

TM = 512
TQ = 256
TK = 256
HEAD_GROUP = 6
CONV_TC = 256
CONV_HALO = 32
CONV_SUB = 64
FF_CHUNKS = ((0, 768), (768, 1536), (1536, 2304), (2304, 2816))
VMEM_LIMIT = 56 * 1024 * 1024


def _split3(x):
    hi = x.astype(BF16)
    r1 = x - hi.astype(F32)
    mid = r1.astype(BF16)
    lo = (r1 - mid.astype(F32)).astype(BF16)
    return hi, mid, lo


def _dot(a, b):
    return jnp.dot(a, b, preferred_element_type=F32)


def _dot_nt(a, b):
    return lax.dot_general(a, b, (((1,), (1,)), ((), ())), preferred_element_type=F32)


def _sigmoid(x):
    return 1.0 / (1.0 + jnp.exp(-x))


def _in_proj_kernel(nt, x_ref, g1_ref, w_ref, bf_ref, gains_ref, hm_ref, tri_ref,
                    glu_ref, qb_ref, kb_ref, vbT_ref, qc_ref, kc_ref, vcT_ref, ccol_ref, crow_ref,
                    h_scr, carry_scr):
    r = pl.program_id(0)
    x = x_ref[...]
    ms = jnp.mean(x * x, axis=-1, keepdims=True)
    h_scr[...] = (x * lax.rsqrt(ms + RMS_EPS) * g1_ref[...]).astype(BF16)

    def seg(c0, c1):
        return _dot(h_scr[...], w_ref[:, c0:c1])

    def head_rms(t, gain):
        sq = t * t
        hi = sq.astype(BF16)
        lo = (sq - hi.astype(F32)).astype(BF16)
        msq = _dot(hi, hm_ref[...]) + _dot(lo, hm_ref[...])
        return t * lax.rsqrt(msq + RMS_EPS) * gain

    a = seg(0, 2 * CONV_CH)
    glu_ref[...] = a[:, :CONV_CH] * _sigmoid(a[:, CONV_CH:])

    c0 = 2 * CONV_CH
    qk = seg(c0, c0 + 2 * MOBA_DIM)
    qb_ref[...] = (head_rms(qk[:, :MOBA_DIM], gains_ref[0:1, :]) * (HEAD_DIM ** -0.5)).astype(BF16)
    kb_ref[...] = head_rms(qk[:, MOBA_DIM:], gains_ref[1:2, :]).astype(BF16)

    c0 += 2 * MOBA_DIM
    vq = seg(c0, c0 + MOBA_DIM + FOX_DIM)
    vbT_ref[0] = vq[:, :MOBA_DIM].T.astype(BF16)
    qc_ref[...] = (head_rms(vq[:, MOBA_DIM:], gains_ref[2:3, :]) * (HEAD_DIM ** -0.5)).astype(BF16)

    c0 += MOBA_DIM + FOX_DIM
    kv = seg(c0, c0 + 2 * FOX_DIM)
    kc_ref[...] = head_rms(kv[:, :FOX_DIM], gains_ref[3:4, :]).astype(BF16)
    vcT_ref[0] = kv[:, FOX_DIM:].T.astype(BF16)

    z = seg(IN_MAIN, IN_PAD) + bf_ref[...]
    logf = jnp.minimum(z, 0.0) - jnp.log1p(jnp.exp(-jnp.abs(z)))
    hi, mid, lo = _split3(logf)
    tri = tri_ref[...]
    csum = _dot(tri, hi) + _dot(tri, mid) + _dot(tri, lo)

    @pl.when(r % nt == 0)
    def _():
        carry_scr[...] = jnp.zeros_like(carry_scr)

    c = csum + carry_scr[...]
    carry_scr[...] = c[TM - 1:TM, :]
    ccol_ref[...] = c
    crow_ref[0] = c.T[:SUBLANES, :]


def _in_proj(x2, g1, w, bf, gains, hm, tri, batch, seq):
    n = x2.shape[0]
    nt = seq // TM
    const = lambda shape: pl.BlockSpec(shape, lambda r: (0,) * len(shape), pipeline_mode=pl.Buffered(1))
    rows = lambda width: pl.BlockSpec((TM, width), lambda r: (r, 0))
    tposed = lambda height: pl.BlockSpec((1, height, TM), lambda r: (r // nt, 0, r % nt))
    out_shape = (
        jax.ShapeDtypeStruct((n, CONV_CH), F32),
        jax.ShapeDtypeStruct((n, MOBA_DIM), BF16),
        jax.ShapeDtypeStruct((n, MOBA_DIM), BF16),
        jax.ShapeDtypeStruct((batch, MOBA_DIM, seq), BF16),
        jax.ShapeDtypeStruct((n, FOX_DIM), BF16),
        jax.ShapeDtypeStruct((n, FOX_DIM), BF16),
        jax.ShapeDtypeStruct((batch, FOX_DIM, seq), BF16),
        jax.ShapeDtypeStruct((n, LANES), F32),
        jax.ShapeDtypeStruct((batch, SUBLANES, seq), F32),
    )
    return pl.pallas_call(
        functools.partial(_in_proj_kernel, nt),
        grid=(n // TM,),
        in_specs=[rows(D_MODEL), const((1, D_MODEL)), const((D_MODEL, IN_PAD)), const((1, LANES)),
                  const((4, MOBA_DIM)), const((MOBA_DIM, MOBA_DIM)), const((TM, TM))],
        out_specs=(rows(CONV_CH), rows(MOBA_DIM), rows(MOBA_DIM), tposed(MOBA_DIM),
                   rows(FOX_DIM), rows(FOX_DIM), tposed(FOX_DIM), rows(LANES), tposed(SUBLANES)),
        out_shape=out_shape,
        scratch_shapes=[pltpu.VMEM((TM, D_MODEL), BF16), pltpu.VMEM((1, LANES), F32)],
        compiler_params=pltpu.CompilerParams(dimension_semantics=("arbitrary",),
                                             vmem_limit_bytes=VMEM_LIMIT),
        name="in_proj",
    )(x2, g1, w, bf, gains, hm, tri)


def _conv_kernel(glu_ref, w_ref, b_ref, lg_ref, lb_ref, o_ref, hbuf):
    t = pl.program_id(1)

    @pl.when(t == 0)
    def _():
        hbuf[0:CONV_HALO, :] = jnp.zeros((CONV_HALO, CONV_CH), F32)

    @pl.when(t > 0)
    def _():
        hbuf[0:CONV_HALO, :] = hbuf[CONV_TC:CONV_TC + CONV_HALO, :]

    hbuf[CONV_HALO:, :] = glu_ref[...]
    shift = CONV_HALO - (CONV_WIDTH - 1)
    for r0 in range(0, CONV_TC, CONV_SUB):
        acc = jnp.zeros((CONV_SUB, CONV_CH), F32) + b_ref[...]
        for j in range(CONV_WIDTH):
            acc = acc + w_ref[j:j + 1, :] * hbuf[r0 + j + shift:r0 + j + shift + CONV_SUB, :]
        mu = jnp.mean(acc, axis=-1, keepdims=True)
        d = acc - mu
        var = jnp.mean(d * d, axis=-1, keepdims=True)
        y = d * lax.rsqrt(var + LN_EPS) * lg_ref[...] + lb_ref[...]
        o_ref[r0:r0 + CONV_SUB, :] = (y * _sigmoid(y)).astype(BF16)


def _conv(glu, w, b, lg, lb, batch, seq):
    n = glu.shape[0]
    nt = seq // CONV_TC
    const = lambda shape: pl.BlockSpec(shape, lambda bb, t: (0,) * len(shape))
    return pl.pallas_call(
        _conv_kernel,
        grid=(batch, nt),
        in_specs=[pl.BlockSpec((CONV_TC, CONV_CH), lambda bb, t: (bb * nt + t, 0)),
                  const((CONV_WIDTH, CONV_CH)), const((1, CONV_CH)), const((1, CONV_CH)), const((1, CONV_CH))],
        out_specs=pl.BlockSpec((CONV_TC, CONV_CH), lambda bb, t: (bb * nt + t, 0)),
        out_shape=jax.ShapeDtypeStruct((n, CONV_CH), BF16),
        scratch_shapes=[pltpu.VMEM((CONV_HALO + CONV_TC, CONV_CH), F32)],
        compiler_params=pltpu.CompilerParams(dimension_semantics=("arbitrary", "arbitrary")),
        name="conv_module",
    )(glu, w, b, lg, lb)


def _softmax_steps(heads, score_fn, m_scr, l_scr, acc_scr, first):
    for g0 in range(0, heads, HEAD_GROUP):
        hs = range(g0, min(g0 + HEAD_GROUP, heads))
        tiles = [score_fn(h) for h in hs]
        tile_max = [jnp.max(sT, axis=0, keepdims=True) for sT, _ in tiles]
        if first:
            m_new = tile_max
        else:
            m_old = [m_scr[h:h + 1, :] for h in hs]
            m_new = [jnp.maximum(a, b) for a, b in zip(m_old, tile_max)]
        ps = [jnp.exp(sT - mn) for (sT, _), mn in zip(tiles, m_new)]
        tile_sum = [jnp.sum(p, axis=0, keepdims=True) for p in ps]
        pvs = [_dot(vT, p.astype(BF16)) for (_, vT), p in zip(tiles, ps)]
        for n, h in enumerate(hs):
            if first:
                l_scr[h:h + 1, :] = tile_sum[n]
                acc_scr[h] = pvs[n]
            else:
                alpha = jnp.exp(m_old[n] - m_new[n])
                l_scr[h:h + 1, :] = alpha * l_scr[h:h + 1, :] + tile_sum[n]
                acc_scr[h] = alpha * acc_scr[h] + pvs[n]
            m_scr[h:h + 1, :] = m_new[n]


def _store_masked_queries(q_ref, qm_scr, heads):
    lane = lax.broadcasted_iota(jnp.int32, (TQ, LANES), 1)
    for h in range(heads):
        pair, hh = divmod(h, 2)
        qp = q_ref[:, pair * LANES:(pair + 1) * LANES]
        keep = lane < HEAD_DIM if hh == 0 else lane >= HEAD_DIM
        qm_scr[h] = jnp.where(keep, qp, jnp.zeros_like(qp))


def _write_heads(o_ref, l_scr, acc_scr, heads):
    for pair in range(heads // 2):
        both = [acc_scr[h] / l_scr[h:h + 1, :] for h in (2 * pair, 2 * pair + 1)]
        o_ref[:, pair * LANES:(pair + 1) * LANES] = jnp.concatenate(both, axis=0).T.astype(BF16)


def _key_tile(k_ref, vT_ref, h, k0):
    pair = h // 2
    kj = k_ref[pl.ds(k0, TK), pair * LANES:(pair + 1) * LANES]
    vTj = vT_ref[0, h * HEAD_DIM:(h + 1) * HEAD_DIM, pl.ds(k0, TK)]
    return kj, vTj


_ATTN_SCRATCH = lambda heads: [
    pltpu.VMEM((heads, TQ, LANES), BF16),
    pltpu.VMEM((SUBLANES, TQ), F32),
    pltpu.VMEM((SUBLANES, TQ), F32),
    pltpu.VMEM((heads, HEAD_DIM, TQ), F32),
]


def _t5_bucket(dist):
    n = jnp.maximum(dist, 0)
    max_exact = NUM_BUCKETS // 2
    nf = jnp.maximum(n, 1).astype(F32)
    large = max_exact + (jnp.log(nf / max_exact) / math.log(MAX_DISTANCE / max_exact)
                         * (NUM_BUCKETS - max_exact)).astype(jnp.int32)
    large = jnp.minimum(large, NUM_BUCKETS - 1)
    return jnp.where(n < max_exact, n, large)


def _bias_kernel(bucket_ref, rb_ref, o_ref):
    h = pl.program_id(0)
    bucket = bucket_ref[...]
    acc = jnp.zeros(bucket.shape, F32)
    for b in range(NUM_BUCKETS):
        acc = jnp.where(bucket == b, rb_ref[b, h], acc)
    key = lax.broadcasted_iota(jnp.int32, bucket.shape, 0)
    qry = lax.broadcasted_iota(jnp.int32, bucket.shape, 1)
    o_ref[0] = jnp.where((key < MOBA_BLOCK) | (key - MOBA_BLOCK <= qry), acc, NEG)


def _moba_bias(rel_bias):
    key = lax.broadcasted_iota(jnp.int32, (2 * MOBA_BLOCK, TQ), 0)
    qry = lax.broadcasted_iota(jnp.int32, (2 * MOBA_BLOCK, TQ), 1)
    bucket = _t5_bucket(qry + MOBA_BLOCK - key)
    return pl.pallas_call(
        _bias_kernel,
        grid=(MOBA_HEADS,),
        in_specs=[pl.BlockSpec((2 * MOBA_BLOCK, TQ), lambda h: (0, 0)),
                  pl.BlockSpec(memory_space=pltpu.SMEM)],
        out_specs=pl.BlockSpec((1, 2 * MOBA_BLOCK, TQ), lambda h: (h, 0, 0)),
        out_shape=jax.ShapeDtypeStruct((MOBA_HEADS, 2 * MOBA_BLOCK, TQ), F32),
        compiler_params=pltpu.CompilerParams(dimension_semantics=("arbitrary",)),
        name="moba_bias",
    )(bucket, rel_bias)


def _moba_kernel(nblk, q_ref, k_ref, vT_ref, bias_ref, rb_ref, o_ref,
                 qm_scr, m_scr, l_scr, acc_scr, kmean_scr, sel_scr):
    i = pl.program_id(1)
    heads = MOBA_HEADS

    @pl.when(i == 0)
    def _():
        for n in range(nblk):
            blk = k_ref[n * MOBA_BLOCK:(n + 1) * MOBA_BLOCK, :].astype(F32)
            kmean_scr[n:n + 1, :] = jnp.mean(blk, axis=0, keepdims=True)

    _store_masked_queries(q_ref, qm_scr, heads)
    q0 = pl.multiple_of(i * TQ, TQ)
    blk_row = lax.broadcasted_iota(jnp.int32, (nblk, TQ), 0)
    lane = lax.broadcasted_iota(jnp.int32, (nblk, LANES), 1)

    for h in range(heads):
        pair, hh = divmod(h, 2)
        km = kmean_scr[:, pair * LANES:(pair + 1) * LANES]
        km = jnp.where(lane < HEAD_DIM if hh == 0 else lane >= HEAD_DIM, km, 0.0)
        km_hi, km_mid, km_lo = _split3(km)
        qm = qm_scr[h]
        gate = _dot_nt(km_hi, qm) + _dot_nt(km_mid, qm) + _dot_nt(km_lo, qm)
        cnt = jnp.zeros((nblk, TQ), jnp.int32)
        for mblk in range(nblk):
            gm = gate[mblk:mblk + 1, :]
            beats = (gm > gate) | ((gm == gate) & (mblk < blk_row))
            cnt = cnt + jnp.where(mblk < i, beats.astype(jnp.int32), 0)
        chosen = (cnt < MOBA_TOPK) & (blk_row < i)
        sel_scr[h] = jnp.where(chosen, rb_ref[NUM_BUCKETS - 1, h], NEG)

    def own_scores(h):
        kj, vTj = _key_tile(k_ref, vT_ref, h, q0)
        return _dot_nt(kj, qm_scr[h]) + bias_ref[h, MOBA_BLOCK:, :], vTj

    _softmax_steps(heads, own_scores, m_scr, l_scr, acc_scr, first=True)

    @pl.when(i >= 1)
    def _():
        k0 = pl.multiple_of((i - 1) * TK, TK)

        def prev_scores(h):
            kj, vTj = _key_tile(k_ref, vT_ref, h, k0)
            near = bias_ref[h, :MOBA_BLOCK, :] - rb_ref[NUM_BUCKETS - 1, h]
            return _dot_nt(kj, qm_scr[h]) + (near + sel_scr[h, pl.ds(i - 1, 1), :]), vTj

        _softmax_steps(heads, prev_scores, m_scr, l_scr, acc_scr, first=False)

    def far_tile(j, carry):
        k0 = pl.multiple_of(j * TK, TK)

        def far_scores(h):
            kj, vTj = _key_tile(k_ref, vT_ref, h, k0)
            return _dot_nt(kj, qm_scr[h]) + sel_scr[h, pl.ds(j, 1), :], vTj

        _softmax_steps(heads, far_scores, m_scr, l_scr, acc_scr, first=False)
        return carry

    lax.fori_loop(0, i - 1, far_tile, 0)
    _write_heads(o_ref, l_scr, acc_scr, heads)


def _moba(q, k, vT, bias, rel_bias, batch, seq):
    n = q.shape[0]
    nq = seq // TQ
    nblk = seq // MOBA_BLOCK
    return pl.pallas_call(
        functools.partial(_moba_kernel, nblk),
        grid=(batch, nq),
        in_specs=[pl.BlockSpec((TQ, MOBA_DIM), lambda b, i: (b * nq + i, 0)),
                  pl.BlockSpec((seq, MOBA_DIM), lambda b, i: (b, 0)),
                  pl.BlockSpec((1, MOBA_DIM, seq), lambda b, i: (b, 0, 0)),
                  pl.BlockSpec((MOBA_HEADS, 2 * MOBA_BLOCK, TQ), lambda b, i: (0, 0, 0)),
                  pl.BlockSpec(memory_space=pltpu.SMEM)],
        out_specs=pl.BlockSpec((TQ, MOBA_DIM), lambda b, i: (b * nq + i, 0)),
        out_shape=jax.ShapeDtypeStruct((n, MOBA_DIM), BF16),
        scratch_shapes=_ATTN_SCRATCH(MOBA_HEADS) + [pltpu.VMEM((nblk, MOBA_DIM), F32),
                                                    pltpu.VMEM((MOBA_HEADS, nblk, TQ), F32)],
        compiler_params=pltpu.CompilerParams(dimension_semantics=("arbitrary", "arbitrary"),
                                             vmem_limit_bytes=VMEM_LIMIT),
        name="moba_attention",
    )(q, k, vT, bias, rel_bias)


def _fox_kernel(q_ref, k_ref, vT_ref, ccol_ref, crow_ref, o_ref, qm_scr, m_scr, l_scr, acc_scr):
    i = pl.program_id(1)
    heads = FOX_HEADS
    _store_masked_queries(q_ref, qm_scr, heads)
    q0 = pl.multiple_of(i * TQ, TQ)
    lane = lax.broadcasted_iota(jnp.int32, (TK, LANES), 1)

    def scores(h, k0):
        kj, vTj = _key_tile(k_ref, vT_ref, h, k0)
        c_q = crow_ref[0, h:h + 1, pl.ds(q0, TQ)]
        c_k = jnp.sum(jnp.where(lane == h, ccol_ref[pl.ds(k0, TK), :], 0.0), axis=1, keepdims=True)
        return _dot_nt(kj, qm_scr[h]) + c_q - c_k, vTj

    key = lax.broadcasted_iota(jnp.int32, (TK, TQ), 0)
    qry = lax.broadcasted_iota(jnp.int32, (TK, TQ), 1)
    def diag_scores(h):
        sT, vTj = scores(h, q0)
        return jnp.where(key <= qry, sT, NEG), vTj

    _softmax_steps(heads, diag_scores, m_scr, l_scr, acc_scr, first=True)

    def past_tile(j, carry):
        k0 = pl.multiple_of(j * TK, TK)
        _softmax_steps(heads, lambda h: scores(h, k0), m_scr, l_scr, acc_scr, first=False)
        return carry

    lax.fori_loop(0, i, past_tile, 0)
    _write_heads(o_ref, l_scr, acc_scr, heads)


def _fox(q, k, vT, ccol, crow, batch, seq):
    n = q.shape[0]
    nq = seq // TQ
    return pl.pallas_call(
        _fox_kernel,
        grid=(batch, nq),
        in_specs=[pl.BlockSpec((TQ, FOX_DIM), lambda b, i: (b * nq + i, 0)),
                  pl.BlockSpec((seq, FOX_DIM), lambda b, i: (b, 0)),
                  pl.BlockSpec((1, FOX_DIM, seq), lambda b, i: (b, 0, 0)),
                  pl.BlockSpec((seq, LANES), lambda b, i: (b, 0)),
                  pl.BlockSpec((1, SUBLANES, seq), lambda b, i: (b, 0, 0))],
        out_specs=pl.BlockSpec((TQ, FOX_DIM), lambda b, i: (b * nq + i, 0)),
        out_shape=jax.ShapeDtypeStruct((n, FOX_DIM), BF16),
        scratch_shapes=_ATTN_SCRATCH(FOX_HEADS),
        compiler_params=pltpu.CompilerParams(dimension_semantics=("arbitrary", "arbitrary"),
                                             vmem_limit_bytes=VMEM_LIMIT),
        name="fox_attention",
    )(q, k, vT, ccol, crow)


def _out_ffn_kernel(x_ref, ya_ref, yb_ref, yc_ref, wo_ref, g2_ref, wgu_ref, wd_ref, o_ref,
                    mix_scr, h2_scr, acc_scr):
    mix_scr[:, 0:CONV_CH] = ya_ref[...]
    mix_scr[:, CONV_CH:CONV_CH + MOBA_DIM] = yb_ref[...]
    mix_scr[:, CONV_CH + MOBA_DIM:] = yc_ref[...]
    x1 = x_ref[...] + _dot(mix_scr[...], wo_ref[...])
    ms = jnp.mean(x1 * x1, axis=-1, keepdims=True)
    h2_scr[...] = (x1 * lax.rsqrt(ms + RMS_EPS) * g2_ref[...]).astype(BF16)
    acc_scr[...] = x1
    for c0, c1 in FF_CHUNKS:
        g = _dot(h2_scr[...], wgu_ref[:, c0:c1])
        u = _dot(h2_scr[...], wgu_ref[:, D_FF + c0:D_FF + c1])
        act = (g * _sigmoid(g) * u).astype(BF16)
        acc_scr[...] += _dot(act, wd_ref[c0:c1, :])
    o_ref[...] = acc_scr[...]


def _out_ffn(x2, ya, yb, yc, wo, g2, wgu, wd):
    n = x2.shape[0]
    const = lambda shape: pl.BlockSpec(shape, lambda r: (0,) * len(shape), pipeline_mode=pl.Buffered(1))
    rows = lambda width: pl.BlockSpec((TM, width), lambda r: (r, 0))
    return pl.pallas_call(
        _out_ffn_kernel,
        grid=(n // TM,),
        in_specs=[rows(D_MODEL), rows(CONV_CH), rows(MOBA_DIM), rows(FOX_DIM),
                  const((D_MODEL, D_MODEL)), const((1, D_MODEL)), const((D_MODEL, 2 * D_FF)),
                  const((D_FF, D_MODEL))],
        out_specs=rows(D_MODEL),
        out_shape=jax.ShapeDtypeStruct((n, D_MODEL), F32),
        scratch_shapes=[pltpu.VMEM((TM, D_MODEL), BF16), pltpu.VMEM((TM, D_MODEL), BF16),
                        pltpu.VMEM((TM, D_MODEL), F32)],
        compiler_params=pltpu.CompilerParams(dimension_semantics=("arbitrary",),
                                             vmem_limit_bytes=VMEM_LIMIT),
        name="out_ffn",
    )(x2, ya, yb, yc, wo, g2, wgu, wd)


def kernel(x, w_in, b_forget, conv_w, conv_b, conv_ln_g, conv_ln_b, moba_qn_g, moba_kn_g,
           fox_qn_g, fox_kn_g, rel_bias, w_out, norm1_g, norm2_g, w_gate_up, w_down):
    batch, seq, d_model = x.shape
    depth = w_in.shape[0]
    assert d_model == D_MODEL and seq % TM == 0 and seq % MOBA_BLOCK == 0 and TQ == MOBA_BLOCK
    assert w_in.shape[2] == IN_MAIN + FOX_HEADS and max(MOBA_HEADS, FOX_HEADS) <= SUBLANES

    hd = jnp.arange(MOBA_DIM) // HEAD_DIM
    head_mean = jnp.where(hd[:, None] == hd[None, :], 1.0 / HEAD_DIM, 0.0).astype(BF16)
    tri = (jnp.arange(TM)[:, None] >= jnp.arange(TM)[None, :]).astype(BF16)

    bias = _moba_bias(rel_bias.astype(F32))
    x2 = x.reshape(batch * seq, D_MODEL)
    for l in range(depth):
        w = jnp.pad(w_in[l], ((0, 0), (0, IN_PAD - w_in.shape[2]))).astype(BF16)
        bf = jnp.pad(b_forget[l].astype(F32), (0, LANES - FOX_HEADS)).reshape(1, LANES)
        gains = jnp.stack([jnp.tile(g[l].astype(F32), MOBA_HEADS)
                           for g in (moba_qn_g, moba_kn_g, fox_qn_g, fox_kn_g)])
        glu, qb, kb, vbT, qc, kc, vcT, ccol, crow = _in_proj(
            x2, norm1_g[l].reshape(1, D_MODEL), w, bf, gains, head_mean, tri, batch, seq)
        ya = _conv(glu, conv_w[l], conv_b[l].reshape(1, CONV_CH), conv_ln_g[l].reshape(1, CONV_CH),
                   conv_ln_b[l].reshape(1, CONV_CH), batch, seq)
        yb = _moba(qb, kb, vbT, bias, rel_bias.astype(F32), batch, seq)
        yc = _fox(qc, kc, vcT, ccol, crow, batch, seq)
        x2 = _out_ffn(x2, ya, yb, yc, w_out[l].astype(BF16), norm2_g[l].reshape(1, D_MODEL),
                      w_gate_up[l].astype(BF16), w_down[l].astype(BF16))
    return x2.reshape(batch, seq, D_MODEL)
```

```python
import functools
import math

import numpy as np
import jax
import jax.numpy as jnp
from jax import lax
from jax.experimental import pallas as pl
from jax.experimental.pallas import tpu as pltpu

F32 = jnp.float32
BF16 = jnp.bfloat16

D_MODEL = 1024
HEAD_DIM = 64
CONV_CH = D_MODEL // 4
MOBA_DIM = 3 * D_MODEL // 8
FOX_DIM = D_MODEL - CONV_CH - MOBA_DIM
MOBA_HEADS = MOBA_DIM // HEAD_DIM
FOX_HEADS = FOX_DIM // HEAD_DIM
HEADS = MOBA_HEADS
CONV_WIDTH = 31
LN_EPS = 1e-5
RMS_EPS = 1e-6
MOBA_BLOCK = 256
MOBA_TOPK = 3
NUM_BUCKETS = 32
MAX_DISTANCE = 128
D_FF = -(-8 * D_MODEL // (3 * 256)) * 256
NEG = -1e30
LOG2E = math.log2(math.e)
QSCALE = HEAD_DIM ** -0.5 * LOG2E

LANES = 128
SUBLANES = 8
BF16_ROWS = 16
IN_MAIN = 2 * CONV_CH + 3 * MOBA_DIM + 3 * FOX_DIM
IN_PAD = IN_MAIN + LANES
AUG = HEADS * LANES
SEL_LANES = 16
V_ROWS = HEAD_DIM + BF16_ROWS

TM = 512
TQ = 256
TK = 256
HEAD_GROUP = 6
FAR_MERGE = 4
LAZY_LIMIT = 64.0
CONV_TC = 256
CONV_HALO = 32
CONV_SUB = 64
FF_CHUNKS = ((0, 768), (768, 1536), (1536, 2304), (2304, 2816))
VMEM_LIMIT = 56 * 1024 * 1024


def _split3(x):
    hi = x.astype(BF16)
    r1 = x - hi.astype(F32)
    mid = r1.astype(BF16)
    lo = (r1 - mid.astype(F32)).astype(BF16)
    return hi, mid, lo


def _dot(a, b):
    return jnp.dot(a, b, preferred_element_type=F32)


def _dot_nt(a, b):
    return lax.dot_general(a, b, (((1,), (1,)), ((), ())), preferred_element_type=F32)


def _sigmoid(x):
    return 1.0 / (1.0 + jnp.exp(-x))


def _group(h):
    return slice(h * LANES, (h + 1) * LANES)


def _placement_constants():
    spread = np.zeros((MOBA_DIM, AUG), np.float32)
    for h in range(HEADS):
        for d in range(HEAD_DIM):
            spread[h * HEAD_DIM + d, h * LANES + d] = 1.0
    cq = np.zeros((3, LANES, AUG), np.float32)
    ck = np.zeros((3, LANES, AUG), np.float32)
    ones_q = np.zeros((1, AUG), np.float32)
    ones_k = np.zeros((1, AUG), np.float32)
    for h in range(FOX_HEADS):
        for j in range(3):
            cq[j, h, h * LANES + HEAD_DIM + j] = 1.0
            ones_k[0, h * LANES + HEAD_DIM + j] = 1.0
            ck[j, h, h * LANES + HEAD_DIM + 3 + j] = -1.0
            ones_q[0, h * LANES + HEAD_DIM + 3 + j] = 1.0
    return (jnp.asarray(spread, BF16), jnp.asarray(cq, BF16), jnp.asarray(ck, BF16),
            jnp.asarray(ones_q), jnp.asarray(ones_k))


def _in_proj_kernel(nt, x_ref, g1_ref, w_ref, bf_ref, gains_ref, hm_ref, tri_ref,
                    spread_ref, cq_ref, ck_ref, onesq_ref, onesk_ref,
                    glu_ref, qb_ref, kb_ref, vbT_ref, qc_ref, kc_ref, vcT_ref,
                    h_scr, carry_scr):
    r = pl.program_id(0)
    x = x_ref[...]
    ms = jnp.mean(x * x, axis=-1, keepdims=True)
    h_scr[...] = (x * lax.rsqrt(ms + RMS_EPS) * g1_ref[...]).astype(BF16)

    def seg(c0, c1):
        return _dot(h_scr[...], w_ref[:, c0:c1])

    def head_rms(t, gain):
        sq = t * t
        hi = sq.astype(BF16)
        lo = (sq - hi.astype(F32)).astype(BF16)
        msq = _dot(hi, hm_ref[...]) + _dot(lo, hm_ref[...])
        return t * lax.rsqrt(msq + RMS_EPS) * gain

    def spread(t):
        return _dot(t.astype(BF16), spread_ref[...])

    def store_vT(ref, v):
        vT = v.T.astype(BF16)
        row = lax.broadcasted_iota(jnp.int32, (BF16_ROWS, TM), 0)
        tail = jnp.where(row == 0, 1.0, 0.0).astype(BF16)
        for h in range(HEADS):
            ref[0, h * V_ROWS:h * V_ROWS + HEAD_DIM, :] = vT[h * HEAD_DIM:(h + 1) * HEAD_DIM, :]
            ref[0, h * V_ROWS + HEAD_DIM:(h + 1) * V_ROWS, :] = tail

    a = seg(0, 2 * CONV_CH)
    glu_ref[...] = a[:, :CONV_CH] * _sigmoid(a[:, CONV_CH:])

    c0 = 2 * CONV_CH
    qk = seg(c0, c0 + 2 * MOBA_DIM)
    qb_ref[...] = spread(head_rms(qk[:, :MOBA_DIM], gains_ref[0:1, :]) * QSCALE).astype(BF16)
    lane = lax.broadcasted_iota(jnp.int32, (TM, AUG), 1)
    pos = (r % nt) * TM + lax.broadcasted_iota(jnp.int32, (TM, AUG), 0)
    in_sel = ((lane & (LANES - 1)) >= HEAD_DIM) & ((lane & (LANES - 1)) < HEAD_DIM + 3 * SEL_LANES)
    onehot = jnp.where(in_sel & ((lane & (SEL_LANES - 1)) == pos // MOBA_BLOCK), 1.0, 0.0)
    kb_ref[...] = (spread(head_rms(qk[:, MOBA_DIM:], gains_ref[1:2, :])) + onehot).astype(BF16)

    c0 += 2 * MOBA_DIM
    vq = seg(c0, c0 + MOBA_DIM + FOX_DIM)
    store_vT(vbT_ref, vq[:, :MOBA_DIM])
    qc = spread(head_rms(vq[:, MOBA_DIM:], gains_ref[2:3, :]) * QSCALE)

    c0 += MOBA_DIM + FOX_DIM
    kv = seg(c0, c0 + 2 * FOX_DIM)
    kc = spread(head_rms(kv[:, :FOX_DIM], gains_ref[3:4, :]))
    store_vT(vcT_ref, kv[:, FOX_DIM:])

    z = seg(IN_MAIN, IN_PAD) + bf_ref[...]
    logf = jnp.minimum(z, 0.0) - jnp.log1p(jnp.exp(-jnp.abs(z)))
    hi, mid, lo = _split3(logf)
    tri = tri_ref[...]
    csum = _dot(tri, hi) + _dot(tri, mid) + _dot(tri, lo)

    @pl.when(r % nt == 0)
    def _():
        carry_scr[...] = jnp.zeros_like(carry_scr)

    c = csum + carry_scr[...]
    carry_scr[...] = c[TM - 1:TM, :]
    terms = _split3(c * LOG2E)
    qc_ref[...] = (qc + sum(_dot(t, cq_ref[j]) for j, t in enumerate(terms)) + onesq_ref[...]).astype(BF16)
    kc_ref[...] = (kc + sum(_dot(t, ck_ref[j]) for j, t in enumerate(terms)) + onesk_ref[...]).astype(BF16)


def _in_proj(x2, g1, w, bf, gains, hm, tri, placement, batch, seq):
    n = x2.shape[0]
    nt = seq // TM
    const = lambda shape: pl.BlockSpec(shape, lambda r: (0,) * len(shape), pipeline_mode=pl.Buffered(1))
    rows = lambda width: pl.BlockSpec((TM, width), lambda r: (r, 0))
    tposed = lambda height: pl.BlockSpec((1, height, TM), lambda r: (r // nt, 0, r % nt))
    out_shape = (
        jax.ShapeDtypeStruct((n, CONV_CH), F32),
        jax.ShapeDtypeStruct((n, AUG), BF16),
        jax.ShapeDtypeStruct((n, AUG), BF16),
        jax.ShapeDtypeStruct((batch, HEADS * V_ROWS, seq), BF16),
        jax.ShapeDtypeStruct((n, AUG), BF16),
        jax.ShapeDtypeStruct((n, AUG), BF16),
        jax.ShapeDtypeStruct((batch, HEADS * V_ROWS, seq), BF16),
    )
    return pl.pallas_call(
        functools.partial(_in_proj_kernel, nt),
        grid=(n // TM,),
        in_specs=[rows(D_MODEL), const((1, D_MODEL)), const((D_MODEL, IN_PAD)), const((1, LANES)),
                  const((4, MOBA_DIM)), const((MOBA_DIM, MOBA_DIM)), const((TM, TM)),
                  const((MOBA_DIM, AUG)), const((3, LANES, AUG)), const((3, LANES, AUG)),
                  const((1, AUG)), const((1, AUG))],
        out_specs=(rows(CONV_CH), rows(AUG), rows(AUG), tposed(HEADS * V_ROWS),
                   rows(AUG), rows(AUG), tposed(HEADS * V_ROWS)),
        out_shape=out_shape,
        scratch_shapes=[pltpu.VMEM((TM, D_MODEL), BF16), pltpu.VMEM((1, LANES), F32)],
        compiler_params=pltpu.CompilerParams(dimension_semantics=("arbitrary",),
                                             vmem_limit_bytes=VMEM_LIMIT),
        name="in_proj",
    )(x2, g1, w, bf, gains, hm, tri, *placement)


def _conv_kernel(glu_ref, w_ref, b_ref, lg_ref, lb_ref, o_ref, hbuf):
    t = pl.program_id(1)

    @pl.when(t == 0)
    def _():
        hbuf[0:CONV_HALO, :] = jnp.zeros((CONV_HALO, CONV_CH), F32)

    @pl.when(t > 0)
    def _():
        hbuf[0:CONV_HALO, :] = hbuf[CONV_TC:CONV_TC + CONV_HALO, :]

    hbuf[CONV_HALO:, :] = glu_ref[...]
    shift = CONV_HALO - (CONV_WIDTH - 1)
    for r0 in range(0, CONV_TC, CONV_SUB):
        acc = jnp.zeros((CONV_SUB, CONV_CH), F32) + b_ref[...]
        for j in range(CONV_WIDTH):
            acc = acc + w_ref[j:j + 1, :] * hbuf[r0 + j + shift:r0 + j + shift + CONV_SUB, :]
        mu = jnp.mean(acc, axis=-1, keepdims=True)
        d = acc - mu
        var = jnp.mean(d * d, axis=-1, keepdims=True)
        y = d * lax.rsqrt(var + LN_EPS) * lg_ref[...] + lb_ref[...]
        o_ref[r0:r0 + CONV_SUB, :] = (y * _sigmoid(y)).astype(BF16)


def _conv(glu, w, b, lg, lb, batch, seq):
    n = glu.shape[0]
    nt = seq // CONV_TC
    const = lambda shape: pl.BlockSpec(shape, lambda bb, t: (0,) * len(shape))
    return pl.pallas_call(
        _conv_kernel,
        grid=(batch, nt),
        in_specs=[pl.BlockSpec((CONV_TC, CONV_CH), lambda bb, t: (bb * nt + t, 0)),
                  const((CONV_WIDTH, CONV_CH)), const((1, CONV_CH)), const((1, CONV_CH)), const((1, CONV_CH))],
        out_specs=pl.BlockSpec((CONV_TC, CONV_CH), lambda bb, t: (bb * nt + t, 0)),
        out_shape=jax.ShapeDtypeStruct((n, CONV_CH), BF16),
        scratch_shapes=[pltpu.VMEM((CONV_HALO + CONV_TC, CONV_CH), F32)],
        compiler_params=pltpu.CompilerParams(dimension_semantics=("arbitrary", "arbitrary")),
        name="conv_module",
    )(glu, w, b, lg, lb)


def _softmax_steps(score_fn, m_scr, acc_scr, first):
    for g0 in range(0, HEADS, HEAD_GROUP):
        hs = range(g0, min(g0 + HEAD_GROUP, HEADS))
        tiles = [score_fn(h) for h in hs]
        tile_max = [jnp.max(sT, axis=0, keepdims=True) for sT, _ in tiles]
        if first:
            m_new = tile_max
        else:
            m_old = [m_scr[h:h + 1, :] for h in hs]
            m_new = [jnp.maximum(a, b) for a, b in zip(m_old, tile_max)]
        pvs = [_dot(vT, jnp.exp2(sT - mn).astype(BF16)) for (sT, vT), mn in zip(tiles, m_new)]
        for n, h in enumerate(hs):
            if first:
                acc_scr[h] = pvs[n]
            else:
                acc_scr[h] = jnp.exp2(m_old[n] - m_new[n]) * acc_scr[h] + pvs[n]
            m_scr[h:h + 1, :] = m_new[n]


def _softmax_steps_lazy(score_fn, m_scr, acc_scr, pv_scr):
    worst = None
    tiles = [score_fn(h) for h in range(HEADS)]
    for h in range(HEADS):
        sT, vT = tiles[h]
        m_old = m_scr[h:h + 1, :]
        tile_max = jnp.max(sT, axis=0, keepdims=True)
        pv_scr[h] = _dot(vT, jnp.exp2(sT - m_old).astype(BF16))
        m_scr[SUBLANES + h:SUBLANES + h + 1, :] = tile_max
        rise = tile_max - m_old
        worst = rise if worst is None else jnp.maximum(worst, rise)
    safe = jnp.max(worst) <= LAZY_LIMIT

    @pl.when(safe)
    def _():
        for h in range(HEADS):
            m_old = m_scr[h:h + 1, :]
            m_new = jnp.maximum(m_old, m_scr[SUBLANES + h:SUBLANES + h + 1, :])
            acc_scr[h] = jnp.exp2(m_old - m_new) * (acc_scr[h] + pv_scr[h])
            m_scr[h:h + 1, :] = m_new

    @pl.when(jnp.logical_not(safe))
    def _():
        _softmax_steps(score_fn, m_scr, acc_scr, first=False)


def _write_heads(o_ref, acc_scr):
    for pair in range(HEADS // 2):
        both = []
        for h in (2 * pair, 2 * pair + 1):
            acc = acc_scr[h]
            both.append(acc[:HEAD_DIM, :] / acc[HEAD_DIM:HEAD_DIM + 1, :])
        o_ref[:, pair * LANES:(pair + 1) * LANES] = jnp.concatenate(both, axis=0).T.astype(BF16)


def _key_tile(k_ref, vT_ref, h, k0, tk=TK):
    return k_ref[pl.ds(k0, tk), _group(h)], vT_ref[0, h * V_ROWS:(h + 1) * V_ROWS, pl.ds(k0, tk)]


def _past_tiles(count, step):
    big = FAR_MERGE * TK

    def body(j, carry):
        step(pl.multiple_of(j * big, big), big)
        return carry

    lax.fori_loop(0, count // FAR_MERGE, body, 0)
    done = (count // FAR_MERGE) * FAR_MERGE
    part = FAR_MERGE // 2
    while part >= 1:
        rem = count - done

        @pl.when(rem >= part)
        def _(done=done, part=part):
            step(pl.multiple_of(done * TK, TK), part * TK)

        done = done + jnp.where(rem >= part, part, 0)
        part //= 2


_ATTN_SCRATCH = [
    pltpu.VMEM((2 * SUBLANES, TQ), F32),
    pltpu.VMEM((HEADS, V_ROWS, TQ), F32),
    pltpu.VMEM((HEADS, V_ROWS, TQ), F32),
]


def _attention_specs(batch, seq):
    nq = seq // TQ
    in_specs = [pl.BlockSpec((TQ, AUG), lambda b, i: (b * nq + i, 0)),
                pl.BlockSpec((seq, AUG), lambda b, i: (b, 0)),
                pl.BlockSpec((1, HEADS * V_ROWS, seq), lambda b, i: (b, 0, 0))]
    out_spec = pl.BlockSpec((TQ, MOBA_DIM), lambda b, i: (b * nq + i, 0))
    params = pltpu.CompilerParams(dimension_semantics=("arbitrary", "arbitrary"), vmem_limit_bytes=VMEM_LIMIT)
    return (batch, nq), in_specs, out_spec, params


def _t5_bucket(dist):
    n = jnp.maximum(dist, 0)
    max_exact = NUM_BUCKETS // 2
    nf = jnp.maximum(n, 1).astype(F32)
    large = max_exact + (jnp.log(nf / max_exact) / math.log(MAX_DISTANCE / max_exact)
                         * (NUM_BUCKETS - max_exact)).astype(jnp.int32)
    large = jnp.minimum(large, NUM_BUCKETS - 1)
    return jnp.where(n < max_exact, n, large)


def _bias_kernel(bucket_ref, rb_ref, o_ref):
    h = pl.program_id(0)
    bucket = bucket_ref[...]
    acc = jnp.zeros(bucket.shape, F32)
    for b in range(NUM_BUCKETS):
        acc = jnp.where(bucket == b, rb_ref[b, h], acc)
    key = lax.broadcasted_iota(jnp.int32, bucket.shape, 0)
    qry = lax.broadcasted_iota(jnp.int32, bucket.shape, 1)
    o_ref[0] = jnp.where((key < MOBA_BLOCK) | (key - MOBA_BLOCK <= qry), acc * LOG2E, NEG)


def _moba_bias(rel_bias):
    key = lax.broadcasted_iota(jnp.int32, (2 * MOBA_BLOCK, TQ), 0)
    qry = lax.broadcasted_iota(jnp.int32, (2 * MOBA_BLOCK, TQ), 1)
    bucket = _t5_bucket(qry + MOBA_BLOCK - key)
    return pl.pallas_call(
        _bias_kernel,
        grid=(MOBA_HEADS,),
        in_specs=[pl.BlockSpec((2 * MOBA_BLOCK, TQ), lambda h: (0, 0)),
                  pl.BlockSpec(memory_space=pltpu.SMEM)],
        out_specs=pl.BlockSpec((1, 2 * MOBA_BLOCK, TQ), lambda h: (h, 0, 0)),
        out_shape=jax.ShapeDtypeStruct((MOBA_HEADS, 2 * MOBA_BLOCK, TQ), F32),
        compiler_params=pltpu.CompilerParams(dimension_semantics=("arbitrary",)),
        name="moba_bias",
    )(bucket, rel_bias)


def _moba_kernel(nblk, q_ref, k_ref, vT_ref, bias_ref, rb_ref, o_ref, m_scr, acc_scr, pv_scr, qm_scr, kmean_scr):
    i = pl.program_id(1)

    @pl.when(i == 0)
    def _():
        kmean_scr[...] = jnp.zeros_like(kmean_scr)
        for n in range(nblk):
            blk = k_ref[n * MOBA_BLOCK:(n + 1) * MOBA_BLOCK, :].astype(F32)
            kmean_scr[n:n + 1, :] = jnp.mean(blk, axis=0, keepdims=True)

    q0 = pl.multiple_of(i * TQ, TQ)
    blk_row = lax.broadcasted_iota(jnp.int32, (SEL_LANES, TQ), 0)

    for h in range(HEADS):
        qg = q_ref[:, _group(h)]
        km_hi, km_mid, km_lo = _split3(kmean_scr[:, _group(h)])
        gate = _dot_nt(km_hi, qg) + _dot_nt(km_mid, qg) + _dot_nt(km_lo, qg)
        cnt = jnp.zeros((SEL_LANES, TQ), jnp.int32)
        for mblk in range(nblk):
            gm = gate[mblk:mblk + 1, :]
            beats = (gm > gate) | ((gm == gate) & (mblk < blk_row))
            cnt = cnt + jnp.where(mblk < i, beats.astype(jnp.int32), 0)
        chosen = (cnt < MOBA_TOPK) & (blk_row < i)
        far = jnp.where(blk_row == i - 1, 0.0, rb_ref[NUM_BUCKETS - 1, h] * LOG2E)
        term = jnp.where(chosen, far, jnp.where(blk_row < i, NEG, 0.0))
        t_hi, t_mid, t_lo = _split3(term)
        lanes = jnp.concatenate(
            [jnp.zeros((HEAD_DIM, TQ), F32), t_hi.astype(F32), t_mid.astype(F32), t_lo.astype(F32),
             jnp.zeros((LANES - HEAD_DIM - 3 * SEL_LANES, TQ), F32)], axis=0)
        qm_scr[h] = (qg.astype(F32) + lanes.T).astype(BF16)

    def own_scores(h):
        kj, vTj = _key_tile(k_ref, vT_ref, h, q0)
        return _dot_nt(kj, qm_scr[h]) + bias_ref[h, MOBA_BLOCK:, :], vTj

    _softmax_steps(own_scores, m_scr, acc_scr, first=True)

    @pl.when(i >= 1)
    def _():
        k0 = pl.multiple_of((i - 1) * TK, TK)

        def prev_scores(h):
            kj, vTj = _key_tile(k_ref, vT_ref, h, k0)
            return _dot_nt(kj, qm_scr[h]) + bias_ref[h, :MOBA_BLOCK, :], vTj

        _softmax_steps_lazy(prev_scores, m_scr, acc_scr, pv_scr)

    def far_step(k0, tk):
        def far_scores(h):
            kj, vTj = _key_tile(k_ref, vT_ref, h, k0, tk)
            return _dot_nt(kj, qm_scr[h]), vTj

        _softmax_steps_lazy(far_scores, m_scr, acc_scr, pv_scr)

    _past_tiles(jnp.maximum(i - 1, 0), far_step)
    _write_heads(o_ref, acc_scr)


def _moba(q, k, vT, bias, rel_bias, batch, seq):
    nblk = seq // MOBA_BLOCK
    grid, in_specs, out_spec, params = _attention_specs(batch, seq)
    return pl.pallas_call(
        functools.partial(_moba_kernel, nblk),
        grid=grid,
        in_specs=in_specs + [pl.BlockSpec((MOBA_HEADS, 2 * MOBA_BLOCK, TQ), lambda b, i: (0, 0, 0)),
                             pl.BlockSpec(memory_space=pltpu.SMEM)],
        out_specs=out_spec,
        out_shape=jax.ShapeDtypeStruct((q.shape[0], MOBA_DIM), BF16),
        scratch_shapes=_ATTN_SCRATCH + [pltpu.VMEM((HEADS, TQ, LANES), BF16),
                                        pltpu.VMEM((SEL_LANES, AUG), F32)],
        compiler_params=params,
        name="moba_attention",
    )(q, k, vT, bias, rel_bias)


def _fox_kernel(q_ref, k_ref, vT_ref, o_ref, m_scr, acc_scr, pv_scr):
    i = pl.program_id(1)
    q0 = pl.multiple_of(i * TQ, TQ)

    def scores(h, k0, tk=TK):
        kj, vTj = _key_tile(k_ref, vT_ref, h, k0, tk)
        return _dot_nt(kj, q_ref[:, _group(h)]), vTj

    key = lax.broadcasted_iota(jnp.int32, (TK, TQ), 0)
    qry = lax.broadcasted_iota(jnp.int32, (TK, TQ), 1)

    def diag_scores(h):
        sT, vTj = scores(h, q0)
        return jnp.where(key <= qry, sT, NEG), vTj

    _softmax_steps(diag_scores, m_scr, acc_scr, first=True)

    def past_step(k0, tk):
        _softmax_steps_lazy(lambda h: scores(h, k0, tk), m_scr, acc_scr, pv_scr)

    _past_tiles(i, past_step)
    _write_heads(o_ref, acc_scr)


def _fox(q, k, vT, batch, seq):
    grid, in_specs, out_spec, params = _attention_specs(batch, seq)
    return pl.pallas_call(
        _fox_kernel,
        grid=grid,
        in_specs=in_specs,
        out_specs=out_spec,
        out_shape=jax.ShapeDtypeStruct((q.shape[0], FOX_DIM), BF16),
        scratch_shapes=_ATTN_SCRATCH,
        compiler_params=params,
        name="fox_attention",
    )(q, k, vT)


def _out_ffn_kernel(x_ref, ya_ref, yb_ref, yc_ref, wo_ref, g2_ref, wgu_ref, wd_ref, o_ref,
                    mix_scr, h2_scr, acc_scr):
    mix_scr[:, 0:CONV_CH] = ya_ref[...]
    mix_scr[:, CONV_CH:CONV_CH + MOBA_DIM] = yb_ref[...]
    mix_scr[:, CONV_CH + MOBA_DIM:] = yc_ref[...]
    x1 = x_ref[...] + _dot(mix_scr[...], wo_ref[...])
    ms = jnp.mean(x1 * x1, axis=-1, keepdims=True)
    h2_scr[...] = (x1 * lax.rsqrt(ms + RMS_EPS) * g2_ref[...]).astype(BF16)
    acc_scr[...] = x1
    for c0, c1 in FF_CHUNKS:
        g = _dot(h2_scr[...], wgu_ref[:, c0:c1])
        u = _dot(h2_scr[...], wgu_ref[:, D_FF + c0:D_FF + c1])
        act = (g * _sigmoid(g) * u).astype(BF16)
        acc_scr[...] += _dot(act, wd_ref[c0:c1, :])
    o_ref[...] = acc_scr[...]


def _out_ffn(x2, ya, yb, yc, wo, g2, wgu, wd):
    n = x2.shape[0]
    const = lambda shape: pl.BlockSpec(shape, lambda r: (0,) * len(shape), pipeline_mode=pl.Buffered(1))
    rows = lambda width: pl.BlockSpec((TM, width), lambda r: (r, 0))
    return pl.pallas_call(
        _out_ffn_kernel,
        grid=(n // TM,),
        in_specs=[rows(D_MODEL), rows(CONV_CH), rows(MOBA_DIM), rows(FOX_DIM),
                  const((D_MODEL, D_MODEL)), const((1, D_MODEL)), const((D_MODEL, 2 * D_FF)),
                  const((D_FF, D_MODEL))],
        out_specs=rows(D_MODEL),
        out_shape=jax.ShapeDtypeStruct((n, D_MODEL), F32),
        scratch_shapes=[pltpu.VMEM((TM, D_MODEL), BF16), pltpu.VMEM((TM, D_MODEL), BF16),
                        pltpu.VMEM((TM, D_MODEL), F32)],
        compiler_params=pltpu.CompilerParams(dimension_semantics=("arbitrary",),
                                             vmem_limit_bytes=VMEM_LIMIT),
        name="out_ffn",
    )(x2, ya, yb, yc, wo, g2, wgu, wd)


def kernel(x, w_in, b_forget, conv_w, conv_b, conv_ln_g, conv_ln_b, moba_qn_g, moba_kn_g,
           fox_qn_g, fox_kn_g, rel_bias, w_out, norm1_g, norm2_g, w_gate_up, w_down):
    batch, seq, d_model = x.shape
    depth = w_in.shape[0]
    assert d_model == D_MODEL and seq % TM == 0 and seq % MOBA_BLOCK == 0 and TQ == MOBA_BLOCK == TK
    assert w_in.shape[2] == IN_MAIN + FOX_HEADS and MOBA_HEADS == FOX_HEADS <= SUBLANES
    assert seq // MOBA_BLOCK <= SEL_LANES and HEAD_DIM + 3 * SEL_LANES <= LANES

    hd = jnp.arange(MOBA_DIM) // HEAD_DIM
    head_mean = jnp.where(hd[:, None] == hd[None, :], 1.0 / HEAD_DIM, 0.0).astype(BF16)
    tri = (jnp.arange(TM)[:, None] >= jnp.arange(TM)[None, :]).astype(BF16)
    placement = _placement_constants()

    bias = _moba_bias(rel_bias.astype(F32))
    x2 = x.reshape(batch * seq, D_MODEL)
    for l in range(depth):
        w = jnp.pad(w_in[l], ((0, 0), (0, IN_PAD - w_in.shape[2]))).astype(BF16)
        bf = jnp.pad(b_forget[l].astype(F32), (0, LANES - FOX_HEADS)).reshape(1, LANES)
        gains = jnp.stack([jnp.tile(g[l].astype(F32), HEADS)
                           for g in (moba_qn_g, moba_kn_g, fox_qn_g, fox_kn_g)])
        glu, qb, kb, vbT, qc, kc, vcT = _in_proj(
            x2, norm1_g[l].reshape(1, D_MODEL), w, bf, gains, head_mean, tri, placement, batch, seq)
        ya = _conv(glu, conv_w[l], conv_b[l].reshape(1, CONV_CH), conv_ln_g[l].reshape(1, CONV_CH),
                   conv_ln_b[l].reshape(1, CONV_CH), batch, seq)
        yb = _moba(qb, kb, vbT, bias, rel_bias.astype(F32), batch, seq)
        yc = _fox(qc, kc, vcT, batch, seq)
        x2 = _out_ffn(x2, ya, yb, yc, w_out[l].astype(BF16), norm2_g[l].reshape(1, D_MODEL),
                      w_gate_up[l].astype(BF16), w_down[l].astype(BF16))
    return x2.reshape(batch, seq, D_MODEL)
```

```python
import functools
import math

import numpy as np
import jax
import jax.numpy as jnp
from jax import lax
from jax.experimental import pallas as pl
from jax.experimental.pallas import tpu as pltpu

F32 = jnp.float32
BF16 = jnp.bfloat16

D_MODEL = 1024
HEAD_DIM = 64
CONV_CH = D_MODEL // 4
MOBA_DIM = 3 * D_MODEL // 8
FOX_DIM = D_MODEL - CONV_CH - MOBA_DIM
MOBA_HEADS = MOBA_DIM // HEAD_DIM
FOX_HEADS = FOX_DIM // HEAD_DIM
HEADS = MOBA_HEADS
CONV_WIDTH = 31
LN_EPS = 1e-5
RMS_EPS = 1e-6
MOBA_BLOCK = 256
MOBA_TOPK = 3
NUM_BUCKETS = 32
MAX_DISTANCE = 128
D_FF = -(-8 * D_MODEL // (3 * 256)) * 256
NEG = -1e30
LOG2E = math.log2(math.e)
QSCALE = HEAD_DIM ** -0.5 * LOG2E

LANES = 128
SUBLANES = 8
BF16_ROWS = 16
IN_MAIN = 2 * CONV_CH + 3 * MOBA_DIM + 3 * FOX_DIM
IN_PAD = IN_MAIN + LANES
AUG = HEADS * LANES
HM = 256
SEL_LANES = 16
V_ROWS = HEAD_DIM + BF16_ROWS

TM = 512
TQ = 256
TK = 256
HEAD_GROUP = 6
FAR_MERGE = 4
LAZY_LIMIT = 64.0
CONV_TC = 256
CONV_HALO = 32
CONV_SUB = 64
FF_CHUNKS = ((0, 768), (768, 1536), (1536, 2304), (2304, 2816))
VMEM_LIMIT = 56 * 1024 * 1024


def _split3(x):
    hi = x.astype(BF16)
    r1 = x - hi.astype(F32)
    mid = r1.astype(BF16)
    lo = (r1 - mid.astype(F32)).astype(BF16)
    return hi, mid, lo


def _dot(a, b):
    return jnp.dot(a, b, preferred_element_type=F32)


def _dot_nt(a, b):
    return lax.dot_general(a, b, (((1,), (1,)), ((), ())), preferred_element_type=F32)


def _sigmoid(x):
    return 1.0 / (1.0 + jnp.exp(-x))


def _group(h):
    return slice(h * LANES, (h + 1) * LANES)


def _placement_constants():
    cq = np.zeros((LANES, AUG), np.float32)
    ck = np.zeros((LANES, AUG), np.float32)
    ones_q = np.zeros((1, AUG), np.float32)
    ones_k = np.zeros((1, AUG), np.float32)
    for h in range(FOX_HEADS):
        for j in range(3):
            cq[j * SUBLANES + h, h * LANES + HEAD_DIM + j] = 1.0
            ones_k[0, h * LANES + HEAD_DIM + j] = 1.0
            ck[j * SUBLANES + h, h * LANES + HEAD_DIM + 3 + j] = -1.0
            ones_q[0, h * LANES + HEAD_DIM + 3 + j] = 1.0
    return jnp.asarray(cq, BF16), jnp.asarray(ck, BF16), jnp.asarray(ones_q), jnp.asarray(ones_k)


def _in_proj_columns():
    cols = list(range(2 * CONV_CH))
    values = []
    base = 2 * CONV_CH
    for width in (MOBA_DIM, FOX_DIM):
        q0, k0, v0 = base, base + width, base + 2 * width
        for h in range(HEADS):
            cols += list(range(q0 + h * HEAD_DIM, q0 + (h + 1) * HEAD_DIM))
            cols += list(range(k0 + h * HEAD_DIM, k0 + (h + 1) * HEAD_DIM))
        values += list(range(v0, v0 + width))
        base += 3 * width
    cols += values + list(range(IN_MAIN, IN_MAIN + FOX_HEADS))
    return np.asarray(cols, np.int32)


def _in_proj_kernel(nt, x_ref, g1_ref, w_ref, bf_ref, gains_ref, hm_ref, tri_ref,
                    cq_ref, ck_ref, onesq_ref, onesk_ref,
                    glu_ref, qb_ref, kb_ref, vbT_ref, qc_ref, kc_ref, vcT_ref,
                    h_scr, carry_scr):
    r = pl.program_id(0)

    @pl.when(r % nt == 0)
    def _():
        carry_scr[...] = jnp.zeros_like(carry_scr)

    x = x_ref[...]
    ms = jnp.mean(x * x, axis=-1, keepdims=True)
    h_scr[...] = (x * lax.rsqrt(ms + RMS_EPS) * g1_ref[...]).astype(BF16)
    lane = lax.broadcasted_iota(jnp.int32, (TM, LANES), 1)
    low = lane < HEAD_DIM

    def seg(c0, c1):
        return _dot(h_scr[...], w_ref[:, c0:c1])

    def store_heads(y, q_ref, k_ref, gains, upper_q, upper_k):
        sq = y * y
        hi = sq.astype(BF16)
        lo = (sq - hi.astype(F32)).astype(BF16)
        hm = hm_ref[...]
        msq = jnp.concatenate([_dot(hi[:, c:c + HM], hm) + _dot(lo[:, c:c + HM], hm) for c in range(0, AUG, HM)],
                              axis=1)
        gain = gains * jnp.where(low[0:1, :], QSCALE, 1.0)
        for h in range(HEADS):
            g = y[:, _group(h)] * lax.rsqrt(msq[:, _group(h)] + RMS_EPS) * gain
            q_ref[:, _group(h)] = jnp.where(low, g, upper_q(h)).astype(BF16)
            k_ref[:, _group(h)] = jnp.where(low, pltpu.roll(g, HEAD_DIM, axis=1), upper_k(h)).astype(BF16)

    def store_vT(ref, v):
        vT = v.T.astype(BF16)
        row = lax.broadcasted_iota(jnp.int32, (BF16_ROWS, TM), 0)
        tail = jnp.where(row == 0, 1.0, 0.0).astype(BF16)
        for h in range(HEADS):
            ref[0, h * V_ROWS:h * V_ROWS + HEAD_DIM, :] = vT[h * HEAD_DIM:(h + 1) * HEAD_DIM, :]
            ref[0, h * V_ROWS + HEAD_DIM:(h + 1) * V_ROWS, :] = tail

    c_moba, c_fox, c_val = 2 * CONV_CH, 2 * CONV_CH + AUG, 2 * CONV_CH + 2 * AUG
    a = seg(0, 2 * CONV_CH)
    z = seg(IN_MAIN, IN_PAD) + bf_ref[...]
    y_moba = seg(c_moba, c_moba + AUG)
    glu_ref[...] = a[:, :CONV_CH] * _sigmoid(a[:, CONV_CH:])

    logf = jnp.minimum(z, 0.0) - jnp.log1p(jnp.exp(-jnp.abs(z)))
    hi, mid, lo = _split3(logf)
    tri = tri_ref[...]
    csum = _dot(tri, hi) + _dot(tri, mid) + _dot(tri, lo)
    c = csum + carry_scr[...]
    carry_scr[...] = c[TM - 1:TM, :]
    y_fox = seg(c_fox, c_fox + AUG)

    pos = (r % nt) * TM + lax.broadcasted_iota(jnp.int32, (TM, LANES), 0)
    in_sel = (lane >= HEAD_DIM) & (lane < HEAD_DIM + 3 * SEL_LANES)
    onehot = jnp.where(in_sel & ((lane & (SEL_LANES - 1)) == pos // MOBA_BLOCK), 1.0, 0.0)
    store_heads(y_moba, qb_ref, kb_ref, gains_ref[0:1, :], lambda h: 0.0, lambda h: onehot)

    hi, mid, lo = _split3(jnp.where(lane < FOX_HEADS, c * LOG2E, 0.0))
    packed = (hi.astype(F32) + pltpu.roll(mid.astype(F32), SUBLANES, axis=1)
              + pltpu.roll(lo.astype(F32), 2 * SUBLANES, axis=1)).astype(BF16)
    upper_q = _dot(packed, cq_ref[...]) + onesq_ref[...]
    upper_k = _dot(packed, ck_ref[...]) + onesk_ref[...]
    v = seg(c_val, c_val + MOBA_DIM + FOX_DIM)
    store_heads(y_fox, qc_ref, kc_ref, gains_ref[1:2, :],
                lambda h: upper_q[:, _group(h)], lambda h: upper_k[:, _group(h)])
    store_vT(vbT_ref, v[:, :MOBA_DIM])
    store_vT(vcT_ref, v[:, MOBA_DIM:])


def _in_proj(x2, g1, w, bf, gains, hm, tri, placement, batch, seq):
    n = x2.shape[0]
    nt = seq // TM
    const = lambda shape: pl.BlockSpec(shape, lambda r: (0,) * len(shape), pipeline_mode=pl.Buffered(1))
    rows = lambda width: pl.BlockSpec((TM, width), lambda r: (r, 0))
    tposed = lambda height: pl.BlockSpec((1, height, TM), lambda r: (r // nt, 0, r % nt))
    out_shape = (
        jax.ShapeDtypeStruct((n, CONV_CH), F32),
        jax.ShapeDtypeStruct((n, AUG), BF16),
        jax.ShapeDtypeStruct((n, AUG), BF16),
        jax.ShapeDtypeStruct((batch, HEADS * V_ROWS, seq), BF16),
        jax.ShapeDtypeStruct((n, AUG), BF16),
        jax.ShapeDtypeStruct((n, AUG), BF16),
        jax.ShapeDtypeStruct((batch, HEADS * V_ROWS, seq), BF16),
    )
    return pl.pallas_call(
        functools.partial(_in_proj_kernel, nt),
        grid=(n // TM,),
        in_specs=[rows(D_MODEL), const((1, D_MODEL)), const((D_MODEL, IN_PAD)), const((1, LANES)),
                  const((2, LANES)), const((HM, HM)), const((TM, TM)),
                  const((LANES, AUG)), const((LANES, AUG)), const((1, AUG)), const((1, AUG))],
        out_specs=(rows(CONV_CH), rows(AUG), rows(AUG), tposed(HEADS * V_ROWS),
                   rows(AUG), rows(AUG), tposed(HEADS * V_ROWS)),
        out_shape=out_shape,
        scratch_shapes=[pltpu.VMEM((TM, D_MODEL), BF16), pltpu.VMEM((1, LANES), F32)],
        compiler_params=pltpu.CompilerParams(dimension_semantics=("arbitrary",),
                                             vmem_limit_bytes=VMEM_LIMIT),
        name="in_proj",
    )(x2, g1, w, bf, gains, hm, tri, *placement)


def _conv_kernel(glu_ref, w_ref, b_ref, lg_ref, lb_ref, o_ref, hbuf, sbuf):
    t = pl.program_id(1)

    @pl.when(t == 0)
    def _():
        hbuf[0:CONV_HALO, :] = jnp.zeros((CONV_HALO, CONV_CH), F32)

    @pl.when(t > 0)
    def _():
        hbuf[0:CONV_HALO, :] = hbuf[CONV_TC:CONV_TC + CONV_HALO, :]

    hbuf[CONV_HALO:, :] = glu_ref[...]
    shift = CONV_HALO - (CONV_WIDTH - 1)
    for rho in range(SUBLANES):
        rows = CONV_TC + CONV_HALO - (SUBLANES if rho else 0)
        sbuf[rho, 0:rows, :] = hbuf[rho:rho + rows, :]
    for r0 in range(0, CONV_TC, CONV_SUB):
        acc = jnp.zeros((CONV_SUB, CONV_CH), F32) + b_ref[...]
        for j in range(CONV_WIDTH):
            rho, base = (j + shift) % SUBLANES, (j + shift) // SUBLANES * SUBLANES
            acc = acc + w_ref[j:j + 1, :] * sbuf[rho, r0 + base:r0 + base + CONV_SUB, :]
        mu = jnp.mean(acc, axis=-1, keepdims=True)
        d = acc - mu
        var = jnp.mean(d * d, axis=-1, keepdims=True)
        y = d * lax.rsqrt(var + LN_EPS) * lg_ref[...] + lb_ref[...]
        o_ref[r0:r0 + CONV_SUB, :] = (y * _sigmoid(y)).astype(BF16)


def _conv(glu, w, b, lg, lb, batch, seq):
    n = glu.shape[0]
    nt = seq // CONV_TC
    const = lambda shape: pl.BlockSpec(shape, lambda bb, t: (0,) * len(shape))
    return pl.pallas_call(
        _conv_kernel,
        grid=(batch, nt),
        in_specs=[pl.BlockSpec((CONV_TC, CONV_CH), lambda bb, t: (bb * nt + t, 0)),
                  const((CONV_WIDTH, CONV_CH)), const((1, CONV_CH)), const((1, CONV_CH)), const((1, CONV_CH))],
        out_specs=pl.BlockSpec((CONV_TC, CONV_CH), lambda bb, t: (bb * nt + t, 0)),
        out_shape=jax.ShapeDtypeStruct((n, CONV_CH), BF16),
        scratch_shapes=[pltpu.VMEM((CONV_HALO + CONV_TC, CONV_CH), F32),
                        pltpu.VMEM((SUBLANES, CONV_HALO + CONV_TC, CONV_CH), F32)],
        compiler_params=pltpu.CompilerParams(dimension_semantics=("arbitrary", "arbitrary")),
        name="conv_module",
    )(glu, w, b, lg, lb)


def _softmax_steps(score_fn, m_scr, acc_scr, first):
    for g0 in range(0, HEADS, HEAD_GROUP):
        hs = range(g0, min(g0 + HEAD_GROUP, HEADS))
        tiles = [score_fn(h) for h in hs]
        tile_max = [jnp.max(sT, axis=0, keepdims=True) for sT, _ in tiles]
        if first:
            m_new = tile_max
        else:
            m_old = [m_scr[h:h + 1, :] for h in hs]
            m_new = [jnp.maximum(a, b) for a, b in zip(m_old, tile_max)]
        pvs = [_dot(vT, jnp.exp2(sT - mn).astype(BF16)) for (sT, vT), mn in zip(tiles, m_new)]
        for n, h in enumerate(hs):
            if first:
                acc_scr[h] = pvs[n]
            else:
                acc_scr[h] = jnp.exp2(m_old[n] - m_new[n]) * acc_scr[h] + pvs[n]
            m_scr[h:h + 1, :] = m_new[n]


def _softmax_steps_lazy(score_fn, m_scr, acc_scr, pv_scr):
    worst = None
    tiles = [score_fn(h) for h in range(HEADS)]
    for h in range(HEADS):
        sT, vT = tiles[h]
        m_old = m_scr[h:h + 1, :]
        tile_max = jnp.max(sT, axis=0, keepdims=True)
        pv_scr[h] = _dot(vT, jnp.exp2(sT - m_old).astype(BF16))
        m_scr[SUBLANES + h:SUBLANES + h + 1, :] = tile_max
        rise = tile_max - m_old
        worst = rise if worst is None else jnp.maximum(worst, rise)
    safe = jnp.max(worst) <= LAZY_LIMIT

    @pl.when(safe)
    def _():
        for h in range(HEADS):
            m_old = m_scr[h:h + 1, :]
            m_new = jnp.maximum(m_old, m_scr[SUBLANES + h:SUBLANES + h + 1, :])
            acc_scr[h] = jnp.exp2(m_old - m_new) * (acc_scr[h] + pv_scr[h])
            m_scr[h:h + 1, :] = m_new

    @pl.when(jnp.logical_not(safe))
    def _():
        _softmax_steps(score_fn, m_scr, acc_scr, first=False)


def _write_heads(o_ref, acc_scr):
    for pair in range(HEADS // 2):
        both = []
        for h in (2 * pair, 2 * pair + 1):
            acc = acc_scr[h]
            both.append(acc[:HEAD_DIM, :] / acc[HEAD_DIM:HEAD_DIM + 1, :])
        o_ref[:, pair * LANES:(pair + 1) * LANES] = jnp.concatenate(both, axis=0).T.astype(BF16)


def _key_tile(k_ref, vT_ref, h, k0, tk=TK):
    return k_ref[pl.ds(k0, tk), _group(h)], vT_ref[0, h * V_ROWS:(h + 1) * V_ROWS, pl.ds(k0, tk)]


def _past_tiles(count, step):
    big = FAR_MERGE * TK

    def body(j, carry):
        step(pl.multiple_of(j * big, big), big)
        return carry

    lax.fori_loop(0, count // FAR_MERGE, body, 0)
    done = (count // FAR_MERGE) * FAR_MERGE
    part = FAR_MERGE // 2
    while part >= 1:
        rem = count - done

        @pl.when(rem >= part)
        def _(done=done, part=part):
            step(pl.multiple_of(done * TK, TK), part * TK)

        done = done + jnp.where(rem >= part, part, 0)
        part //= 2


_ATTN_SCRATCH = [
    pltpu.VMEM((2 * SUBLANES, TQ), F32),
    pltpu.VMEM((HEADS, V_ROWS, TQ), F32),
    pltpu.VMEM((HEADS, V_ROWS, TQ), F32),
]


def _attention_specs(batch, seq):
    nq = seq // TQ
    in_specs = [pl.BlockSpec((TQ, AUG), lambda b, i: (b * nq + i, 0)),
                pl.BlockSpec((seq, AUG), lambda b, i: (b, 0)),
                pl.BlockSpec((1, HEADS * V_ROWS, seq), lambda b, i: (b, 0, 0))]
    out_spec = pl.BlockSpec((TQ, MOBA_DIM), lambda b, i: (b * nq + i, 0))
    params = pltpu.CompilerParams(dimension_semantics=("arbitrary", "arbitrary"), vmem_limit_bytes=VMEM_LIMIT)
    return (batch, nq), in_specs, out_spec, params


def _t5_bucket(dist):
    n = jnp.maximum(dist, 0)
    max_exact = NUM_BUCKETS // 2
    nf = jnp.maximum(n, 1).astype(F32)
    large = max_exact + (jnp.log(nf / max_exact) / math.log(MAX_DISTANCE / max_exact)
                         * (NUM_BUCKETS - max_exact)).astype(jnp.int32)
    large = jnp.minimum(large, NUM_BUCKETS - 1)
    return jnp.where(n < max_exact, n, large)


def _bias_kernel(bucket_ref, rb_ref, o_ref):
    h = pl.program_id(0)
    bucket = bucket_ref[...]
    acc = jnp.zeros(bucket.shape, F32)
    for b in range(NUM_BUCKETS):
        acc = jnp.where(bucket == b, rb_ref[b, h], acc)
    key = lax.broadcasted_iota(jnp.int32, bucket.shape, 0)
    qry = lax.broadcasted_iota(jnp.int32, bucket.shape, 1)
    o_ref[0] = jnp.where((key < MOBA_BLOCK) | (key - MOBA_BLOCK <= qry), acc * LOG2E, NEG)


def _moba_bias(rel_bias):
    key = lax.broadcasted_iota(jnp.int32, (2 * MOBA_BLOCK, TQ), 0)
    qry = lax.broadcasted_iota(jnp.int32, (2 * MOBA_BLOCK, TQ), 1)
    bucket = _t5_bucket(qry + MOBA_BLOCK - key)
    return pl.pallas_call(
        _bias_kernel,
        grid=(MOBA_HEADS,),
        in_specs=[pl.BlockSpec((2 * MOBA_BLOCK, TQ), lambda h: (0, 0)),
                  pl.BlockSpec(memory_space=pltpu.SMEM)],
        out_specs=pl.BlockSpec((1, 2 * MOBA_BLOCK, TQ), lambda h: (h, 0, 0)),
        out_shape=jax.ShapeDtypeStruct((MOBA_HEADS, 2 * MOBA_BLOCK, TQ), F32),
        compiler_params=pltpu.CompilerParams(dimension_semantics=("arbitrary",)),
        name="moba_bias",
    )(bucket, rel_bias)


def _moba_kernel(nblk, q_ref, k_ref, vT_ref, bias_ref, rb_ref, o_ref, m_scr, acc_scr, pv_scr, qm_scr, kmean_scr):
    i = pl.program_id(1)

    @pl.when(i == 0)
    def _():
        kmean_scr[...] = jnp.zeros_like(kmean_scr)
        for n in range(nblk):
            blk = k_ref[n * MOBA_BLOCK:(n + 1) * MOBA_BLOCK, :].astype(F32)
            kmean_scr[n:n + 1, :] = jnp.mean(blk, axis=0, keepdims=True)

    q0 = pl.multiple_of(i * TQ, TQ)
    blk_row = lax.broadcasted_iota(jnp.int32, (SEL_LANES, TQ), 0)

    for h in range(HEADS):
        qg = q_ref[:, _group(h)]
        km_hi, km_mid, km_lo = _split3(kmean_scr[:, _group(h)])
        gate = _dot_nt(km_hi, qg) + _dot_nt(km_mid, qg) + _dot_nt(km_lo, qg)
        cnt = jnp.zeros((SEL_LANES, TQ), jnp.int32)
        for mblk in range(nblk):
            gm = gate[mblk:mblk + 1, :]
            beats = (gm > gate) | ((gm == gate) & (mblk < blk_row))
            cnt = cnt + jnp.where(mblk < i, beats.astype(jnp.int32), 0)
        chosen = (cnt < MOBA_TOPK) & (blk_row < i)
        far = jnp.where(blk_row == i - 1, 0.0, rb_ref[NUM_BUCKETS - 1, h] * LOG2E)
        term = jnp.where(chosen, far, jnp.where(blk_row < i, NEG, 0.0))
        t_hi, t_mid, t_lo = _split3(term)
        lanes = jnp.concatenate(
            [jnp.zeros((HEAD_DIM, TQ), F32), t_hi.astype(F32), t_mid.astype(F32), t_lo.astype(F32),
             jnp.zeros((LANES - HEAD_DIM - 3 * SEL_LANES, TQ), F32)], axis=0)
        qm_scr[h] = (qg.astype(F32) + lanes.T).astype(BF16)

    def own_scores(h):
        kj, vTj = _key_tile(k_ref, vT_ref, h, q0)
        return _dot_nt(kj, qm_scr[h]) + bias_ref[h, MOBA_BLOCK:, :], vTj

    _softmax_steps(own_scores, m_scr, acc_scr, first=True)

    @pl.when(i >= 1)
    def _():
        k0 = pl.multiple_of((i - 1) * TK, TK)

        def prev_scores(h):
            kj, vTj = _key_tile(k_ref, vT_ref, h, k0)
            return _dot_nt(kj, qm_scr[h]) + bias_ref[h, :MOBA_BLOCK, :], vTj

        _softmax_steps_lazy(prev_scores, m_scr, acc_scr, pv_scr)

    def far_step(k0, tk):
        def far_scores(h):
            kj, vTj = _key_tile(k_ref, vT_ref, h, k0, tk)
            return _dot_nt(kj, qm_scr[h]), vTj

        _softmax_steps_lazy(far_scores, m_scr, acc_scr, pv_scr)

    _past_tiles(jnp.maximum(i - 1, 0), far_step)
    _write_heads(o_ref, acc_scr)


def _moba(q, k, vT, bias, rel_bias, batch, seq):
    nblk = seq // MOBA_BLOCK
    grid, in_specs, out_spec, params = _attention_specs(batch, seq)
    return pl.pallas_call(
        functools.partial(_moba_kernel, nblk),
        grid=grid,
        in_specs=in_specs + [pl.BlockSpec((MOBA_HEADS, 2 * MOBA_BLOCK, TQ), lambda b, i: (0, 0, 0)),
                             pl.BlockSpec(memory_space=pltpu.SMEM)],
        out_specs=out_spec,
        out_shape=jax.ShapeDtypeStruct((q.shape[0], MOBA_DIM), BF16),
        scratch_shapes=_ATTN_SCRATCH + [pltpu.VMEM((HEADS, TQ, LANES), BF16),
                                        pltpu.VMEM((SEL_LANES, AUG), F32)],
        compiler_params=params,
        name="moba_attention",
    )(q, k, vT, bias, rel_bias)


def _fox_kernel(q_ref, k_ref, vT_ref, o_ref, m_scr, acc_scr, pv_scr):
    i = pl.program_id(1)
    q0 = pl.multiple_of(i * TQ, TQ)

    def scores(h, k0, tk=TK):
        kj, vTj = _key_tile(k_ref, vT_ref, h, k0, tk)
        return _dot_nt(kj, q_ref[:, _group(h)]), vTj

    key = lax.broadcasted_iota(jnp.int32, (TK, TQ), 0)
    qry = lax.broadcasted_iota(jnp.int32, (TK, TQ), 1)

    def diag_scores(h):
        sT, vTj = scores(h, q0)
        return jnp.where(key <= qry, sT, NEG), vTj

    _softmax_steps(diag_scores, m_scr, acc_scr, first=True)

    def past_step(k0, tk):
        _softmax_steps_lazy(lambda h: scores(h, k0, tk), m_scr, acc_scr, pv_scr)

    _past_tiles(i, past_step)
    _write_heads(o_ref, acc_scr)


def _fox(q, k, vT, batch, seq):
    grid, in_specs, out_spec, params = _attention_specs(batch, seq)
    return pl.pallas_call(
        _fox_kernel,
        grid=grid,
        in_specs=in_specs,
        out_specs=out_spec,
        out_shape=jax.ShapeDtypeStruct((q.shape[0], FOX_DIM), BF16),
        scratch_shapes=_ATTN_SCRATCH,
        compiler_params=params,
        name="fox_attention",
    )(q, k, vT)


def _out_ffn_kernel(x_ref, ya_ref, yb_ref, yc_ref, wo_ref, g2_ref, wgu_ref, wd_ref, o_ref,
                    mix_scr, h2_scr, acc_scr):
    mix_scr[:, 0:CONV_CH] = ya_ref[...]
    mix_scr[:, CONV_CH:CONV_CH + MOBA_DIM] = yb_ref[...]
    mix_scr[:, CONV_CH + MOBA_DIM:] = yc_ref[...]
    x1 = x_ref[...] + _dot(mix_scr[...], wo_ref[...])
    ms = jnp.mean(x1 * x1, axis=-1, keepdims=True)
    h2_scr[...] = (x1 * lax.rsqrt(ms + RMS_EPS) * g2_ref[...]).astype(BF16)
    acc_scr[...] = x1
    for c0, c1 in FF_CHUNKS:
        g = _dot(h2_scr[...], wgu_ref[:, c0:c1])
        u = _dot(h2_scr[...], wgu_ref[:, D_FF + c0:D_FF + c1])
        act = (g * _sigmoid(g) * u).astype(BF16)
        acc_scr[...] += _dot(act, wd_ref[c0:c1, :])
    o_ref[...] = acc_scr[...]


def _out_ffn(x2, ya, yb, yc, wo, g2, wgu, wd):
    n = x2.shape[0]
    const = lambda shape: pl.BlockSpec(shape, lambda r: (0,) * len(shape), pipeline_mode=pl.Buffered(1))
    rows = lambda width: pl.BlockSpec((TM, width), lambda r: (r, 0))
    return pl.pallas_call(
        _out_ffn_kernel,
        grid=(n // TM,),
        in_specs=[rows(D_MODEL), rows(CONV_CH), rows(MOBA_DIM), rows(FOX_DIM),
                  const((D_MODEL, D_MODEL)), const((1, D_MODEL)), const((D_MODEL, 2 * D_FF)),
                  const((D_FF, D_MODEL))],
        out_specs=rows(D_MODEL),
        out_shape=jax.ShapeDtypeStruct((n, D_MODEL), F32),
        scratch_shapes=[pltpu.VMEM((TM, D_MODEL), BF16), pltpu.VMEM((TM, D_MODEL), BF16),
                        pltpu.VMEM((TM, D_MODEL), F32)],
        compiler_params=pltpu.CompilerParams(dimension_semantics=("arbitrary",),
                                             vmem_limit_bytes=VMEM_LIMIT),
        name="out_ffn",
    )(x2, ya, yb, yc, wo, g2, wgu, wd)


def kernel(x, w_in, b_forget, conv_w, conv_b, conv_ln_g, conv_ln_b, moba_qn_g, moba_kn_g,
           fox_qn_g, fox_kn_g, rel_bias, w_out, norm1_g, norm2_g, w_gate_up, w_down):
    batch, seq, d_model = x.shape
    depth = w_in.shape[0]
    assert d_model == D_MODEL and seq % TM == 0 and seq % MOBA_BLOCK == 0 and TQ == MOBA_BLOCK == TK
    assert w_in.shape[2] == IN_MAIN + FOX_HEADS and MOBA_HEADS == FOX_HEADS <= SUBLANES
    assert seq // MOBA_BLOCK <= SEL_LANES and HEAD_DIM + 3 * SEL_LANES <= LANES

    hd = jnp.arange(HM) // HEAD_DIM
    head_mean = jnp.where(hd[:, None] == hd[None, :], 1.0 / HEAD_DIM, 0.0).astype(BF16)
    tri = (jnp.arange(TM)[:, None] >= jnp.arange(TM)[None, :]).astype(BF16)
    placement = _placement_constants()
    columns = _in_proj_columns()

    bias = _moba_bias(rel_bias.astype(F32))
    x2 = x.reshape(batch * seq, D_MODEL)
    for l in range(depth):
        w = jnp.pad(w_in[l][:, columns], ((0, 0), (0, IN_PAD - w_in.shape[2]))).astype(BF16)
        bf = jnp.pad(b_forget[l].astype(F32), (0, LANES - FOX_HEADS)).reshape(1, LANES)
        gains = jnp.stack([jnp.concatenate([gq[l], gk[l]]).astype(F32)
                           for gq, gk in ((moba_qn_g, moba_kn_g), (fox_qn_g, fox_kn_g))])
        glu, qb, kb, vbT, qc, kc, vcT = _in_proj(
            x2, norm1_g[l].reshape(1, D_MODEL), w, bf, gains, head_mean, tri, placement, batch, seq)
        ya = _conv(glu, conv_w[l], conv_b[l].reshape(1, CONV_CH), conv_ln_g[l].reshape(1, CONV_CH),
                   conv_ln_b[l].reshape(1, CONV_CH), batch, seq)
        yb = _moba(qb, kb, vbT, bias, rel_bias.astype(F32), batch, seq)
        yc = _fox(qc, kc, vcT, batch, seq)
        x2 = _out_ffn(x2, ya, yb, yc, w_out[l].astype(BF16), norm2_g[l].reshape(1, D_MODEL),
                      w_gate_up[l].astype(BF16), w_down[l].astype(BF16))
    return x2.reshape(batch, seq, D_MODEL)
```

```python
import functools
import math

import numpy as np
import jax
import jax.numpy as jnp
from jax import lax
from jax.experimental import pallas as pl
from jax.experimental.pallas import tpu as pltpu

F32 = jnp.float32
BF16 = jnp.bfloat16

D_MODEL = 1024
HEAD_DIM = 64
CONV_CH = D_MODEL // 4
MOBA_DIM = 3 * D_MODEL // 8
FOX_DIM = D_MODEL - CONV_CH - MOBA_DIM
MOBA_HEADS = MOBA_DIM // HEAD_DIM
FOX_HEADS = FOX_DIM // HEAD_DIM
HEADS = MOBA_HEADS
CONV_WIDTH = 31
LN_EPS = 1e-5
RMS_EPS = 1e-6
MOBA_BLOCK = 256
MOBA_TOPK = 3
NUM_BUCKETS = 32
MAX_DISTANCE = 128
D_FF = -(-8 * D_MODEL // (3 * 256)) * 256
NEG = -1e30
LOG2E = math.log2(math.e)
QSCALE = HEAD_DIM ** -0.5 * LOG2E

LANES = 128
SUBLANES = 8
BF16_ROWS = 16
IN_MAIN = 2 * CONV_CH + 3 * MOBA_DIM + 3 * FOX_DIM
IN_PAD = IN_MAIN + LANES
AUG = HEADS * LANES
HM = 256
SEL_LANES = 16
V_ROWS = HEAD_DIM + BF16_ROWS

TM = 512
TQ = 256
TK = 256
HEAD_GROUP = 6
FAR_MERGE = 4
LAZY_LIMIT = 64.0
UNDERFLOW_LOG2 = 160.0
ROUNDING_MARGIN = 2.0
NORM_SLACK = 1.01
CONV_TC = 256
CONV_HALO = 32
CONV_SUB = 64
FF_CHUNKS = ((0, 768), (768, 1536), (1536, 2304), (2304, 2816))
VMEM_LIMIT = 56 * 1024 * 1024


def _split3(x):
    hi = x.astype(BF16)
    r1 = x - hi.astype(F32)
    mid = r1.astype(BF16)
    lo = (r1 - mid.astype(F32)).astype(BF16)
    return hi, mid, lo


def _dot(a, b):
    return jnp.dot(a, b, preferred_element_type=F32)


def _dot_nt(a, b):
    return lax.dot_general(a, b, (((1,), (1,)), ((), ())), preferred_element_type=F32)


def _sigmoid(x):
    return 1.0 / (1.0 + jnp.exp(-x))


def _group(h):
    return slice(h * LANES, (h + 1) * LANES)


def _placement_constants():
    cq = np.zeros((LANES, AUG), np.float32)
    ck = np.zeros((LANES, AUG), np.float32)
    ones_q = np.zeros((1, AUG), np.float32)
    ones_k = np.zeros((1, AUG), np.float32)
    for h in range(FOX_HEADS):
        for j in range(3):
            cq[j * SUBLANES + h, h * LANES + HEAD_DIM + j] = 1.0
            ones_k[0, h * LANES + HEAD_DIM + j] = 1.0
            ck[j * SUBLANES + h, h * LANES + HEAD_DIM + 3 + j] = -1.0
            ones_q[0, h * LANES + HEAD_DIM + 3 + j] = 1.0
    return jnp.asarray(cq, BF16), jnp.asarray(ck, BF16), jnp.asarray(ones_q), jnp.asarray(ones_k)


def _in_proj_columns():
    cols = list(range(2 * CONV_CH))
    values = []
    base = 2 * CONV_CH
    for width in (MOBA_DIM, FOX_DIM):
        q0, k0, v0 = base, base + width, base + 2 * width
        for h in range(HEADS):
            cols += list(range(q0 + h * HEAD_DIM, q0 + (h + 1) * HEAD_DIM))
            cols += list(range(k0 + h * HEAD_DIM, k0 + (h + 1) * HEAD_DIM))
        values += list(range(v0, v0 + width))
        base += 3 * width
    cols += values + list(range(IN_MAIN, IN_MAIN + FOX_HEADS))
    return np.asarray(cols, np.int32)


def _in_proj_kernel(nt, x_ref, g1_ref, w_ref, bf_ref, gains_ref, hm_ref, tri_ref,
                    cq_ref, ck_ref, onesq_ref, onesk_ref,
                    glu_ref, qb_ref, kb_ref, vbT_ref, qc_ref, kc_ref, vcT_ref, cmax_ref, cmin_ref,
                    h_scr, carry_scr):
    r = pl.program_id(0)

    @pl.when(r % nt == 0)
    def _():
        carry_scr[...] = jnp.zeros_like(carry_scr)

    x = x_ref[...]
    ms = jnp.mean(x * x, axis=-1, keepdims=True)
    h_scr[...] = (x * lax.rsqrt(ms + RMS_EPS) * g1_ref[...]).astype(BF16)
    lane = lax.broadcasted_iota(jnp.int32, (TM, LANES), 1)
    low = lane < HEAD_DIM

    def seg(c0, c1):
        return _dot(h_scr[...], w_ref[:, c0:c1])

    def store_heads(y, q_ref, k_ref, gains, upper_q, upper_k):
        sq = y * y
        hi = sq.astype(BF16)
        lo = (sq - hi.astype(F32)).astype(BF16)
        hm = hm_ref[...]
        msq = jnp.concatenate([_dot(hi[:, c:c + HM], hm) + _dot(lo[:, c:c + HM], hm) for c in range(0, AUG, HM)],
                              axis=1)
        gain = gains * jnp.where(low[0:1, :], QSCALE, 1.0)
        for h in range(HEADS):
            g = y[:, _group(h)] * lax.rsqrt(msq[:, _group(h)] + RMS_EPS) * gain
            q_ref[:, _group(h)] = jnp.where(low, g, upper_q(h)).astype(BF16)
            k_ref[:, _group(h)] = jnp.where(low, pltpu.roll(g, HEAD_DIM, axis=1), upper_k(h)).astype(BF16)

    def store_vT(ref, v):
        vT = v.T.astype(BF16)
        row = lax.broadcasted_iota(jnp.int32, (BF16_ROWS, TM), 0)
        tail = jnp.where(row == 0, 1.0, 0.0).astype(BF16)
        for h in range(HEADS):
            ref[0, h * V_ROWS:h * V_ROWS + HEAD_DIM, :] = vT[h * HEAD_DIM:(h + 1) * HEAD_DIM, :]
            ref[0, h * V_ROWS + HEAD_DIM:(h + 1) * V_ROWS, :] = tail

    c_moba, c_fox, c_val = 2 * CONV_CH, 2 * CONV_CH + AUG, 2 * CONV_CH + 2 * AUG
    a = seg(0, 2 * CONV_CH)
    z = seg(IN_MAIN, IN_PAD) + bf_ref[...]
    y_moba = seg(c_moba, c_moba + AUG)
    glu_ref[...] = a[:, :CONV_CH] * _sigmoid(a[:, CONV_CH:])

    logf = jnp.minimum(z, 0.0) - jnp.log1p(jnp.exp(-jnp.abs(z)))
    hi, mid, lo = _split3(logf)
    tri = tri_ref[...]
    csum = _dot(tri, hi) + _dot(tri, mid) + _dot(tri, lo)
    c = csum + carry_scr[...]
    carry_scr[...] = c[TM - 1:TM, :]
    y_fox = seg(c_fox, c_fox + AUG)

    pos = (r % nt) * TM + lax.broadcasted_iota(jnp.int32, (TM, LANES), 0)
    in_sel = (lane >= HEAD_DIM) & (lane < HEAD_DIM + 3 * SEL_LANES)
    onehot = jnp.where(in_sel & ((lane & (SEL_LANES - 1)) == pos // MOBA_BLOCK), 1.0, 0.0)
    store_heads(y_moba, qb_ref, kb_ref, gains_ref[0:1, :], lambda h: 0.0, lambda h: onehot)

    c2 = c * LOG2E
    for t in range(TM // TQ):
        part = c2[t * TQ:(t + 1) * TQ, :]
        cmax_ref[pl.ds(r * (TM // TQ) + t, 1), :] = jnp.max(part, axis=0, keepdims=True)
        cmin_ref[pl.ds(r * (TM // TQ) + t, 1), :] = jnp.min(part, axis=0, keepdims=True)
    hi, mid, lo = _split3(jnp.where(lane < FOX_HEADS, c2, 0.0))
    packed = (hi.astype(F32) + pltpu.roll(mid.astype(F32), SUBLANES, axis=1)
              + pltpu.roll(lo.astype(F32), 2 * SUBLANES, axis=1)).astype(BF16)
    upper_q = _dot(packed, cq_ref[...]) + onesq_ref[...]
    upper_k = _dot(packed, ck_ref[...]) + onesk_ref[...]
    v = seg(c_val, c_val + MOBA_DIM + FOX_DIM)
    store_heads(y_fox, qc_ref, kc_ref, gains_ref[1:2, :],
                lambda h: upper_q[:, _group(h)], lambda h: upper_k[:, _group(h)])
    store_vT(vbT_ref, v[:, :MOBA_DIM])
    store_vT(vcT_ref, v[:, MOBA_DIM:])


def _in_proj(x2, g1, w, bf, gains, hm, tri, placement, batch, seq):
    n = x2.shape[0]
    nt = seq // TM
    const = lambda shape: pl.BlockSpec(shape, lambda r: (0,) * len(shape), pipeline_mode=pl.Buffered(1))
    rows = lambda width: pl.BlockSpec((TM, width), lambda r: (r, 0))
    tposed = lambda height: pl.BlockSpec((1, height, TM), lambda r: (r // nt, 0, r % nt))
    out_shape = (
        jax.ShapeDtypeStruct((n, CONV_CH), F32),
        jax.ShapeDtypeStruct((n, AUG), BF16),
        jax.ShapeDtypeStruct((n, AUG), BF16),
        jax.ShapeDtypeStruct((batch, HEADS * V_ROWS, seq), BF16),
        jax.ShapeDtypeStruct((n, AUG), BF16),
        jax.ShapeDtypeStruct((n, AUG), BF16),
        jax.ShapeDtypeStruct((batch, HEADS * V_ROWS, seq), BF16),
        jax.ShapeDtypeStruct((n // TQ, LANES), F32),
        jax.ShapeDtypeStruct((n // TQ, LANES), F32),
    )
    whole = pl.BlockSpec((n // TQ, LANES), lambda r: (0, 0))
    return pl.pallas_call(
        functools.partial(_in_proj_kernel, nt),
        grid=(n // TM,),
        in_specs=[rows(D_MODEL), const((1, D_MODEL)), const((D_MODEL, IN_PAD)), const((1, LANES)),
                  const((2, LANES)), const((HM, HM)), const((TM, TM)),
                  const((LANES, AUG)), const((LANES, AUG)), const((1, AUG)), const((1, AUG))],
        out_specs=(rows(CONV_CH), rows(AUG), rows(AUG), tposed(HEADS * V_ROWS),
                   rows(AUG), rows(AUG), tposed(HEADS * V_ROWS), whole, whole),
        out_shape=out_shape,
        scratch_shapes=[pltpu.VMEM((TM, D_MODEL), BF16), pltpu.VMEM((1, LANES), F32)],
        compiler_params=pltpu.CompilerParams(dimension_semantics=("arbitrary",),
                                             vmem_limit_bytes=VMEM_LIMIT),
        name="in_proj",
    )(x2, g1, w, bf, gains, hm, tri, *placement)


def _conv_kernel(glu_ref, w_ref, b_ref, lg_ref, lb_ref, o_ref, hbuf, sbuf):
    t = pl.program_id(1)

    @pl.when(t == 0)
    def _():
        hbuf[0:CONV_HALO, :] = jnp.zeros((CONV_HALO, CONV_CH), F32)

    @pl.when(t > 0)
    def _():
        hbuf[0:CONV_HALO, :] = hbuf[CONV_TC:CONV_TC + CONV_HALO, :]

    hbuf[CONV_HALO:, :] = glu_ref[...]
    shift = CONV_HALO - (CONV_WIDTH - 1)
    for rho in range(SUBLANES):
        rows = CONV_TC + CONV_HALO - (SUBLANES if rho else 0)
        sbuf[rho, 0:rows, :] = hbuf[rho:rho + rows, :]
    for r0 in range(0, CONV_TC, CONV_SUB):
        acc = jnp.zeros((CONV_SUB, CONV_CH), F32) + b_ref[...]
        for j in range(CONV_WIDTH):
            rho, base = (j + shift) % SUBLANES, (j + shift) // SUBLANES * SUBLANES
            acc = acc + w_ref[j:j + 1, :] * sbuf[rho, r0 + base:r0 + base + CONV_SUB, :]
        mu = jnp.mean(acc, axis=-1, keepdims=True)
        d = acc - mu
        var = jnp.mean(d * d, axis=-1, keepdims=True)
        y = d * lax.rsqrt(var + LN_EPS) * lg_ref[...] + lb_ref[...]
        o_ref[r0:r0 + CONV_SUB, :] = (y * _sigmoid(y)).astype(BF16)


def _conv(glu, w, b, lg, lb, batch, seq):
    n = glu.shape[0]
    nt = seq // CONV_TC
    const = lambda shape: pl.BlockSpec(shape, lambda bb, t: (0,) * len(shape))
    return pl.pallas_call(
        _conv_kernel,
        grid=(batch, nt),
        in_specs=[pl.BlockSpec((CONV_TC, CONV_CH), lambda bb, t: (bb * nt + t, 0)),
                  const((CONV_WIDTH, CONV_CH)), const((1, CONV_CH)), const((1, CONV_CH)), const((1, CONV_CH))],
        out_specs=pl.BlockSpec((CONV_TC, CONV_CH), lambda bb, t: (bb * nt + t, 0)),
        out_shape=jax.ShapeDtypeStruct((n, CONV_CH), BF16),
        scratch_shapes=[pltpu.VMEM((CONV_HALO + CONV_TC, CONV_CH), F32),
                        pltpu.VMEM((SUBLANES, CONV_HALO + CONV_TC, CONV_CH), F32)],
        compiler_params=pltpu.CompilerParams(dimension_semantics=("arbitrary", "arbitrary")),
        name="conv_module",
    )(glu, w, b, lg, lb)


def _softmax_steps(score_fn, m_scr, acc_scr, first):
    for g0 in range(0, HEADS, HEAD_GROUP):
        hs = range(g0, min(g0 + HEAD_GROUP, HEADS))
        tiles = [score_fn(h) for h in hs]
        tile_max = [jnp.max(sT, axis=0, keepdims=True) for sT, _ in tiles]
        if first:
            m_new = tile_max
        else:
            m_old = [m_scr[h:h + 1, :] for h in hs]
            m_new = [jnp.maximum(a, b) for a, b in zip(m_old, tile_max)]
        pvs = [_dot(vT, jnp.exp2(sT - mn).astype(BF16)) for (sT, vT), mn in zip(tiles, m_new)]
        for n, h in enumerate(hs):
            if first:
                acc_scr[h] = pvs[n]
            else:
                acc_scr[h] = jnp.exp2(m_old[n] - m_new[n]) * acc_scr[h] + pvs[n]
            m_scr[h:h + 1, :] = m_new[n]


def _softmax_steps_lazy(score_fn, m_scr, acc_scr, pv_scr):
    worst = None
    tiles = [score_fn(h) for h in range(HEADS)]
    for h in range(HEADS):
        sT, vT = tiles[h]
        m_old = m_scr[h:h + 1, :]
        tile_max = jnp.max(sT, axis=0, keepdims=True)
        pv_scr[h] = _dot(vT, jnp.exp2(sT - m_old).astype(BF16))
        m_scr[SUBLANES + h:SUBLANES + h + 1, :] = tile_max
        rise = tile_max - m_old
        worst = rise if worst is None else jnp.maximum(worst, rise)
    safe = jnp.max(worst) <= LAZY_LIMIT

    @pl.when(safe)
    def _():
        for h in range(HEADS):
            m_old = m_scr[h:h + 1, :]
            m_new = jnp.maximum(m_old, m_scr[SUBLANES + h:SUBLANES + h + 1, :])
            acc_scr[h] = jnp.exp2(m_old - m_new) * (acc_scr[h] + pv_scr[h])
            m_scr[h:h + 1, :] = m_new

    @pl.when(jnp.logical_not(safe))
    def _():
        _softmax_steps(score_fn, m_scr, acc_scr, first=False)


def _write_heads(o_ref, acc_scr):
    for pair in range(HEADS // 2):
        both = []
        for h in (2 * pair, 2 * pair + 1):
            acc = acc_scr[h]
            both.append(acc[:HEAD_DIM, :] / acc[HEAD_DIM:HEAD_DIM + 1, :])
        o_ref[:, pair * LANES:(pair + 1) * LANES] = jnp.concatenate(both, axis=0).T.astype(BF16)


def _key_tile(k_ref, vT_ref, h, k0, tk=TK):
    return k_ref[pl.ds(k0, tk), _group(h)], vT_ref[0, h * V_ROWS:(h + 1) * V_ROWS, pl.ds(k0, tk)]


def _past_tiles(first, count, step):
    big = FAR_MERGE * TK

    def body(j, carry):
        step(pl.multiple_of(first * TK + j * big, TK), big)
        return carry

    lax.fori_loop(0, count // FAR_MERGE, body, 0)
    done = first + (count // FAR_MERGE) * FAR_MERGE
    count = first + count
    part = FAR_MERGE // 2
    while part >= 1:
        rem = count - done

        @pl.when(rem >= part)
        def _(done=done, part=part):
            step(pl.multiple_of(done * TK, TK), part * TK)

        done = done + jnp.where(rem >= part, part, 0)
        part //= 2


_ATTN_SCRATCH = [
    pltpu.VMEM((2 * SUBLANES, TQ), F32),
    pltpu.VMEM((HEADS, V_ROWS, TQ), F32),
    pltpu.VMEM((HEADS, V_ROWS, TQ), F32),
]


def _attention_specs(batch, seq):
    nq = seq // TQ
    in_specs = [pl.BlockSpec((TQ, AUG), lambda b, i: (b * nq + i, 0)),
                pl.BlockSpec((seq, AUG), lambda b, i: (b, 0)),
                pl.BlockSpec((1, HEADS * V_ROWS, seq), lambda b, i: (b, 0, 0))]
    out_spec = pl.BlockSpec((TQ, MOBA_DIM), lambda b, i: (b * nq + i, 0))
    params = pltpu.CompilerParams(dimension_semantics=("arbitrary", "arbitrary"), vmem_limit_bytes=VMEM_LIMIT)
    return (batch, nq), in_specs, out_spec, params


def _t5_bucket(dist):
    n = jnp.maximum(dist, 0)
    max_exact = NUM_BUCKETS // 2
    nf = jnp.maximum(n, 1).astype(F32)
    large = max_exact + (jnp.log(nf / max_exact) / math.log(MAX_DISTANCE / max_exact)
                         * (NUM_BUCKETS - max_exact)).astype(jnp.int32)
    large = jnp.minimum(large, NUM_BUCKETS - 1)
    return jnp.where(n < max_exact, n, large)


def _bias_kernel(bucket_ref, rb_ref, o_ref):
    h = pl.program_id(0)
    bucket = bucket_ref[...]
    acc = jnp.zeros(bucket.shape, F32)
    for b in range(NUM_BUCKETS):
        acc = jnp.where(bucket == b, rb_ref[b, h], acc)
    key = lax.broadcasted_iota(jnp.int32, bucket.shape, 0)
    qry = lax.broadcasted_iota(jnp.int32, bucket.shape, 1)
    o_ref[0] = jnp.where((key < MOBA_BLOCK) | (key - MOBA_BLOCK <= qry), acc * LOG2E, NEG)


def _moba_bias(rel_bias):
    key = lax.broadcasted_iota(jnp.int32, (2 * MOBA_BLOCK, TQ), 0)
    qry = lax.broadcasted_iota(jnp.int32, (2 * MOBA_BLOCK, TQ), 1)
    bucket = _t5_bucket(qry + MOBA_BLOCK - key)
    return pl.pallas_call(
        _bias_kernel,
        grid=(MOBA_HEADS,),
        in_specs=[pl.BlockSpec((2 * MOBA_BLOCK, TQ), lambda h: (0, 0)),
                  pl.BlockSpec(memory_space=pltpu.SMEM)],
        out_specs=pl.BlockSpec((1, 2 * MOBA_BLOCK, TQ), lambda h: (h, 0, 0)),
        out_shape=jax.ShapeDtypeStruct((MOBA_HEADS, 2 * MOBA_BLOCK, TQ), F32),
        compiler_params=pltpu.CompilerParams(dimension_semantics=("arbitrary",)),
        name="moba_bias",
    )(bucket, rel_bias)


def _moba_kernel(nblk, q_ref, k_ref, vT_ref, bias_ref, rb_ref, o_ref, m_scr, acc_scr, pv_scr, qm_scr, kmean_scr):
    i = pl.program_id(1)

    @pl.when(i == 0)
    def _():
        kmean_scr[...] = jnp.zeros_like(kmean_scr)
        for n in range(nblk):
            blk = k_ref[n * MOBA_BLOCK:(n + 1) * MOBA_BLOCK, :].astype(F32)
            kmean_scr[n:n + 1, :] = jnp.mean(blk, axis=0, keepdims=True)

    q0 = pl.multiple_of(i * TQ, TQ)
    blk_row = lax.broadcasted_iota(jnp.int32, (SEL_LANES, TQ), 0)

    for h in range(HEADS):
        qg = q_ref[:, _group(h)]
        km_hi, km_mid, km_lo = _split3(kmean_scr[:, _group(h)])
        gate = _dot_nt(km_hi, qg) + _dot_nt(km_mid, qg) + _dot_nt(km_lo, qg)
        cnt = jnp.zeros((SEL_LANES, TQ), jnp.int32)
        for mblk in range(nblk):
            gm = gate[mblk:mblk + 1, :]
            beats = (gm > gate) | ((gm == gate) & (mblk < blk_row))
            cnt = cnt + jnp.where(mblk < i, beats.astype(jnp.int32), 0)
        chosen = (cnt < MOBA_TOPK) & (blk_row < i)
        far = jnp.where(blk_row == i - 1, 0.0, rb_ref[NUM_BUCKETS - 1, h] * LOG2E)
        term = jnp.where(chosen, far, jnp.where(blk_row < i, NEG, 0.0))
        t_hi, t_mid, t_lo = _split3(term)
        lanes = jnp.concatenate(
            [jnp.zeros((HEAD_DIM, TQ), F32), t_hi.astype(F32), t_mid.astype(F32), t_lo.astype(F32),
             jnp.zeros((LANES - HEAD_DIM - 3 * SEL_LANES, TQ), F32)], axis=0)
        qm_scr[h] = (qg.astype(F32) + lanes.T).astype(BF16)

    def own_scores(h):
        kj, vTj = _key_tile(k_ref, vT_ref, h, q0)
        return _dot_nt(kj, qm_scr[h]) + bias_ref[h, MOBA_BLOCK:, :], vTj

    _softmax_steps(own_scores, m_scr, acc_scr, first=True)

    @pl.when(i >= 1)
    def _():
        k0 = pl.multiple_of((i - 1) * TK, TK)

        def prev_scores(h):
            kj, vTj = _key_tile(k_ref, vT_ref, h, k0)
            return _dot_nt(kj, qm_scr[h]) + bias_ref[h, :MOBA_BLOCK, :], vTj

        _softmax_steps_lazy(prev_scores, m_scr, acc_scr, pv_scr)

    def far_step(k0, tk):
        def far_scores(h):
            kj, vTj = _key_tile(k_ref, vT_ref, h, k0, tk)
            return _dot_nt(kj, qm_scr[h]), vTj

        _softmax_steps_lazy(far_scores, m_scr, acc_scr, pv_scr)

    _past_tiles(0, jnp.maximum(i - 1, 0), far_step)
    _write_heads(o_ref, acc_scr)


def _moba(q, k, vT, bias, rel_bias, batch, seq):
    nblk = seq // MOBA_BLOCK
    grid, in_specs, out_spec, params = _attention_specs(batch, seq)
    return pl.pallas_call(
        functools.partial(_moba_kernel, nblk),
        grid=grid,
        in_specs=in_specs + [pl.BlockSpec((MOBA_HEADS, 2 * MOBA_BLOCK, TQ), lambda b, i: (0, 0, 0)),
                             pl.BlockSpec(memory_space=pltpu.SMEM)],
        out_specs=out_spec,
        out_shape=jax.ShapeDtypeStruct((q.shape[0], MOBA_DIM), BF16),
        scratch_shapes=_ATTN_SCRATCH + [pltpu.VMEM((HEADS, TQ, LANES), BF16),
                                        pltpu.VMEM((SEL_LANES, AUG), F32)],
        compiler_params=params,
        name="moba_attention",
    )(q, k, vT, bias, rel_bias)


def _fox_kernel(q_ref, k_ref, vT_ref, cmax_ref, cmin_ref, bound_ref, o_ref, m_scr, acc_scr, pv_scr):
    i = pl.program_id(1)
    q0 = pl.multiple_of(i * TQ, TQ)

    ntiles = cmin_ref.shape[0]
    gap = cmax_ref[pl.ds(i, 1), :] - cmin_ref[...]
    limit = -(2.0 * bound_ref[0] + (UNDERFLOW_LOG2 + ROUNDING_MARGIN))
    head_lane = lax.broadcasted_iota(jnp.int32, (ntiles, LANES), 1) < FOX_HEADS
    live = jnp.max(jnp.where(head_lane & (gap > limit), 1, 0), axis=1, keepdims=True)
    tile = lax.broadcasted_iota(jnp.int32, (ntiles, 1), 0)
    first_live = jnp.min(jnp.where((live > 0) & (tile < i), tile, i))

    def scores(h, k0, tk=TK):
        kj, vTj = _key_tile(k_ref, vT_ref, h, k0, tk)
        return _dot_nt(kj, q_ref[:, _group(h)]), vTj

    key = lax.broadcasted_iota(jnp.int32, (TK, TQ), 0)
    qry = lax.broadcasted_iota(jnp.int32, (TK, TQ), 1)

    def diag_scores(h):
        sT, vTj = scores(h, q0)
        return jnp.where(key <= qry, sT, NEG), vTj

    _softmax_steps(diag_scores, m_scr, acc_scr, first=True)

    def past_step(k0, tk):
        _softmax_steps_lazy(lambda h: scores(h, k0, tk), m_scr, acc_scr, pv_scr)

    _past_tiles(first_live, i - first_live, past_step)
    _write_heads(o_ref, acc_scr)


def _fox(q, k, vT, cmax, cmin, bound, batch, seq):
    grid, in_specs, out_spec, params = _attention_specs(batch, seq)
    ntiles = seq // TQ
    return pl.pallas_call(
        _fox_kernel,
        grid=grid,
        in_specs=in_specs + [pl.BlockSpec((ntiles, LANES), lambda b, i: (b, 0)),
                             pl.BlockSpec((ntiles, LANES), lambda b, i: (b, 0)),
                             pl.BlockSpec(memory_space=pltpu.SMEM)],
        out_specs=out_spec,
        out_shape=jax.ShapeDtypeStruct((q.shape[0], FOX_DIM), BF16),
        scratch_shapes=_ATTN_SCRATCH,
        compiler_params=params,
        name="fox_attention",
    )(q, k, vT, cmax, cmin, bound)


def _out_ffn_kernel(x_ref, ya_ref, yb_ref, yc_ref, wo_ref, g2_ref, wgu_ref, wd_ref, o_ref,
                    mix_scr, h2_scr, acc_scr):
    mix_scr[:, 0:CONV_CH] = ya_ref[...]
    mix_scr[:, CONV_CH:CONV_CH + MOBA_DIM] = yb_ref[...]
    mix_scr[:, CONV_CH + MOBA_DIM:] = yc_ref[...]
    x1 = x_ref[...] + _dot(mix_scr[...], wo_ref[...])
    ms = jnp.mean(x1 * x1, axis=-1, keepdims=True)
    h2_scr[...] = (x1 * lax.rsqrt(ms + RMS_EPS) * g2_ref[...]).astype(BF16)
    acc_scr[...] = x1
    for c0, c1 in FF_CHUNKS:
        g = _dot(h2_scr[...], wgu_ref[:, c0:c1])
        u = _dot(h2_scr[...], wgu_ref[:, D_FF + c0:D_FF + c1])
        act = (g * _sigmoid(g) * u).astype(BF16)
        acc_scr[...] += _dot(act, wd_ref[c0:c1, :])
    o_ref[...] = acc_scr[...]


def _out_ffn(x2, ya, yb, yc, wo, g2, wgu, wd):
    n = x2.shape[0]
    const = lambda shape: pl.BlockSpec(shape, lambda r: (0,) * len(shape), pipeline_mode=pl.Buffered(1))
    rows = lambda width: pl.BlockSpec((TM, width), lambda r: (r, 0))
    return pl.pallas_call(
        _out_ffn_kernel,
        grid=(n // TM,),
        in_specs=[rows(D_MODEL), rows(CONV_CH), rows(MOBA_DIM), rows(FOX_DIM),
                  const((D_MODEL, D_MODEL)), const((1, D_MODEL)), const((D_MODEL, 2 * D_FF)),
                  const((D_FF, D_MODEL))],
        out_specs=rows(D_MODEL),
        out_shape=jax.ShapeDtypeStruct((n, D_MODEL), F32),
        scratch_shapes=[pltpu.VMEM((TM, D_MODEL), BF16), pltpu.VMEM((TM, D_MODEL), BF16),
                        pltpu.VMEM((TM, D_MODEL), F32)],
        compiler_params=pltpu.CompilerParams(dimension_semantics=("arbitrary",),
                                             vmem_limit_bytes=VMEM_LIMIT),
        name="out_ffn",
    )(x2, ya, yb, yc, wo, g2, wgu, wd)


def kernel(x, w_in, b_forget, conv_w, conv_b, conv_ln_g, conv_ln_b, moba_qn_g, moba_kn_g,
           fox_qn_g, fox_kn_g, rel_bias, w_out, norm1_g, norm2_g, w_gate_up, w_down):
    batch, seq, d_model = x.shape
    depth = w_in.shape[0]
    assert d_model == D_MODEL and seq % TM == 0 and seq % MOBA_BLOCK == 0 and TQ == MOBA_BLOCK == TK
    assert w_in.shape[2] == IN_MAIN + FOX_HEADS and MOBA_HEADS == FOX_HEADS <= SUBLANES
    assert seq // MOBA_BLOCK <= SEL_LANES and HEAD_DIM + 3 * SEL_LANES <= LANES

    hd = jnp.arange(HM) // HEAD_DIM
    head_mean = jnp.where(hd[:, None] == hd[None, :], 1.0 / HEAD_DIM, 0.0).astype(BF16)
    tri = (jnp.arange(TM)[:, None] >= jnp.arange(TM)[None, :]).astype(BF16)
    placement = _placement_constants()
    columns = _in_proj_columns()

    bias = _moba_bias(rel_bias.astype(F32))
    x2 = x.reshape(batch * seq, D_MODEL)
    for l in range(depth):
        w = jnp.pad(w_in[l][:, columns], ((0, 0), (0, IN_PAD - w_in.shape[2]))).astype(BF16)
        bf = jnp.pad(b_forget[l].astype(F32), (0, LANES - FOX_HEADS)).reshape(1, LANES)
        gains = jnp.stack([jnp.concatenate([gq[l], gk[l]]).astype(F32)
                           for gq, gk in ((moba_qn_g, moba_kn_g), (fox_qn_g, fox_kn_g))])
        bound = (HEAD_DIM * QSCALE * NORM_SLACK * jnp.max(jnp.abs(fox_qn_g[l])) * jnp.max(jnp.abs(fox_kn_g[l]))
                 ).astype(F32).reshape(1)
        glu, qb, kb, vbT, qc, kc, vcT, cmax, cmin = _in_proj(
            x2, norm1_g[l].reshape(1, D_MODEL), w, bf, gains, head_mean, tri, placement, batch, seq)
        ya = _conv(glu, conv_w[l], conv_b[l].reshape(1, CONV_CH), conv_ln_g[l].reshape(1, CONV_CH),
                   conv_ln_b[l].reshape(1, CONV_CH), batch, seq)
        yb = _moba(qb, kb, vbT, bias, rel_bias.astype(F32), batch, seq)
        yc = _fox(qc, kc, vcT, cmax, cmin, bound, batch, seq)
        x2 = _out_ffn(x2, ya, yb, yc, w_out[l].astype(BF16), norm2_g[l].reshape(1, D_MODEL),
                      w_gate_up[l].astype(BF16), w_down[l].astype(BF16))
    return x2.reshape(batch, seq, D_MODEL)
```

```python
import functools
import math

import numpy as np
import jax
import jax.numpy as jnp
from jax import lax
from jax.experimental import pallas as pl
from jax.experimental.pallas import tpu as pltpu

F32 = jnp.float32
BF16 = jnp.bfloat16

D_MODEL = 1024
HEAD_DIM = 64
CONV_CH = D_MODEL // 4
MOBA_DIM = 3 * D_MODEL // 8
FOX_DIM = D_MODEL - CONV_CH - MOBA_DIM
MOBA_HEADS = MOBA_DIM // HEAD_DIM
FOX_HEADS = FOX_DIM // HEAD_DIM
HEADS = MOBA_HEADS
CONV_WIDTH = 31
LN_EPS = 1e-5
RMS_EPS = 1e-6
MOBA_BLOCK = 256
MOBA_TOPK = 3
NUM_BUCKETS = 32
MAX_DISTANCE = 128
D_FF = -(-8 * D_MODEL // (3 * 256)) * 256
NEG = -1e30
LOG2E = math.log2(math.e)
QSCALE = HEAD_DIM ** -0.5 * LOG2E

LANES = 128
SUBLANES = 8
BF16_ROWS = 16
IN_MAIN = 2 * CONV_CH + 3 * MOBA_DIM + 3 * FOX_DIM
IN_PAD = IN_MAIN + LANES
AUG = HEADS * LANES
HM = 256
SEL_LANES = 16
V_ROWS = HEAD_DIM + BF16_ROWS

TM = 512
TQ = 256
TK = 256
HEAD_GROUP = 6
MOBA_MERGE = 8
FOX_MERGE = 4
LAZY_LIMIT = 64.0
UNDERFLOW_LOG2 = 160.0
ROUNDING_MARGIN = 2.0
NORM_SLACK = 1.01
CONV_TC = 256
CONV_HALO = 32
CONV_SUB = 64
FF_CHUNKS = ((0, 768), (768, 1536), (1536, 2304), (2304, 2816))
VMEM_LIMIT = 56 * 1024 * 1024


def _split3(x):
    hi = x.astype(BF16)
    r1 = x - hi.astype(F32)
    mid = r1.astype(BF16)
    lo = (r1 - mid.astype(F32)).astype(BF16)
    return hi, mid, lo


def _dot(a, b):
    return jnp.dot(a, b, preferred_element_type=F32)


def _dot_nt(a, b):
    return lax.dot_general(a, b, (((1,), (1,)), ((), ())), preferred_element_type=F32)


def _sigmoid(x):
    return 1.0 / (1.0 + jnp.exp(-x))


def _group(h):
    return slice(h * LANES, (h + 1) * LANES)


def _placement_constants():
    cq = np.zeros((LANES, AUG), np.float32)
    ck = np.zeros((LANES, AUG), np.float32)
    ones_q = np.zeros((1, AUG), np.float32)
    ones_k = np.zeros((1, AUG), np.float32)
    for h in range(FOX_HEADS):
        for j in range(3):
            cq[j * SUBLANES + h, h * LANES + HEAD_DIM + j] = 1.0
            ones_k[0, h * LANES + HEAD_DIM + j] = 1.0
            ck[j * SUBLANES + h, h * LANES + HEAD_DIM + 3 + j] = -1.0
            ones_q[0, h * LANES + HEAD_DIM + 3 + j] = 1.0
    return jnp.asarray(cq, BF16), jnp.asarray(ck, BF16), jnp.asarray(ones_q), jnp.asarray(ones_k)


def _in_proj_weight(w):
    d = w.shape[0]
    parts, values = [w[:, :2 * CONV_CH]], []
    base = 2 * CONV_CH
    for width in (MOBA_DIM, FOX_DIM):
        q = w[:, base:base + width].reshape(d, HEADS, HEAD_DIM)
        k = w[:, base + width:base + 2 * width].reshape(d, HEADS, HEAD_DIM)
        parts.append(jnp.concatenate([q, k], axis=2).reshape(d, 2 * width))
        values.append(w[:, base + 2 * width:base + 3 * width])
        base += 3 * width
    pad = jnp.zeros((d, IN_PAD - w.shape[1]), w.dtype)
    return jnp.concatenate(parts + values + [w[:, IN_MAIN:], pad], axis=1).astype(BF16)


def _in_proj_kernel(nt, x_ref, g1_ref, w_ref, bf_ref, gains_ref, hm_ref, tri_ref,
                    cq_ref, ck_ref, onesq_ref, onesk_ref,
                    glu_ref, qb_ref, kb_ref, vbT_ref, qc_ref, kc_ref, vcT_ref, cmax_ref, cmin_ref,
                    h_scr, carry_scr):
    r = pl.program_id(0)

    @pl.when(r % nt == 0)
    def _():
        carry_scr[...] = jnp.zeros_like(carry_scr)

    x = x_ref[...]
    ms = jnp.mean(x * x, axis=-1, keepdims=True)
    h_scr[...] = (x * lax.rsqrt(ms + RMS_EPS) * g1_ref[...]).astype(BF16)
    lane = lax.broadcasted_iota(jnp.int32, (TM, LANES), 1)
    low = lane < HEAD_DIM

    def seg(c0, c1):
        return _dot(h_scr[...], w_ref[:, c0:c1])

    def store_heads(y, q_ref, k_ref, gains, upper_q, upper_k):
        sq = y * y
        hi = sq.astype(BF16)
        lo = (sq - hi.astype(F32)).astype(BF16)
        hm = hm_ref[...]
        msq = jnp.concatenate([_dot(hi[:, c:c + HM], hm) + _dot(lo[:, c:c + HM], hm) for c in range(0, AUG, HM)],
                              axis=1)
        gain = gains * jnp.where(low[0:1, :], QSCALE, 1.0)
        for h in range(HEADS):
            g = y[:, _group(h)] * lax.rsqrt(msq[:, _group(h)] + RMS_EPS) * gain
            q_ref[:, _group(h)] = jnp.where(low, g, upper_q(h)).astype(BF16)
            k_ref[:, _group(h)] = jnp.where(low, pltpu.roll(g, HEAD_DIM, axis=1), upper_k(h)).astype(BF16)

    def pack_terms(t):
        hi, mid, lo = _split3(jnp.where(lane < FOX_HEADS, t, 0.0))
        return (hi.astype(F32) + pltpu.roll(mid.astype(F32), SUBLANES, axis=1)
                + pltpu.roll(lo.astype(F32), 2 * SUBLANES, axis=1)).astype(BF16)

    def unpack_terms(p):
        return p + pltpu.roll(p, LANES - SUBLANES, axis=1) + pltpu.roll(p, LANES - 2 * SUBLANES, axis=1)

    def store_vT(ref, v):
        vT = v.T.astype(BF16)
        row = lax.broadcasted_iota(jnp.int32, (BF16_ROWS, TM), 0)
        tail = jnp.where(row == 0, 1.0, 0.0).astype(BF16)
        for h in range(HEADS):
            ref[0, h * V_ROWS:h * V_ROWS + HEAD_DIM, :] = vT[h * HEAD_DIM:(h + 1) * HEAD_DIM, :]
            ref[0, h * V_ROWS + HEAD_DIM:(h + 1) * V_ROWS, :] = tail

    c_moba, c_fox, c_val = 2 * CONV_CH, 2 * CONV_CH + AUG, 2 * CONV_CH + 2 * AUG
    a = seg(0, 2 * CONV_CH)
    z = seg(IN_MAIN, IN_PAD) + bf_ref[...]
    y_moba = seg(c_moba, c_moba + AUG)
    glu_ref[...] = a[:, :CONV_CH] * _sigmoid(a[:, CONV_CH:])

    logf = jnp.minimum(z, 0.0) - jnp.log1p(jnp.exp(-jnp.abs(z)))
    csum = unpack_terms(_dot(tri_ref[...], pack_terms(logf)))
    c = csum + carry_scr[...]
    carry_scr[...] = c[TM - 1:TM, :]
    y_fox = seg(c_fox, c_fox + AUG)

    pos = (r % nt) * TM + lax.broadcasted_iota(jnp.int32, (TM, LANES), 0)
    in_sel = (lane >= HEAD_DIM) & (lane < HEAD_DIM + 3 * SEL_LANES)
    onehot = jnp.where(in_sel & ((lane & (SEL_LANES - 1)) == pos // MOBA_BLOCK), 1.0, 0.0)
    store_heads(y_moba, qb_ref, kb_ref, gains_ref[0:1, :], lambda h: 0.0, lambda h: onehot)

    c2 = c * LOG2E
    for t in range(TM // TQ):
        part = c2[t * TQ:(t + 1) * TQ, :]
        cmax_ref[pl.ds(r * (TM // TQ) + t, 1), :] = jnp.max(part, axis=0, keepdims=True)
        cmin_ref[pl.ds(r * (TM // TQ) + t, 1), :] = jnp.min(part, axis=0, keepdims=True)
    packed = pack_terms(c2)
    upper_q = _dot(packed, cq_ref[...]) + onesq_ref[...]
    upper_k = _dot(packed, ck_ref[...]) + onesk_ref[...]
    v = seg(c_val, c_val + MOBA_DIM + FOX_DIM)
    store_heads(y_fox, qc_ref, kc_ref, gains_ref[1:2, :],
                lambda h: upper_q[:, _group(h)], lambda h: upper_k[:, _group(h)])
    store_vT(vbT_ref, v[:, :MOBA_DIM])
    store_vT(vcT_ref, v[:, MOBA_DIM:])


def _in_proj(x2, g1, w, bf, gains, hm, tri, placement, batch, seq):
    n = x2.shape[0]
    nt = seq // TM
    const = lambda shape: pl.BlockSpec(shape, lambda r: (0,) * len(shape), pipeline_mode=pl.Buffered(1))
    rows = lambda width: pl.BlockSpec((TM, width), lambda r: (r, 0))
    tposed = lambda height: pl.BlockSpec((1, height, TM), lambda r: (r // nt, 0, r % nt))
    out_shape = (
        jax.ShapeDtypeStruct((n, CONV_CH), F32),
        jax.ShapeDtypeStruct((n, AUG), BF16),
        jax.ShapeDtypeStruct((n, AUG), BF16),
        jax.ShapeDtypeStruct((batch, HEADS * V_ROWS, seq), BF16),
        jax.ShapeDtypeStruct((n, AUG), BF16),
        jax.ShapeDtypeStruct((n, AUG), BF16),
        jax.ShapeDtypeStruct((batch, HEADS * V_ROWS, seq), BF16),
        jax.ShapeDtypeStruct((n // TQ, LANES), F32),
        jax.ShapeDtypeStruct((n // TQ, LANES), F32),
    )
    whole = pl.BlockSpec((n // TQ, LANES), lambda r: (0, 0))
    return pl.pallas_call(
        functools.partial(_in_proj_kernel, nt),
        grid=(n // TM,),
        in_specs=[rows(D_MODEL), const((1, D_MODEL)), const((D_MODEL, IN_PAD)), const((1, LANES)),
                  const((2, LANES)), const((HM, HM)), const((TM, TM)),
                  const((LANES, AUG)), const((LANES, AUG)), const((1, AUG)), const((1, AUG))],
        out_specs=(rows(CONV_CH), rows(AUG), rows(AUG), tposed(HEADS * V_ROWS),
                   rows(AUG), rows(AUG), tposed(HEADS * V_ROWS), whole, whole),
        out_shape=out_shape,
        scratch_shapes=[pltpu.VMEM((TM, D_MODEL), BF16), pltpu.VMEM((1, LANES), F32)],
        compiler_params=pltpu.CompilerParams(dimension_semantics=("arbitrary",),
                                             vmem_limit_bytes=VMEM_LIMIT),
        name="in_proj",
    )(x2, g1, w, bf, gains, hm, tri, *placement)


def _conv_kernel(glu_ref, w_ref, b_ref, lg_ref, lb_ref, o_ref, hbuf, sbuf):
    t = pl.program_id(1)

    @pl.when(t == 0)
    def _():
        hbuf[0:CONV_HALO, :] = jnp.zeros((CONV_HALO, CONV_CH), F32)

    @pl.when(t > 0)
    def _():
        hbuf[0:CONV_HALO, :] = hbuf[CONV_TC:CONV_TC + CONV_HALO, :]

    hbuf[CONV_HALO:, :] = glu_ref[...]
    shift = CONV_HALO - (CONV_WIDTH - 1)
    for rho in range(SUBLANES):
        rows = CONV_TC + CONV_HALO - (SUBLANES if rho else 0)
        sbuf[rho, 0:rows, :] = hbuf[rho:rho + rows, :]
    for r0 in range(0, CONV_TC, CONV_SUB):
        acc = jnp.zeros((CONV_SUB, CONV_CH), F32) + b_ref[...]
        for j in range(CONV_WIDTH):
            rho, base = (j + shift) % SUBLANES, (j + shift) // SUBLANES * SUBLANES
            acc = acc + w_ref[j:j + 1, :] * sbuf[rho, r0 + base:r0 + base + CONV_SUB, :]
        mu = jnp.mean(acc, axis=-1, keepdims=True)
        d = acc - mu
        var = jnp.mean(d * d, axis=-1, keepdims=True)
        y = d * lax.rsqrt(var + LN_EPS) * lg_ref[...] + lb_ref[...]
        o_ref[r0:r0 + CONV_SUB, :] = (y * _sigmoid(y)).astype(BF16)


def _conv(glu, w, b, lg, lb, batch, seq):
    n = glu.shape[0]
    nt = seq // CONV_TC
    const = lambda shape: pl.BlockSpec(shape, lambda bb, t: (0,) * len(shape))
    return pl.pallas_call(
        _conv_kernel,
        grid=(batch, nt),
        in_specs=[pl.BlockSpec((CONV_TC, CONV_CH), lambda bb, t: (bb * nt + t, 0)),
                  const((CONV_WIDTH, CONV_CH)), const((1, CONV_CH)), const((1, CONV_CH)), const((1, CONV_CH))],
        out_specs=pl.BlockSpec((CONV_TC, CONV_CH), lambda bb, t: (bb * nt + t, 0)),
        out_shape=jax.ShapeDtypeStruct((n, CONV_CH), BF16),
        scratch_shapes=[pltpu.VMEM((CONV_HALO + CONV_TC, CONV_CH), F32),
                        pltpu.VMEM((SUBLANES, CONV_HALO + CONV_TC, CONV_CH), F32)],
        compiler_params=pltpu.CompilerParams(dimension_semantics=("arbitrary", "arbitrary")),
        name="conv_module",
    )(glu, w, b, lg, lb)


def _softmax_steps(score_fn, m_scr, acc_scr, first):
    for g0 in range(0, HEADS, HEAD_GROUP):
        hs = range(g0, min(g0 + HEAD_GROUP, HEADS))
        tiles = [score_fn(h) for h in hs]
        tile_max = [jnp.max(sT, axis=0, keepdims=True) for sT, _ in tiles]
        if first:
            m_new = tile_max
        else:
            m_old = [m_scr[h:h + 1, :] for h in hs]
            m_new = [jnp.maximum(a, b) for a, b in zip(m_old, tile_max)]
        pvs = [_dot(vT, jnp.exp2(sT - mn).astype(BF16)) for (sT, vT), mn in zip(tiles, m_new)]
        for n, h in enumerate(hs):
            if first:
                acc_scr[h] = pvs[n]
            else:
                acc_scr[h] = jnp.exp2(m_old[n] - m_new[n]) * acc_scr[h] + pvs[n]
            m_scr[h:h + 1, :] = m_new[n]


def _softmax_steps_lazy(score_fn, m_scr, acc_scr, pv_scr):
    worst = None
    tiles = [score_fn(h) for h in range(HEADS)]
    for h in range(HEADS):
        sT, vT = tiles[h]
        m_old = m_scr[h:h + 1, :]
        tile_max = jnp.max(sT, axis=0, keepdims=True)
        pv_scr[h] = _dot(vT, jnp.exp2(sT - m_old).astype(BF16))
        m_scr[SUBLANES + h:SUBLANES + h + 1, :] = tile_max
        rise = tile_max - m_old
        worst = rise if worst is None else jnp.maximum(worst, rise)
    safe = jnp.max(worst) <= LAZY_LIMIT

    @pl.when(safe)
    def _():
        for h in range(HEADS):
            m_old = m_scr[h:h + 1, :]
            m_new = jnp.maximum(m_old, m_scr[SUBLANES + h:SUBLANES + h + 1, :])
            acc_scr[h] = jnp.exp2(m_old - m_new) * (acc_scr[h] + pv_scr[h])
            m_scr[h:h + 1, :] = m_new

    @pl.when(jnp.logical_not(safe))
    def _():
        _softmax_steps(score_fn, m_scr, acc_scr, first=False)


def _write_heads(o_ref, acc_scr):
    for pair in range(HEADS // 2):
        both = []
        for h in (2 * pair, 2 * pair + 1):
            acc = acc_scr[h]
            both.append(acc[:HEAD_DIM, :] / acc[HEAD_DIM:HEAD_DIM + 1, :])
        o_ref[:, pair * LANES:(pair + 1) * LANES] = jnp.concatenate(both, axis=0).T.astype(BF16)


def _key_tile(k_ref, vT_ref, h, k0, tk=TK):
    return k_ref[pl.ds(k0, tk), _group(h)], vT_ref[0, h * V_ROWS:(h + 1) * V_ROWS, pl.ds(k0, tk)]


def _past_tiles(first, count, step, merge):
    big = merge * TK

    def body(j, carry):
        step(pl.multiple_of(first * TK + j * big, TK), big)
        return carry

    lax.fori_loop(0, count // merge, body, 0)
    done = first + (count // merge) * merge
    count = first + count
    part = merge // 2
    while part >= 1:
        rem = count - done

        @pl.when(rem >= part)
        def _(done=done, part=part):
            step(pl.multiple_of(done * TK, TK), part * TK)

        done = done + jnp.where(rem >= part, part, 0)
        part //= 2


_ATTN_SCRATCH = [
    pltpu.VMEM((2 * SUBLANES, TQ), F32),
    pltpu.VMEM((HEADS, V_ROWS, TQ), F32),
    pltpu.VMEM((HEADS, V_ROWS, TQ), F32),
]


def _attention_specs(batch, seq):
    nq = seq // TQ
    in_specs = [pl.BlockSpec((TQ, AUG), lambda b, i: (b * nq + i, 0)),
                pl.BlockSpec((seq, AUG), lambda b, i: (b, 0)),
                pl.BlockSpec((1, HEADS * V_ROWS, seq), lambda b, i: (b, 0, 0))]
    out_spec = pl.BlockSpec((TQ, MOBA_DIM), lambda b, i: (b * nq + i, 0))
    params = pltpu.CompilerParams(dimension_semantics=("arbitrary", "arbitrary"), vmem_limit_bytes=VMEM_LIMIT)
    return (batch, nq), in_specs, out_spec, params


def _t5_bucket(dist):
    n = jnp.maximum(dist, 0)
    max_exact = NUM_BUCKETS // 2
    nf = jnp.maximum(n, 1).astype(F32)
    large = max_exact + (jnp.log(nf / max_exact) / math.log(MAX_DISTANCE / max_exact)
                         * (NUM_BUCKETS - max_exact)).astype(jnp.int32)
    large = jnp.minimum(large, NUM_BUCKETS - 1)
    return jnp.where(n < max_exact, n, large)


def _bias_kernel(bucket_ref, rb_ref, o_ref):
    h = pl.program_id(0)
    bucket = bucket_ref[...]
    acc = jnp.zeros(bucket.shape, F32)
    for b in range(NUM_BUCKETS):
        acc = jnp.where(bucket == b, rb_ref[b, h], acc)
    key = lax.broadcasted_iota(jnp.int32, bucket.shape, 0)
    qry = lax.broadcasted_iota(jnp.int32, bucket.shape, 1)
    o_ref[0] = jnp.where((key < MOBA_BLOCK) | (key - MOBA_BLOCK <= qry), acc * LOG2E, NEG)


def _moba_bias(rel_bias):
    key = lax.broadcasted_iota(jnp.int32, (2 * MOBA_BLOCK, TQ), 0)
    qry = lax.broadcasted_iota(jnp.int32, (2 * MOBA_BLOCK, TQ), 1)
    bucket = _t5_bucket(qry + MOBA_BLOCK - key)
    return pl.pallas_call(
        _bias_kernel,
        grid=(MOBA_HEADS,),
        in_specs=[pl.BlockSpec((2 * MOBA_BLOCK, TQ), lambda h: (0, 0)),
                  pl.BlockSpec(memory_space=pltpu.SMEM)],
        out_specs=pl.BlockSpec((1, 2 * MOBA_BLOCK, TQ), lambda h: (h, 0, 0)),
        out_shape=jax.ShapeDtypeStruct((MOBA_HEADS, 2 * MOBA_BLOCK, TQ), F32),
        compiler_params=pltpu.CompilerParams(dimension_semantics=("arbitrary",)),
        name="moba_bias",
    )(bucket, rel_bias)


def _moba_kernel(nblk, q_ref, k_ref, vT_ref, bias_ref, rb_ref, o_ref, m_scr, acc_scr, pv_scr, qm_scr, kmean_scr):
    i = pl.program_id(1)

    @pl.when(i == 0)
    def _():
        kmean_scr[...] = jnp.zeros_like(kmean_scr)
        for n in range(nblk):
            blk = k_ref[n * MOBA_BLOCK:(n + 1) * MOBA_BLOCK, :].astype(F32)
            kmean_scr[n:n + 1, :] = jnp.mean(blk, axis=0, keepdims=True)

    q0 = pl.multiple_of(i * TQ, TQ)
    blk_row = lax.broadcasted_iota(jnp.int32, (SEL_LANES, TQ), 0)
    earlier = [jnp.where(mblk < blk_row, 1.0, 0.0) for mblk in range(nblk)]

    for h in range(HEADS):
        qg = q_ref[:, _group(h)]
        km_hi, km_mid, km_lo = _split3(kmean_scr[:, _group(h)])
        gate = _dot_nt(km_hi, qg) + _dot_nt(km_mid, qg) + _dot_nt(km_lo, qg)
        gate = jnp.where(blk_row < i, gate, -jnp.inf)
        cnt = jnp.zeros((SEL_LANES, TQ), F32)
        for mblk in range(nblk):
            gm = gate[mblk:mblk + 1, :]
            cnt = cnt + jnp.where(gm > gate, 1.0, jnp.where(gm == gate, earlier[mblk], 0.0))
        chosen = (cnt < MOBA_TOPK) & (blk_row < i)
        far = jnp.where(blk_row == i - 1, 0.0, rb_ref[NUM_BUCKETS - 1, h] * LOG2E)
        term = jnp.where(chosen, far, jnp.where(blk_row < i, NEG, 0.0))
        t_hi, t_mid, t_lo = _split3(term)
        lanes = jnp.concatenate(
            [jnp.zeros((HEAD_DIM, TQ), F32), t_hi.astype(F32), t_mid.astype(F32), t_lo.astype(F32),
             jnp.zeros((LANES - HEAD_DIM - 3 * SEL_LANES, TQ), F32)], axis=0)
        qm_scr[h] = (qg.astype(F32) + lanes.T).astype(BF16)

    @pl.when(i == 0)
    def _():
        def own_scores(h):
            kj, vTj = _key_tile(k_ref, vT_ref, h, q0)
            return _dot_nt(kj, qm_scr[h]) + bias_ref[h, MOBA_BLOCK:, :], vTj

        _softmax_steps(own_scores, m_scr, acc_scr, first=True)

    @pl.when(i >= 1)
    def _():
        k0 = pl.multiple_of((i - 1) * TK, TK)

        def near_scores(h):
            kj, vTj = _key_tile(k_ref, vT_ref, h, k0, 2 * TK)
            return _dot_nt(kj, qm_scr[h]) + bias_ref[h], vTj

        _softmax_steps(near_scores, m_scr, acc_scr, first=True)

    def far_step(k0, tk):
        def far_scores(h):
            kj, vTj = _key_tile(k_ref, vT_ref, h, k0, tk)
            return _dot_nt(kj, qm_scr[h]), vTj

        _softmax_steps_lazy(far_scores, m_scr, acc_scr, pv_scr)

    _past_tiles(0, jnp.maximum(i - 1, 0), far_step, MOBA_MERGE)
    _write_heads(o_ref, acc_scr)


def _moba(q, k, vT, bias, rel_bias, batch, seq):
    nblk = seq // MOBA_BLOCK
    grid, in_specs, out_spec, params = _attention_specs(batch, seq)
    return pl.pallas_call(
        functools.partial(_moba_kernel, nblk),
        grid=grid,
        in_specs=in_specs + [pl.BlockSpec((MOBA_HEADS, 2 * MOBA_BLOCK, TQ), lambda b, i: (0, 0, 0)),
                             pl.BlockSpec(memory_space=pltpu.SMEM)],
        out_specs=out_spec,
        out_shape=jax.ShapeDtypeStruct((q.shape[0], MOBA_DIM), BF16),
        scratch_shapes=_ATTN_SCRATCH + [pltpu.VMEM((HEADS, TQ, LANES), BF16),
                                        pltpu.VMEM((SEL_LANES, AUG), F32)],
        compiler_params=params,
        name="moba_attention",
    )(q, k, vT, bias, rel_bias)


def _fox_kernel(q_ref, k_ref, vT_ref, cmax_ref, cmin_ref, bound_ref, o_ref, m_scr, acc_scr, pv_scr):
    i = pl.program_id(1)
    q0 = pl.multiple_of(i * TQ, TQ)

    ntiles = cmin_ref.shape[0]
    gap = cmax_ref[pl.ds(i, 1), :] - cmin_ref[...]
    limit = -(2.0 * bound_ref[0] + (UNDERFLOW_LOG2 + ROUNDING_MARGIN))
    head_lane = lax.broadcasted_iota(jnp.int32, (ntiles, LANES), 1) < FOX_HEADS
    live = jnp.max(jnp.where(head_lane & (gap > limit), 1, 0), axis=1, keepdims=True)
    tile = lax.broadcasted_iota(jnp.int32, (ntiles, 1), 0)
    first_live = jnp.min(jnp.where((live > 0) & (tile < i), tile, i))

    def scores(h, k0, tk=TK):
        kj, vTj = _key_tile(k_ref, vT_ref, h, k0, tk)
        return _dot_nt(kj, q_ref[:, _group(h)]), vTj

    key = lax.broadcasted_iota(jnp.int32, (TK, TQ), 0)
    qry = lax.broadcasted_iota(jnp.int32, (TK, TQ), 1)

    def diag_scores(h):
        sT, vTj = scores(h, q0)
        return jnp.where(key <= qry, sT, NEG), vTj

    _softmax_steps(diag_scores, m_scr, acc_scr, first=True)

    def past_step(k0, tk):
        _softmax_steps_lazy(lambda h: scores(h, k0, tk), m_scr, acc_scr, pv_scr)

    _past_tiles(first_live, i - first_live, past_step, FOX_MERGE)
    _write_heads(o_ref, acc_scr)


def _fox(q, k, vT, cmax, cmin, bound, batch, seq):
    grid, in_specs, out_spec, params = _attention_specs(batch, seq)
    ntiles = seq // TQ
    return pl.pallas_call(
        _fox_kernel,
        grid=grid,
        in_specs=in_specs + [pl.BlockSpec((ntiles, LANES), lambda b, i: (b, 0)),
                             pl.BlockSpec((ntiles, LANES), lambda b, i: (b, 0)),
                             pl.BlockSpec(memory_space=pltpu.SMEM)],
        out_specs=out_spec,
        out_shape=jax.ShapeDtypeStruct((q.shape[0], FOX_DIM), BF16),
        scratch_shapes=_ATTN_SCRATCH,
        compiler_params=params,
        name="fox_attention",
    )(q, k, vT, cmax, cmin, bound)


def _out_ffn_kernel(x_ref, ya_ref, yb_ref, yc_ref, wo_ref, g2_ref, wgu_ref, wd_ref, o_ref,
                    mix_scr, h2_scr, acc_scr):
    mix_scr[:, 0:CONV_CH] = ya_ref[...]
    mix_scr[:, CONV_CH:CONV_CH + MOBA_DIM] = yb_ref[...]
    mix_scr[:, CONV_CH + MOBA_DIM:] = yc_ref[...]
    x1 = x_ref[...] + _dot(mix_scr[...], wo_ref[...])
    ms = jnp.mean(x1 * x1, axis=-1, keepdims=True)
    h2_scr[...] = (x1 * lax.rsqrt(ms + RMS_EPS) * g2_ref[...]).astype(BF16)
    acc_scr[...] = x1
    for c0, c1 in FF_CHUNKS:
        g = _dot(h2_scr[...], wgu_ref[:, c0:c1])
        u = _dot(h2_scr[...], wgu_ref[:, D_FF + c0:D_FF + c1])
        act = (g * _sigmoid(g) * u).astype(BF16)
        acc_scr[...] += _dot(act, wd_ref[c0:c1, :])
    o_ref[...] = acc_scr[...]


def _out_ffn(x2, ya, yb, yc, wo, g2, wgu, wd):
    n = x2.shape[0]
    const = lambda shape: pl.BlockSpec(shape, lambda r: (0,) * len(shape), pipeline_mode=pl.Buffered(1))
    rows = lambda width: pl.BlockSpec((TM, width), lambda r: (r, 0))
    return pl.pallas_call(
        _out_ffn_kernel,
        grid=(n // TM,),
        in_specs=[rows(D_MODEL), rows(CONV_CH), rows(MOBA_DIM), rows(FOX_DIM),
                  const((D_MODEL, D_MODEL)), const((1, D_MODEL)), const((D_MODEL, 2 * D_FF)),
                  const((D_FF, D_MODEL))],
        out_specs=rows(D_MODEL),
        out_shape=jax.ShapeDtypeStruct((n, D_MODEL), F32),
        scratch_shapes=[pltpu.VMEM((TM, D_MODEL), BF16), pltpu.VMEM((TM, D_MODEL), BF16),
                        pltpu.VMEM((TM, D_MODEL), F32)],
        compiler_params=pltpu.CompilerParams(dimension_semantics=("arbitrary",),
                                             vmem_limit_bytes=VMEM_LIMIT),
        name="out_ffn",
    )(x2, ya, yb, yc, wo, g2, wgu, wd)


def kernel(x, w_in, b_forget, conv_w, conv_b, conv_ln_g, conv_ln_b, moba_qn_g, moba_kn_g,
           fox_qn_g, fox_kn_g, rel_bias, w_out, norm1_g, norm2_g, w_gate_up, w_down):
    batch, seq, d_model = x.shape
    depth = w_in.shape[0]
    assert d_model == D_MODEL and seq % TM == 0 and seq % MOBA_BLOCK == 0 and TQ == MOBA_BLOCK == TK
    assert w_in.shape[2] == IN_MAIN + FOX_HEADS and MOBA_HEADS == FOX_HEADS <= SUBLANES
    assert seq // MOBA_BLOCK <= SEL_LANES and HEAD_DIM + 3 * SEL_LANES <= LANES

    hd = jnp.arange(HM) // HEAD_DIM
    head_mean = jnp.where(hd[:, None] == hd[None, :], 1.0 / HEAD_DIM, 0.0).astype(BF16)
    tri = (jnp.arange(TM)[:, None] >= jnp.arange(TM)[None, :]).astype(BF16)
    placement = _placement_constants()

    bias = _moba_bias(rel_bias.astype(F32))
    x2 = x.reshape(batch * seq, D_MODEL)
    for l in range(depth):
        w = _in_proj_weight(w_in[l])
        bf = jnp.pad(b_forget[l].astype(F32), (0, LANES - FOX_HEADS)).reshape(1, LANES)
        gains = jnp.stack([jnp.concatenate([gq[l], gk[l]]).astype(F32)
                           for gq, gk in ((moba_qn_g, moba_kn_g), (fox_qn_g, fox_kn_g))])
        bound = (HEAD_DIM * QSCALE * NORM_SLACK * jnp.max(jnp.abs(fox_qn_g[l])) * jnp.max(jnp.abs(fox_kn_g[l]))
                 ).astype(F32).reshape(1)
        glu, qb, kb, vbT, qc, kc, vcT, cmax, cmin = _in_proj(
            x2, norm1_g[l].reshape(1, D_MODEL), w, bf, gains, head_mean, tri, placement, batch, seq)
        ya = _conv(glu, conv_w[l], conv_b[l].reshape(1, CONV_CH), conv_ln_g[l].reshape(1, CONV_CH),
                   conv_ln_b[l].reshape(1, CONV_CH), batch, seq)
        yb = _moba(qb, kb, vbT, bias, rel_bias.astype(F32), batch, seq)
        yc = _fox(qc, kc, vcT, cmax, cmin, bound, batch, seq)
        x2 = _out_ffn(x2, ya, yb, yc, w_out[l].astype(BF16), norm2_g[l].reshape(1, D_MODEL),
                      w_gate_up[l].astype(BF16), w_down[l].astype(BF16))
    return x2.reshape(batch, seq, D_MODEL)
```

```python
import functools
import math

import numpy as np
import jax
import jax.numpy as jnp
from jax import lax
from jax.experimental import pallas as pl
from jax.experimental.pallas import tpu as pltpu

F32 = jnp.float32
BF16 = jnp.bfloat16

D_MODEL = 1024
HEAD_DIM = 64
CONV_CH = D_MODEL // 4
MOBA_DIM = 3 * D_MODEL // 8
FOX_DIM = D_MODEL - CONV_CH - MOBA_DIM
MOBA_HEADS = MOBA_DIM // HEAD_DIM
FOX_HEADS = FOX_DIM // HEAD_DIM
HEADS = MOBA_HEADS
CONV_WIDTH = 31
LN_EPS = 1e-5
RMS_EPS = 1e-6
MOBA_BLOCK = 256
MOBA_TOPK = 3
NUM_BUCKETS = 32
MAX_DISTANCE = 128
D_FF = -(-8 * D_MODEL // (3 * 256)) * 256
NEG = -1e30
LOG2E = math.log2(math.e)
QSCALE = HEAD_DIM ** -0.5 * LOG2E

LANES = 128
SUBLANES = 8
BF16_ROWS = 16
IN_MAIN = 2 * CONV_CH + 3 * MOBA_DIM + 3 * FOX_DIM
IN_PAD = IN_MAIN + LANES
AUG = HEADS * LANES
HM = 256
SEL_LANES = 16
V_ROWS = HEAD_DIM + BF16_ROWS

TM = 512
TQ = 256
TK = 256
HEAD_GROUP = 6
MOBA_MERGE = 8
FOX_MERGE = 4
LAZY_LIMIT = 64.0
UNDERFLOW_LOG2 = 160.0
ROUNDING_MARGIN = 2.0
NORM_SLACK = 1.01
CONV_TC = 256
CONV_HALO = 32
CONV_SUB = 64
FF_CHUNKS = ((0, 768), (768, 1536), (1536, 2304), (2304, 2816))
VMEM_LIMIT = 56 * 1024 * 1024


def _split3(x):
    hi = x.astype(BF16)
    r1 = x - hi.astype(F32)
    mid = r1.astype(BF16)
    lo = (r1 - mid.astype(F32)).astype(BF16)
    return hi, mid, lo


def _dot(a, b):
    return jnp.dot(a, b, preferred_element_type=F32)


def _dot_nt(a, b):
    return lax.dot_general(a, b, (((1,), (1,)), ((), ())), preferred_element_type=F32)


def _sigmoid(x):
    return 1.0 / (1.0 + jnp.exp(-x))


def _group(h):
    return slice(h * LANES, (h + 1) * LANES)


def _placement_constants():
    cq = np.zeros((LANES, AUG), np.float32)
    ck = np.zeros((LANES, AUG), np.float32)
    ones_q = np.zeros((1, AUG), np.float32)
    ones_k = np.zeros((1, AUG), np.float32)
    for h in range(FOX_HEADS):
        for j in range(3):
            cq[j * SUBLANES + h, h * LANES + HEAD_DIM + j] = 1.0
            ones_k[0, h * LANES + HEAD_DIM + j] = 1.0
            ck[j * SUBLANES + h, h * LANES + HEAD_DIM + 3 + j] = -1.0
            ones_q[0, h * LANES + HEAD_DIM + 3 + j] = 1.0
    return jnp.asarray(cq, BF16), jnp.asarray(ck, BF16), jnp.asarray(ones_q), jnp.asarray(ones_k)


def _in_proj_weight(w):
    cols = list(range(2 * CONV_CH))
    values = []
    base = 2 * CONV_CH
    for width in (MOBA_DIM, FOX_DIM):
        q0, k0, v0 = base, base + width, base + 2 * width
        for h in range(HEADS):
            cols += list(range(q0 + h * HEAD_DIM, q0 + (h + 1) * HEAD_DIM))
            cols += list(range(k0 + h * HEAD_DIM, k0 + (h + 1) * HEAD_DIM))
        values += list(range(v0, v0 + width))
        base += 3 * width
    cols += values + list(range(IN_MAIN, IN_MAIN + FOX_HEADS))
    return jnp.pad(w.astype(BF16)[:, np.asarray(cols, np.int32)], ((0, 0), (0, IN_PAD - w.shape[1])))


def _in_proj_kernel(nt, x_ref, g1_ref, w_ref, bf_ref, gains_ref, hm_ref, tri_ref,
                    cq_ref, ck_ref, onesq_ref, onesk_ref,
                    glu_ref, qb_ref, kb_ref, vbT_ref, qc_ref, kc_ref, vcT_ref, cmax_ref, cmin_ref,
                    h_scr, carry_scr):
    r = pl.program_id(0)

    @pl.when(r % nt == 0)
    def _():
        carry_scr[...] = jnp.zeros_like(carry_scr)

    x = x_ref[...]
    ms = jnp.mean(x * x, axis=-1, keepdims=True)
    h_scr[...] = (x * lax.rsqrt(ms + RMS_EPS) * g1_ref[...]).astype(BF16)
    lane = lax.broadcasted_iota(jnp.int32, (TM, LANES), 1)
    low = lane < HEAD_DIM

    def seg(c0, c1):
        return _dot(h_scr[...], w_ref[:, c0:c1])

    def store_heads(y, q_ref, k_ref, gains, upper_q, upper_k):
        sq = y * y
        hi = sq.astype(BF16)
        lo = (sq - hi.astype(F32)).astype(BF16)
        hm = hm_ref[...]
        msq = jnp.concatenate([_dot(hi[:, c:c + HM], hm) + _dot(lo[:, c:c + HM], hm) for c in range(0, AUG, HM)],
                              axis=1)
        gain = gains * jnp.where(low[0:1, :], QSCALE, 1.0)
        for h in range(HEADS):
            g = y[:, _group(h)] * lax.rsqrt(msq[:, _group(h)] + RMS_EPS) * gain
            q_ref[:, _group(h)] = jnp.where(low, g, upper_q(h)).astype(BF16)
            k_ref[:, _group(h)] = jnp.where(low, pltpu.roll(g, HEAD_DIM, axis=1), upper_k(h)).astype(BF16)

    def pack_terms(t):
        hi, mid, lo = _split3(jnp.where(lane < FOX_HEADS, t, 0.0))
        return (hi.astype(F32) + pltpu.roll(mid.astype(F32), SUBLANES, axis=1)
                + pltpu.roll(lo.astype(F32), 2 * SUBLANES, axis=1)).astype(BF16)

    def unpack_terms(p):
        return p + pltpu.roll(p, LANES - SUBLANES, axis=1) + pltpu.roll(p, LANES - 2 * SUBLANES, axis=1)

    def store_vT(ref, v):
        vT = v.T.astype(BF16)
        row = lax.broadcasted_iota(jnp.int32, (BF16_ROWS, TM), 0)
        tail = jnp.where(row == 0, 1.0, 0.0).astype(BF16)
        for h in range(HEADS):
            ref[0, h * V_ROWS:h * V_ROWS + HEAD_DIM, :] = vT[h * HEAD_DIM:(h + 1) * HEAD_DIM, :]
            ref[0, h * V_ROWS + HEAD_DIM:(h + 1) * V_ROWS, :] = tail

    c_moba, c_fox, c_val = 2 * CONV_CH, 2 * CONV_CH + AUG, 2 * CONV_CH + 2 * AUG
    a = seg(0, 2 * CONV_CH)
    z = seg(IN_MAIN, IN_PAD) + bf_ref[...]
    y_moba = seg(c_moba, c_moba + AUG)
    glu_ref[...] = a[:, :CONV_CH] * _sigmoid(a[:, CONV_CH:])

    logf = jnp.minimum(z, 0.0) - jnp.log1p(jnp.exp(-jnp.abs(z)))
    csum = unpack_terms(_dot(tri_ref[...], pack_terms(logf)))
    c = csum + carry_scr[...]
    carry_scr[...] = c[TM - 1:TM, :]
    y_fox = seg(c_fox, c_fox + AUG)

    pos = (r % nt) * TM + lax.broadcasted_iota(jnp.int32, (TM, LANES), 0)
    in_sel = (lane >= HEAD_DIM) & (lane < HEAD_DIM + 3 * SEL_LANES)
    onehot = jnp.where(in_sel & ((lane & (SEL_LANES - 1)) == pos // MOBA_BLOCK), 1.0, 0.0)
    store_heads(y_moba, qb_ref, kb_ref, gains_ref[0:1, :], lambda h: 0.0, lambda h: onehot)

    c2 = c * LOG2E
    for t in range(TM // TQ):
        part = c2[t * TQ:(t + 1) * TQ, :]
        cmax_ref[pl.ds(r * (TM // TQ) + t, 1), :] = jnp.max(part, axis=0, keepdims=True)
        cmin_ref[pl.ds(r * (TM // TQ) + t, 1), :] = jnp.min(part, axis=0, keepdims=True)
    packed = pack_terms(c2)
    upper_q = _dot(packed, cq_ref[...]) + onesq_ref[...]
    upper_k = _dot(packed, ck_ref[...]) + onesk_ref[...]
    v = seg(c_val, c_val + MOBA_DIM + FOX_DIM)
    store_heads(y_fox, qc_ref, kc_ref, gains_ref[1:2, :],
                lambda h: upper_q[:, _group(h)], lambda h: upper_k[:, _group(h)])
    store_vT(vbT_ref, v[:, :MOBA_DIM])
    store_vT(vcT_ref, v[:, MOBA_DIM:])


def _in_proj(x2, g1, w, bf, gains, hm, tri, placement, batch, seq):
    n = x2.shape[0]
    nt = seq // TM
    const = lambda shape: pl.BlockSpec(shape, lambda r: (0,) * len(shape), pipeline_mode=pl.Buffered(1))
    rows = lambda width: pl.BlockSpec((TM, width), lambda r: (r, 0))
    tposed = lambda height: pl.BlockSpec((1, height, TM), lambda r: (r // nt, 0, r % nt))
    out_shape = (
        jax.ShapeDtypeStruct((n, CONV_CH), F32),
        jax.ShapeDtypeStruct((n, AUG), BF16),
        jax.ShapeDtypeStruct((n, AUG), BF16),
        jax.ShapeDtypeStruct((batch, HEADS * V_ROWS, seq), BF16),
        jax.ShapeDtypeStruct((n, AUG), BF16),
        jax.ShapeDtypeStruct((n, AUG), BF16),
        jax.ShapeDtypeStruct((batch, HEADS * V_ROWS, seq), BF16),
        jax.ShapeDtypeStruct((n // TQ, LANES), F32),
        jax.ShapeDtypeStruct((n // TQ, LANES), F32),
    )
    whole = pl.BlockSpec((n // TQ, LANES), lambda r: (0, 0))
    return pl.pallas_call(
        functools.partial(_in_proj_kernel, nt),
        grid=(n // TM,),
        in_specs=[rows(D_MODEL), const((1, D_MODEL)), const((D_MODEL, IN_PAD)), const((1, LANES)),
                  const((2, LANES)), const((HM, HM)), const((TM, TM)),
                  const((LANES, AUG)), const((LANES, AUG)), const((1, AUG)), const((1, AUG))],
        out_specs=(rows(CONV_CH), rows(AUG), rows(AUG), tposed(HEADS * V_ROWS),
                   rows(AUG), rows(AUG), tposed(HEADS * V_ROWS), whole, whole),
        out_shape=out_shape,
        scratch_shapes=[pltpu.VMEM((TM, D_MODEL), BF16), pltpu.VMEM((1, LANES), F32)],
        compiler_params=pltpu.CompilerParams(dimension_semantics=("arbitrary",),
                                             vmem_limit_bytes=VMEM_LIMIT),
        name="in_proj",
    )(x2, g1, w, bf, gains, hm, tri, *placement)


def _softmax_steps(score_fn, m_scr, acc_scr, first):
    for g0 in range(0, HEADS, HEAD_GROUP):
        hs = range(g0, min(g0 + HEAD_GROUP, HEADS))
        tiles = [score_fn(h) for h in hs]
        tile_max = [jnp.max(sT, axis=0, keepdims=True) for sT, _ in tiles]
        if first:
            m_new = tile_max
        else:
            m_old = [m_scr[h:h + 1, :] for h in hs]
            m_new = [jnp.maximum(a, b) for a, b in zip(m_old, tile_max)]
        pvs = [_dot(vT, jnp.exp2(sT - mn).astype(BF16)) for (sT, vT), mn in zip(tiles, m_new)]
        for n, h in enumerate(hs):
            if first:
                acc_scr[h] = pvs[n]
            else:
                acc_scr[h] = jnp.exp2(m_old[n] - m_new[n]) * acc_scr[h] + pvs[n]
            m_scr[h:h + 1, :] = m_new[n]


def _softmax_steps_lazy(score_fn, m_scr, acc_scr, pv_scr):
    worst = None
    tiles = [score_fn(h) for h in range(HEADS)]
    for h in range(HEADS):
        sT, vT = tiles[h]
        m_old = m_scr[h:h + 1, :]
        tile_max = jnp.max(sT, axis=0, keepdims=True)
        pv_scr[h] = _dot(vT, jnp.exp2(sT - m_old).astype(BF16))
        m_scr[SUBLANES + h:SUBLANES + h + 1, :] = tile_max
        rise = tile_max - m_old
        worst = rise if worst is None else jnp.maximum(worst, rise)
    safe = jnp.max(worst) <= LAZY_LIMIT

    @pl.when(safe)
    def _():
        for h in range(HEADS):
            m_old = m_scr[h:h + 1, :]
            m_new = jnp.maximum(m_old, m_scr[SUBLANES + h:SUBLANES + h + 1, :])
            acc_scr[h] = jnp.exp2(m_old - m_new) * (acc_scr[h] + pv_scr[h])
            m_scr[h:h + 1, :] = m_new

    @pl.when(jnp.logical_not(safe))
    def _():
        _softmax_steps(score_fn, m_scr, acc_scr, first=False)


def _write_heads(o_ref, acc_scr):
    for pair in range(HEADS // 2):
        both = []
        for h in (2 * pair, 2 * pair + 1):
            acc = acc_scr[h]
            both.append(acc[:HEAD_DIM, :] / acc[HEAD_DIM:HEAD_DIM + 1, :])
        o_ref[:, pair * LANES:(pair + 1) * LANES] = jnp.concatenate(both, axis=0).T.astype(BF16)


def _key_tile(k_ref, vT_ref, h, k0, tk=TK):
    return k_ref[pl.ds(k0, tk), _group(h)], vT_ref[0, h * V_ROWS:(h + 1) * V_ROWS, pl.ds(k0, tk)]


def _past_tiles(first, count, step, merge):
    big = merge * TK

    def body(j, carry):
        step(pl.multiple_of(first * TK + j * big, TK), big)
        return carry

    lax.fori_loop(0, count // merge, body, 0)
    done = first + (count // merge) * merge
    count = first + count
    part = merge // 2
    while part >= 1:
        rem = count - done

        @pl.when(rem >= part)
        def _(done=done, part=part):
            step(pl.multiple_of(done * TK, TK), part * TK)

        done = done + jnp.where(rem >= part, part, 0)
        part //= 2


_ATTN_SCRATCH = [
    pltpu.VMEM((2 * SUBLANES, TQ), F32),
    pltpu.VMEM((HEADS, V_ROWS, TQ), F32),
    pltpu.VMEM((HEADS, V_ROWS, TQ), F32),
]


def _attention_specs(batch, seq):
    nq = seq // TQ
    in_specs = [pl.BlockSpec((TQ, AUG), lambda b, i: (b * nq + i, 0)),
                pl.BlockSpec((seq, AUG), lambda b, i: (b, 0)),
                pl.BlockSpec((1, HEADS * V_ROWS, seq), lambda b, i: (b, 0, 0))]
    out_spec = pl.BlockSpec((TQ, MOBA_DIM), lambda b, i: (b * nq + i, 0))
    params = pltpu.CompilerParams(dimension_semantics=("arbitrary", "arbitrary"), vmem_limit_bytes=VMEM_LIMIT)
    return (batch, nq), in_specs, out_spec, params


def _t5_bucket(dist):
    n = jnp.maximum(dist, 0)
    max_exact = NUM_BUCKETS // 2
    nf = jnp.maximum(n, 1).astype(F32)
    large = max_exact + (jnp.log(nf / max_exact) / math.log(MAX_DISTANCE / max_exact)
                         * (NUM_BUCKETS - max_exact)).astype(jnp.int32)
    large = jnp.minimum(large, NUM_BUCKETS - 1)
    return jnp.where(n < max_exact, n, large)


def _bias_kernel(bucket_ref, rb_ref, o_ref):
    h = pl.program_id(0)
    bucket = bucket_ref[...]
    acc = jnp.zeros(bucket.shape, F32)
    for b in range(NUM_BUCKETS):
        acc = jnp.where(bucket == b, rb_ref[b, h], acc)
    key = lax.broadcasted_iota(jnp.int32, bucket.shape, 0)
    qry = lax.broadcasted_iota(jnp.int32, bucket.shape, 1)
    o_ref[0] = jnp.where((key < MOBA_BLOCK) | (key - MOBA_BLOCK <= qry), acc * LOG2E, NEG)


def _moba_bias(rel_bias):
    key = lax.broadcasted_iota(jnp.int32, (2 * MOBA_BLOCK, TQ), 0)
    qry = lax.broadcasted_iota(jnp.int32, (2 * MOBA_BLOCK, TQ), 1)
    bucket = _t5_bucket(qry + MOBA_BLOCK - key)
    return pl.pallas_call(
        _bias_kernel,
        grid=(MOBA_HEADS,),
        in_specs=[pl.BlockSpec((2 * MOBA_BLOCK, TQ), lambda h: (0, 0)),
                  pl.BlockSpec(memory_space=pltpu.SMEM)],
        out_specs=pl.BlockSpec((1, 2 * MOBA_BLOCK, TQ), lambda h: (h, 0, 0)),
        out_shape=jax.ShapeDtypeStruct((MOBA_HEADS, 2 * MOBA_BLOCK, TQ), F32),
        compiler_params=pltpu.CompilerParams(dimension_semantics=("arbitrary",)),
        name="moba_bias",
    )(bucket, rel_bias)


def _moba_kernel(nblk, q_ref, k_ref, vT_ref, bias_ref, rb_ref, o_ref, m_scr, acc_scr, pv_scr, qm_scr, kmean_scr):
    i = pl.program_id(1)

    @pl.when(i == 0)
    def _():
        kmean_scr[...] = jnp.zeros_like(kmean_scr)
        for n in range(nblk):
            blk = k_ref[n * MOBA_BLOCK:(n + 1) * MOBA_BLOCK, :].astype(F32)
            kmean_scr[n:n + 1, :] = jnp.mean(blk, axis=0, keepdims=True)

    q0 = pl.multiple_of(i * TQ, TQ)
    blk_row = lax.broadcasted_iota(jnp.int32, (SEL_LANES, TQ), 0)
    earlier = [jnp.where(mblk < blk_row, 1.0, 0.0) for mblk in range(nblk)]

    for h in range(HEADS):
        qg = q_ref[:, _group(h)]
        km_hi, km_mid, km_lo = _split3(kmean_scr[:, _group(h)])
        gate = _dot_nt(km_hi, qg) + _dot_nt(km_mid, qg) + _dot_nt(km_lo, qg)
        gate = jnp.where(blk_row < i, gate, -jnp.inf)
        cnt = jnp.zeros((SEL_LANES, TQ), F32)
        for mblk in range(nblk):
            gm = gate[mblk:mblk + 1, :]
            cnt = cnt + jnp.where(gm > gate, 1.0, jnp.where(gm == gate, earlier[mblk], 0.0))
        chosen = (cnt < MOBA_TOPK) & (blk_row < i)
        far = jnp.where(blk_row == i - 1, 0.0, rb_ref[NUM_BUCKETS - 1, h] * LOG2E)
        term = jnp.where(chosen, far, jnp.where(blk_row < i, NEG, 0.0))
        t_hi, t_mid, t_lo = _split3(term)
        lanes = jnp.concatenate(
            [jnp.zeros((HEAD_DIM, TQ), F32), t_hi.astype(F32), t_mid.astype(F32), t_lo.astype(F32),
             jnp.zeros((LANES - HEAD_DIM - 3 * SEL_LANES, TQ), F32)], axis=0)
        qm_scr[h] = (qg.astype(F32) + lanes.T).astype(BF16)

    @pl.when(i == 0)
    def _():
        def own_scores(h):
            kj, vTj = _key_tile(k_ref, vT_ref, h, q0)
            return _dot_nt(kj, qm_scr[h]) + bias_ref[h, MOBA_BLOCK:, :], vTj

        _softmax_steps(own_scores, m_scr, acc_scr, first=True)

    @pl.when(i >= 1)
    def _():
        k0 = pl.multiple_of((i - 1) * TK, TK)

        def near_scores(h):
            kj, vTj = _key_tile(k_ref, vT_ref, h, k0, 2 * TK)
            return _dot_nt(kj, qm_scr[h]) + bias_ref[h], vTj

        _softmax_steps(near_scores, m_scr, acc_scr, first=True)

    def far_step(k0, tk):
        def far_scores(h):
            kj, vTj = _key_tile(k_ref, vT_ref, h, k0, tk)
            return _dot_nt(kj, qm_scr[h]), vTj

        _softmax_steps_lazy(far_scores, m_scr, acc_scr, pv_scr)

    _past_tiles(0, jnp.maximum(i - 1, 0), far_step, MOBA_MERGE)
    _write_heads(o_ref, acc_scr)


def _moba(q, k, vT, bias, rel_bias, batch, seq):
    nblk = seq // MOBA_BLOCK
    grid, in_specs, out_spec, params = _attention_specs(batch, seq)
    return pl.pallas_call(
        functools.partial(_moba_kernel, nblk),
        grid=grid,
        in_specs=in_specs + [pl.BlockSpec((MOBA_HEADS, 2 * MOBA_BLOCK, TQ), lambda b, i: (0, 0, 0)),
                             pl.BlockSpec(memory_space=pltpu.SMEM)],
        out_specs=out_spec,
        out_shape=jax.ShapeDtypeStruct((q.shape[0], MOBA_DIM), BF16),
        scratch_shapes=_ATTN_SCRATCH + [pltpu.VMEM((HEADS, TQ, LANES), BF16),
                                        pltpu.VMEM((SEL_LANES, AUG), F32)],
        compiler_params=params,
        name="moba_attention",
    )(q, k, vT, bias, rel_bias)


def _fox_kernel(q_ref, k_ref, vT_ref, cmax_ref, cmin_ref, bound_ref, o_ref, m_scr, acc_scr, pv_scr):
    i = pl.program_id(1)
    q0 = pl.multiple_of(i * TQ, TQ)

    ntiles = cmin_ref.shape[0]
    gap = cmax_ref[pl.ds(i, 1), :] - cmin_ref[...]
    limit = -(2.0 * bound_ref[0] + (UNDERFLOW_LOG2 + ROUNDING_MARGIN))
    head_lane = lax.broadcasted_iota(jnp.int32, (ntiles, LANES), 1) < FOX_HEADS
    live = jnp.max(jnp.where(head_lane & (gap > limit), 1, 0), axis=1, keepdims=True)
    tile = lax.broadcasted_iota(jnp.int32, (ntiles, 1), 0)
    first_live = jnp.min(jnp.where((live > 0) & (tile < i), tile, i))

    def scores(h, k0, tk=TK):
        kj, vTj = _key_tile(k_ref, vT_ref, h, k0, tk)
        return _dot_nt(kj, q_ref[:, _group(h)]), vTj

    key = lax.broadcasted_iota(jnp.int32, (TK, TQ), 0)
    qry = lax.broadcasted_iota(jnp.int32, (TK, TQ), 1)

    def diag_scores(h):
        sT, vTj = scores(h, q0)
        return jnp.where(key <= qry, sT, NEG), vTj

    _softmax_steps(diag_scores, m_scr, acc_scr, first=True)

    def past_step(k0, tk):
        _softmax_steps_lazy(lambda h: scores(h, k0, tk), m_scr, acc_scr, pv_scr)

    _past_tiles(first_live, i - first_live, past_step, FOX_MERGE)
    _write_heads(o_ref, acc_scr)


def _fox(q, k, vT, cmax, cmin, bound, batch, seq):
    grid, in_specs, out_spec, params = _attention_specs(batch, seq)
    ntiles = seq // TQ
    return pl.pallas_call(
        _fox_kernel,
        grid=grid,
        in_specs=in_specs + [pl.BlockSpec((ntiles, LANES), lambda b, i: (b, 0)),
                             pl.BlockSpec((ntiles, LANES), lambda b, i: (b, 0)),
                             pl.BlockSpec(memory_space=pltpu.SMEM)],
        out_specs=out_spec,
        out_shape=jax.ShapeDtypeStruct((q.shape[0], FOX_DIM), BF16),
        scratch_shapes=_ATTN_SCRATCH,
        compiler_params=params,
        name="fox_attention",
    )(q, k, vT, cmax, cmin, bound)


def _conv_rows(w_ref, b_ref, lg_ref, lb_ref, sbuf, r0, zero_row):
    shift = CONV_HALO - (CONV_WIDTH - 1)
    acc = jnp.zeros((CONV_SUB, CONV_CH), F32) + (b_ref[...] + zero_row)
    for j in range(CONV_WIDTH):
        rho, base = (j + shift) % SUBLANES, (j + shift) // SUBLANES * SUBLANES
        acc = acc + w_ref[j:j + 1, :] * sbuf[rho, r0 + base:r0 + base + CONV_SUB, :]
    mu = jnp.mean(acc, axis=-1, keepdims=True)
    d = acc - mu
    var = jnp.mean(d * d, axis=-1, keepdims=True)
    y = d * lax.rsqrt(var + LN_EPS) * lg_ref[...] + lb_ref[...]
    return (y * _sigmoid(y)).astype(BF16)


def _out_ffn_kernel(nt, x_ref, glu_ref, yb_ref, yc_ref, cw_ref, cb_ref, clg_ref, clb_ref,
                    wo_ref, g2_ref, wgu_ref, wd_ref, o_ref,
                    ya_scr, hbuf, sbuf, mix_scr, h2_scr, acc_scr):
    s = pl.program_id(0)

    @pl.when(s == 0)
    def _():
        ya_scr[...] = jnp.zeros_like(ya_scr)
        hbuf[...] = jnp.zeros_like(hbuf)

    keep = jnp.where(s % nt == 0, 0.0, 1.0)
    hbuf[0:CONV_HALO, :] = hbuf[TM:TM + CONV_HALO, :] * keep
    hbuf[CONV_HALO:, :] = glu_ref[...]
    for rho in range(SUBLANES):
        rows = TM + CONV_HALO - (SUBLANES if rho else 0)
        sbuf[rho, 0:rows, :] = hbuf[rho:rho + rows, :]
    conv_rows = list(range(0, TM, CONV_SUB))
    per_chunk = -(-len(conv_rows) // len(FF_CHUNKS))

    mix_scr[:, 0:CONV_CH] = ya_scr[(s + 1) % 2]
    mix_scr[:, CONV_CH:CONV_CH + MOBA_DIM] = yb_ref[...]
    mix_scr[:, CONV_CH + MOBA_DIM:] = yc_ref[...]
    x1 = x_ref[...] + _dot(mix_scr[...], wo_ref[...])
    ms = jnp.mean(x1 * x1, axis=-1, keepdims=True)
    h2_scr[...] = (x1 * lax.rsqrt(ms + RMS_EPS) * g2_ref[...]).astype(BF16)
    acc_scr[...] = x1
    pending = []
    for n, (c0, c1) in enumerate(FF_CHUNKS):
        g = _dot(h2_scr[...], wgu_ref[:, c0:c1])
        u = _dot(h2_scr[...], wgu_ref[:, D_FF + c0:D_FF + c1])
        act = (g * _sigmoid(g) * u).astype(BF16)
        for r0 in pending:
            done = ya_scr[s % 2, r0:r0 + BF16_ROWS, 0:LANES].astype(F32)
            acc_scr[0:BF16_ROWS, 0:LANES] += jnp.minimum(jnp.abs(done), 0.0)
        acc_scr[...] += _dot(act, wd_ref[c0:c1, :])
        begin = jnp.minimum(jnp.abs(act[0:1, 0:CONV_CH].astype(F32)), 0.0)
        pending = conv_rows[n * per_chunk:(n + 1) * per_chunk]
        for r0 in pending:
            ya_scr[s % 2, r0:r0 + CONV_SUB, :] = _conv_rows(cw_ref, cb_ref, clg_ref, clb_ref, sbuf, r0, begin)
    o_ref[...] = acc_scr[...]


def _out_ffn(x2, glu, yb, yc, cw, cb, clg, clb, wo, g2, wgu, wd, seq):
    n = x2.shape[0]
    steps = n // TM + 1
    const = lambda shape: pl.BlockSpec(shape, lambda s: (0,) * len(shape), pipeline_mode=pl.Buffered(1))
    rows = lambda width: pl.BlockSpec((TM, width), lambda s: (jnp.maximum(s - 1, 0), 0))
    return pl.pallas_call(
        functools.partial(_out_ffn_kernel, seq // TM),
        grid=(steps,),
        in_specs=[rows(D_MODEL), pl.BlockSpec((TM, CONV_CH), lambda s: (jnp.minimum(s, steps - 2), 0)),
                  rows(MOBA_DIM), rows(FOX_DIM),
                  const((CONV_WIDTH, CONV_CH)), const((1, CONV_CH)), const((1, CONV_CH)), const((1, CONV_CH)),
                  const((D_MODEL, D_MODEL)), const((1, D_MODEL)), const((D_MODEL, 2 * D_FF)),
                  const((D_FF, D_MODEL))],
        out_specs=rows(D_MODEL),
        out_shape=jax.ShapeDtypeStruct((n, D_MODEL), F32),
        scratch_shapes=[pltpu.VMEM((2, TM, CONV_CH), BF16),
                        pltpu.VMEM((CONV_HALO + TM, CONV_CH), F32),
                        pltpu.VMEM((SUBLANES, CONV_HALO + TM, CONV_CH), F32),
                        pltpu.VMEM((TM, D_MODEL), BF16), pltpu.VMEM((TM, D_MODEL), BF16),
                        pltpu.VMEM((TM, D_MODEL), F32)],
        compiler_params=pltpu.CompilerParams(dimension_semantics=("arbitrary",),
                                             vmem_limit_bytes=VMEM_LIMIT),
        name="out_ffn",
    )(x2, glu, yb, yc, cw, cb, clg, clb, wo, g2, wgu, wd)


def kernel(x, w_in, b_forget, conv_w, conv_b, conv_ln_g, conv_ln_b, moba_qn_g, moba_kn_g,
           fox_qn_g, fox_kn_g, rel_bias, w_out, norm1_g, norm2_g, w_gate_up, w_down):
    batch, seq, d_model = x.shape
    depth = w_in.shape[0]
    assert d_model == D_MODEL and seq % TM == 0 and seq % MOBA_BLOCK == 0 and TQ == MOBA_BLOCK == TK
    assert w_in.shape[2] == IN_MAIN + FOX_HEADS and MOBA_HEADS == FOX_HEADS <= SUBLANES
    assert seq // MOBA_BLOCK <= SEL_LANES and HEAD_DIM + 3 * SEL_LANES <= LANES

    hd = jnp.arange(HM) // HEAD_DIM
    head_mean = jnp.where(hd[:, None] == hd[None, :], 1.0 / HEAD_DIM, 0.0).astype(BF16)
    tri = (jnp.arange(TM)[:, None] >= jnp.arange(TM)[None, :]).astype(BF16)
    placement = _placement_constants()

    bias = _moba_bias(rel_bias.astype(F32))
    x2 = x.reshape(batch * seq, D_MODEL)
    for l in range(depth):
        w = _in_proj_weight(w_in[l])
        bf = jnp.pad(b_forget[l].astype(F32), (0, LANES - FOX_HEADS)).reshape(1, LANES)
        gains = jnp.stack([jnp.concatenate([gq[l], gk[l]]).astype(F32)
                           for gq, gk in ((moba_qn_g, moba_kn_g), (fox_qn_g, fox_kn_g))])
        bound = (HEAD_DIM * QSCALE * NORM_SLACK * jnp.max(jnp.abs(fox_qn_g[l])) * jnp.max(jnp.abs(fox_kn_g[l]))
                 ).astype(F32).reshape(1)
        glu, qb, kb, vbT, qc, kc, vcT, cmax, cmin = _in_proj(
            x2, norm1_g[l].reshape(1, D_MODEL), w, bf, gains, head_mean, tri, placement, batch, seq)
        yb = _moba(qb, kb, vbT, bias, rel_bias.astype(F32), batch, seq)
        yc = _fox(qc, kc, vcT, cmax, cmin, bound, batch, seq)
        x2 = _out_ffn(x2, glu, yb, yc, conv_w[l], conv_b[l].reshape(1, CONV_CH),
                      conv_ln_g[l].reshape(1, CONV_CH), conv_ln_b[l].reshape(1, CONV_CH),
                      w_out[l].astype(BF16), norm2_g[l].reshape(1, D_MODEL),
                      w_gate_up[l].astype(BF16), w_down[l].astype(BF16), seq)
    return x2.reshape(batch, seq, D_MODEL)
```

```python
import functools
import math

import numpy as np
import jax
import jax.numpy as jnp
from jax import lax
from jax.experimental import pallas as pl
from jax.experimental.pallas import tpu as pltpu

F32 = jnp.float32
BF16 = jnp.bfloat16

D_MODEL = 1024
HEAD_DIM = 64
CONV_CH = D_MODEL // 4
MOBA_DIM = 3 * D_MODEL // 8
FOX_DIM = D_MODEL - CONV_CH - MOBA_DIM
MOBA_HEADS = MOBA_DIM // HEAD_DIM
FOX_HEADS = FOX_DIM // HEAD_DIM
HEADS = MOBA_HEADS
CONV_WIDTH = 31
LN_EPS = 1e-5
RMS_EPS = 1e-6
MOBA_BLOCK = 256
MOBA_TOPK = 3
NUM_BUCKETS = 32
MAX_DISTANCE = 128
D_FF = -(-8 * D_MODEL // (3 * 256)) * 256
NEG = -1e30
LOG2E = math.log2(math.e)
QSCALE = HEAD_DIM ** -0.5 * LOG2E

LANES = 128
SUBLANES = 8
BF16_ROWS = 16
IN_MAIN = 2 * CONV_CH + 3 * MOBA_DIM + 3 * FOX_DIM
IN_PAD = IN_MAIN + LANES
AUG = HEADS * LANES
HM = 256
SEL_LANES = 16
V_ROWS = HEAD_DIM + BF16_ROWS

TM = 512
TQ = 256
TK = 256
HEAD_GROUP = 6
MOBA_MERGE = 8
FOX_MERGE = 4
LAZY_LIMIT = 64.0
UNDERFLOW_LOG2 = 160.0
ROUNDING_MARGIN = 2.0
NORM_SLACK = 1.01
CONV_TC = 256
CONV_HALO = 32
CONV_SUB = 64
FF_CHUNKS = ((0, 768), (768, 1536), (1536, 2304), (2304, 2816))
VMEM_LIMIT = 56 * 1024 * 1024


def _split3(x):
    hi = x.astype(BF16)
    r1 = x - hi.astype(F32)
    mid = r1.astype(BF16)
    lo = (r1 - mid.astype(F32)).astype(BF16)
    return hi, mid, lo


def _dot(a, b):
    return jnp.dot(a, b, preferred_element_type=F32)


def _dot_nt(a, b):
    return lax.dot_general(a, b, (((1,), (1,)), ((), ())), preferred_element_type=F32)


def _sigmoid(x):
    return 1.0 / (1.0 + jnp.exp(-x))


def _group(h):
    return slice(h * LANES, (h + 1) * LANES)


def _placement_constants():
    cq = np.zeros((LANES, AUG), np.float32)
    ck = np.zeros((LANES, AUG), np.float32)
    ones_q = np.zeros((1, AUG), np.float32)
    ones_k = np.zeros((1, AUG), np.float32)
    for h in range(FOX_HEADS):
        for j in range(3):
            cq[j * SUBLANES + h, h * LANES + HEAD_DIM + j] = 1.0
            ones_k[0, h * LANES + HEAD_DIM + j] = 1.0
            ck[j * SUBLANES + h, h * LANES + HEAD_DIM + 3 + j] = -1.0
            ones_q[0, h * LANES + HEAD_DIM + 3 + j] = 1.0
    return jnp.asarray(cq, BF16), jnp.asarray(ck, BF16), jnp.asarray(ones_q), jnp.asarray(ones_k)


def _in_proj_weight(w):
    cols = list(range(2 * CONV_CH))
    values = []
    base = 2 * CONV_CH
    for width in (MOBA_DIM, FOX_DIM):
        q0, k0, v0 = base, base + width, base + 2 * width
        for h in range(HEADS):
            cols += list(range(q0 + h * HEAD_DIM, q0 + (h + 1) * HEAD_DIM))
            cols += list(range(k0 + h * HEAD_DIM, k0 + (h + 1) * HEAD_DIM))
        values += list(range(v0, v0 + width))
        base += 3 * width
    cols += values + list(range(IN_MAIN, IN_MAIN + FOX_HEADS))
    return jnp.pad(w.astype(BF16)[:, :, np.asarray(cols, np.int32)], ((0, 0), (0, 0), (0, IN_PAD - w.shape[2])))


def _in_proj_kernel(nt, x_ref, g1_ref, w_ref, bf_ref, gains_ref, hm_ref, tri_ref,
                    cq_ref, ck_ref, onesq_ref, onesk_ref,
                    glu_ref, qb_ref, kb_ref, vbT_ref, qc_ref, kc_ref, vcT_ref, cmax_ref, cmin_ref,
                    h_scr, carry_scr):
    r = pl.program_id(0)

    @pl.when(r % nt == 0)
    def _():
        carry_scr[...] = jnp.zeros_like(carry_scr)

    x = x_ref[...]
    ms = jnp.mean(x * x, axis=-1, keepdims=True)
    h_scr[...] = (x * lax.rsqrt(ms + RMS_EPS) * g1_ref[...]).astype(BF16)
    lane = lax.broadcasted_iota(jnp.int32, (TM, LANES), 1)
    low = lane < HEAD_DIM

    def seg(c0, c1):
        return _dot(h_scr[...], w_ref[0, :, c0:c1])

    def store_heads(y, q_ref, k_ref, gains, upper_q, upper_k):
        sq = y * y
        hi = sq.astype(BF16)
        lo = (sq - hi.astype(F32)).astype(BF16)
        hm = hm_ref[...]
        msq = jnp.concatenate([_dot(hi[:, c:c + HM], hm) + _dot(lo[:, c:c + HM], hm) for c in range(0, AUG, HM)],
                              axis=1)
        gain = gains * jnp.where(low[0:1, :], QSCALE, 1.0)
        for h in range(HEADS):
            g = y[:, _group(h)] * lax.rsqrt(msq[:, _group(h)] + RMS_EPS) * gain
            q_ref[:, _group(h)] = jnp.where(low, g, upper_q(h)).astype(BF16)
            k_ref[:, _group(h)] = jnp.where(low, pltpu.roll(g, HEAD_DIM, axis=1), upper_k(h)).astype(BF16)

    def pack_terms(t):
        hi, mid, lo = _split3(jnp.where(lane < FOX_HEADS, t, 0.0))
        return (hi.astype(F32) + pltpu.roll(mid.astype(F32), SUBLANES, axis=1)
                + pltpu.roll(lo.astype(F32), 2 * SUBLANES, axis=1)).astype(BF16)

    def unpack_terms(p):
        return p + pltpu.roll(p, LANES - SUBLANES, axis=1) + pltpu.roll(p, LANES - 2 * SUBLANES, axis=1)

    def store_vT(ref, v):
        vT = v.T.astype(BF16)
        row = lax.broadcasted_iota(jnp.int32, (BF16_ROWS, TM), 0)
        tail = jnp.where(row == 0, 1.0, 0.0).astype(BF16)
        for h in range(HEADS):
            ref[0, h * V_ROWS:h * V_ROWS + HEAD_DIM, :] = vT[h * HEAD_DIM:(h + 1) * HEAD_DIM, :]
            ref[0, h * V_ROWS + HEAD_DIM:(h + 1) * V_ROWS, :] = tail

    c_moba, c_fox, c_val = 2 * CONV_CH, 2 * CONV_CH + AUG, 2 * CONV_CH + 2 * AUG
    a = seg(0, 2 * CONV_CH)
    z = seg(IN_MAIN, IN_PAD) + bf_ref[...]
    y_moba = seg(c_moba, c_moba + AUG)
    glu_ref[...] = a[:, :CONV_CH] * _sigmoid(a[:, CONV_CH:])

    logf = jnp.minimum(z, 0.0) - jnp.log1p(jnp.exp(-jnp.abs(z)))
    csum = unpack_terms(_dot(tri_ref[...], pack_terms(logf)))
    c = csum + carry_scr[...]
    carry_scr[...] = c[TM - 1:TM, :]
    y_fox = seg(c_fox, c_fox + AUG)

    pos = (r % nt) * TM + lax.broadcasted_iota(jnp.int32, (TM, LANES), 0)
    in_sel = (lane >= HEAD_DIM) & (lane < HEAD_DIM + 3 * SEL_LANES)
    onehot = jnp.where(in_sel & ((lane & (SEL_LANES - 1)) == pos // MOBA_BLOCK), 1.0, 0.0)
    store_heads(y_moba, qb_ref, kb_ref, gains_ref[0:1, :], lambda h: 0.0, lambda h: onehot)

    c2 = c * LOG2E
    for t in range(TM // TQ):
        part = c2[t * TQ:(t + 1) * TQ, :]
        cmax_ref[pl.ds(r * (TM // TQ) + t, 1), :] = jnp.max(part, axis=0, keepdims=True)
        cmin_ref[pl.ds(r * (TM // TQ) + t, 1), :] = jnp.min(part, axis=0, keepdims=True)
    packed = pack_terms(c2)
    upper_q = _dot(packed, cq_ref[...]) + onesq_ref[...]
    upper_k = _dot(packed, ck_ref[...]) + onesk_ref[...]
    v = seg(c_val, c_val + MOBA_DIM + FOX_DIM)
    store_heads(y_fox, qc_ref, kc_ref, gains_ref[1:2, :],
                lambda h: upper_q[:, _group(h)], lambda h: upper_k[:, _group(h)])
    store_vT(vbT_ref, v[:, :MOBA_DIM])
    store_vT(vcT_ref, v[:, MOBA_DIM:])


def _in_proj(x2, g1, w, layer, bf, gains, hm, tri, placement, batch, seq):
    n = x2.shape[0]
    nt = seq // TM
    const = lambda shape: pl.BlockSpec(shape, lambda r: (0,) * len(shape), pipeline_mode=pl.Buffered(1))
    rows = lambda width: pl.BlockSpec((TM, width), lambda r: (r, 0))
    tposed = lambda height: pl.BlockSpec((1, height, TM), lambda r: (r // nt, 0, r % nt))
    out_shape = (
        jax.ShapeDtypeStruct((n, CONV_CH), F32),
        jax.ShapeDtypeStruct((n, AUG), BF16),
        jax.ShapeDtypeStruct((n, AUG), BF16),
        jax.ShapeDtypeStruct((batch, HEADS * V_ROWS, seq), BF16),
        jax.ShapeDtypeStruct((n, AUG), BF16),
        jax.ShapeDtypeStruct((n, AUG), BF16),
        jax.ShapeDtypeStruct((batch, HEADS * V_ROWS, seq), BF16),
        jax.ShapeDtypeStruct((n // TQ, LANES), F32),
        jax.ShapeDtypeStruct((n // TQ, LANES), F32),
    )
    whole = pl.BlockSpec((n // TQ, LANES), lambda r: (0, 0))
    return pl.pallas_call(
        functools.partial(_in_proj_kernel, nt),
        grid=(n // TM,),
        in_specs=[rows(D_MODEL), const((1, D_MODEL)),
                  pl.BlockSpec((1, D_MODEL, IN_PAD), lambda r: (layer, 0, 0), pipeline_mode=pl.Buffered(1)),
                  const((1, LANES)),
                  const((2, LANES)), const((HM, HM)), const((TM, TM)),
                  const((LANES, AUG)), const((LANES, AUG)), const((1, AUG)), const((1, AUG))],
        out_specs=(rows(CONV_CH), rows(AUG), rows(AUG), tposed(HEADS * V_ROWS),
                   rows(AUG), rows(AUG), tposed(HEADS * V_ROWS), whole, whole),
        out_shape=out_shape,
        scratch_shapes=[pltpu.VMEM((TM, D_MODEL), BF16), pltpu.VMEM((1, LANES), F32)],
        compiler_params=pltpu.CompilerParams(dimension_semantics=("arbitrary",),
                                             vmem_limit_bytes=VMEM_LIMIT),
        name="in_proj",
    )(x2, g1, w, bf, gains, hm, tri, *placement)


def _softmax_steps(score_fn, m_scr, acc_scr, first):
    for g0 in range(0, HEADS, HEAD_GROUP):
        hs = range(g0, min(g0 + HEAD_GROUP, HEADS))
        tiles = [score_fn(h) for h in hs]
        tile_max = [jnp.max(sT, axis=0, keepdims=True) for sT, _ in tiles]
        if first:
            m_new = tile_max
        else:
            m_old = [m_scr[h:h + 1, :] for h in hs]
            m_new = [jnp.maximum(a, b) for a, b in zip(m_old, tile_max)]
        pvs = [_dot(vT, jnp.exp2(sT - mn).astype(BF16)) for (sT, vT), mn in zip(tiles, m_new)]
        for n, h in enumerate(hs):
            if first:
                acc_scr[h] = pvs[n]
            else:
                acc_scr[h] = jnp.exp2(m_old[n] - m_new[n]) * acc_scr[h] + pvs[n]
            m_scr[h:h + 1, :] = m_new[n]


def _softmax_steps_lazy(score_fn, m_scr, acc_scr, pv_scr):
    worst = None
    tiles = [score_fn(h) for h in range(HEADS)]
    for h in range(HEADS):
        sT, vT = tiles[h]
        m_old = m_scr[h:h + 1, :]
        tile_max = jnp.max(sT, axis=0, keepdims=True)
        pv_scr[h] = _dot(vT, jnp.exp2(sT - m_old).astype(BF16))
        m_scr[SUBLANES + h:SUBLANES + h + 1, :] = tile_max
        rise = tile_max - m_old
        worst = rise if worst is None else jnp.maximum(worst, rise)
    safe = jnp.max(worst) <= LAZY_LIMIT

    @pl.when(safe)
    def _():
        for h in range(HEADS):
            m_old = m_scr[h:h + 1, :]
            m_new = jnp.maximum(m_old, m_scr[SUBLANES + h:SUBLANES + h + 1, :])
            acc_scr[h] = jnp.exp2(m_old - m_new) * (acc_scr[h] + pv_scr[h])
            m_scr[h:h + 1, :] = m_new

    @pl.when(jnp.logical_not(safe))
    def _():
        _softmax_steps(score_fn, m_scr, acc_scr, first=False)


def _write_heads(o_ref, acc_scr):
    for pair in range(HEADS // 2):
        both = []
        for h in (2 * pair, 2 * pair + 1):
            acc = acc_scr[h]
            both.append(acc[:HEAD_DIM, :] / acc[HEAD_DIM:HEAD_DIM + 1, :])
        o_ref[:, pair * LANES:(pair + 1) * LANES] = jnp.concatenate(both, axis=0).T.astype(BF16)


def _key_tile(k_ref, vT_ref, h, k0, tk=TK):
    return k_ref[pl.ds(k0, tk), _group(h)], vT_ref[0, h * V_ROWS:(h + 1) * V_ROWS, pl.ds(k0, tk)]


def _past_tiles(first, count, step, merge):
    big = merge * TK

    def body(j, carry):
        step(pl.multiple_of(first * TK + j * big, TK), big)
        return carry

    lax.fori_loop(0, count // merge, body, 0)
    done = first + (count // merge) * merge
    count = first + count
    part = merge // 2
    while part >= 1:
        rem = count - done

        @pl.when(rem >= part)
        def _(done=done, part=part):
            step(pl.multiple_of(done * TK, TK), part * TK)

        done = done + jnp.where(rem >= part, part, 0)
        part //= 2


_ATTN_SCRATCH = [
    pltpu.VMEM((2 * SUBLANES, TQ), F32),
    pltpu.VMEM((HEADS, V_ROWS, TQ), F32),
    pltpu.VMEM((HEADS, V_ROWS, TQ), F32),
]


def _attention_specs(batch, seq):
    nq = seq // TQ
    in_specs = [pl.BlockSpec((TQ, AUG), lambda b, i: (b * nq + i, 0)),
                pl.BlockSpec((seq, AUG), lambda b, i: (b, 0)),
                pl.BlockSpec((1, HEADS * V_ROWS, seq), lambda b, i: (b, 0, 0))]
    out_spec = pl.BlockSpec((TQ, MOBA_DIM), lambda b, i: (b * nq + i, 0))
    params = pltpu.CompilerParams(dimension_semantics=("arbitrary", "arbitrary"), vmem_limit_bytes=VMEM_LIMIT)
    return (batch, nq), in_specs, out_spec, params


def _t5_bucket(dist):
    n = jnp.maximum(dist, 0)
    max_exact = NUM_BUCKETS // 2
    nf = jnp.maximum(n, 1).astype(F32)
    large = max_exact + (jnp.log(nf / max_exact) / math.log(MAX_DISTANCE / max_exact)
                         * (NUM_BUCKETS - max_exact)).astype(jnp.int32)
    large = jnp.minimum(large, NUM_BUCKETS - 1)
    return jnp.where(n < max_exact, n, large)


def _bias_kernel(bucket_ref, rb_ref, o_ref):
    h = pl.program_id(0)
    bucket = bucket_ref[...]
    acc = jnp.zeros(bucket.shape, F32)
    for b in range(NUM_BUCKETS):
        acc = jnp.where(bucket == b, rb_ref[b, h], acc)
    key = lax.broadcasted_iota(jnp.int32, bucket.shape, 0)
    qry = lax.broadcasted_iota(jnp.int32, bucket.shape, 1)
    o_ref[0] = jnp.where((key < MOBA_BLOCK) | (key - MOBA_BLOCK <= qry), acc * LOG2E, NEG)


def _moba_bias(rel_bias):
    key = lax.broadcasted_iota(jnp.int32, (2 * MOBA_BLOCK, TQ), 0)
    qry = lax.broadcasted_iota(jnp.int32, (2 * MOBA_BLOCK, TQ), 1)
    bucket = _t5_bucket(qry + MOBA_BLOCK - key)
    return pl.pallas_call(
        _bias_kernel,
        grid=(MOBA_HEADS,),
        in_specs=[pl.BlockSpec((2 * MOBA_BLOCK, TQ), lambda h: (0, 0)),
                  pl.BlockSpec(memory_space=pltpu.SMEM)],
        out_specs=pl.BlockSpec((1, 2 * MOBA_BLOCK, TQ), lambda h: (h, 0, 0)),
        out_shape=jax.ShapeDtypeStruct((MOBA_HEADS, 2 * MOBA_BLOCK, TQ), F32),
        compiler_params=pltpu.CompilerParams(dimension_semantics=("arbitrary",)),
        name="moba_bias",
    )(bucket, rel_bias)


def _moba_kernel(nblk, q_ref, k_ref, vT_ref, bias_ref, rb_ref, o_ref, m_scr, acc_scr, pv_scr, qm_scr, kmean_scr):
    i = pl.program_id(1)

    @pl.when(i == 0)
    def _():
        kmean_scr[...] = jnp.zeros_like(kmean_scr)
        for n in range(nblk):
            blk = k_ref[n * MOBA_BLOCK:(n + 1) * MOBA_BLOCK, :].astype(F32)
            kmean_scr[n:n + 1, :] = jnp.mean(blk, axis=0, keepdims=True)

    q0 = pl.multiple_of(i * TQ, TQ)
    blk_row = lax.broadcasted_iota(jnp.int32, (SEL_LANES, TQ), 0)

    for h in range(HEADS):
        qg = q_ref[:, _group(h)]
        terms = _dot_nt(jnp.concatenate(_split3(kmean_scr[:, _group(h)]), axis=0), qg)
        gate = terms[:SEL_LANES] + terms[SEL_LANES:2 * SEL_LANES] + terms[2 * SEL_LANES:]
        gate = jnp.where(blk_row < i, gate, -jnp.inf)
        picked = jnp.zeros((SEL_LANES, TQ), F32)
        for _ in range(MOBA_TOPK):
            best = jnp.max(gate, axis=0, keepdims=True)
            first = jnp.min(jnp.where(gate == best, blk_row, SEL_LANES), axis=0, keepdims=True)
            pick = blk_row == first
            picked = jnp.where(pick, 1.0, picked)
            gate = jnp.where(pick, -jnp.inf, gate)
        chosen = (picked > 0.0) & (blk_row < i)
        far = jnp.where(blk_row == i - 1, 0.0, rb_ref[NUM_BUCKETS - 1, h] * LOG2E)
        term = jnp.where(chosen, far, jnp.where(blk_row < i, NEG, 0.0))
        t_hi, t_mid, t_lo = _split3(term)
        lanes = jnp.concatenate(
            [jnp.zeros((HEAD_DIM, TQ), F32), t_hi.astype(F32), t_mid.astype(F32), t_lo.astype(F32),
             jnp.zeros((LANES - HEAD_DIM - 3 * SEL_LANES, TQ), F32)], axis=0)
        qm_scr[h] = (qg.astype(F32) + lanes.T).astype(BF16)

    @pl.when(i == 0)
    def _():
        def own_scores(h):
            kj, vTj = _key_tile(k_ref, vT_ref, h, q0)
            return _dot_nt(kj, qm_scr[h]) + bias_ref[h, MOBA_BLOCK:, :], vTj

        _softmax_steps(own_scores, m_scr, acc_scr, first=True)

    @pl.when(i >= 1)
    def _():
        k0 = pl.multiple_of((i - 1) * TK, TK)

        def near_scores(h):
            kj, vTj = _key_tile(k_ref, vT_ref, h, k0, 2 * TK)
            return _dot_nt(kj, qm_scr[h]) + bias_ref[h], vTj

        _softmax_steps(near_scores, m_scr, acc_scr, first=True)

    def far_step(k0, tk):
        def far_scores(h):
            kj, vTj = _key_tile(k_ref, vT_ref, h, k0, tk)
            return _dot_nt(kj, qm_scr[h]), vTj

        _softmax_steps_lazy(far_scores, m_scr, acc_scr, pv_scr)

    _past_tiles(0, jnp.maximum(i - 1, 0), far_step, MOBA_MERGE)
    _write_heads(o_ref, acc_scr)


def _moba(q, k, vT, bias, rel_bias, batch, seq):
    nblk = seq // MOBA_BLOCK
    grid, in_specs, out_spec, params = _attention_specs(batch, seq)
    return pl.pallas_call(
        functools.partial(_moba_kernel, nblk),
        grid=grid,
        in_specs=in_specs + [pl.BlockSpec((MOBA_HEADS, 2 * MOBA_BLOCK, TQ), lambda b, i: (0, 0, 0)),
                             pl.BlockSpec(memory_space=pltpu.SMEM)],
        out_specs=out_spec,
        out_shape=jax.ShapeDtypeStruct((q.shape[0], MOBA_DIM), BF16),
        scratch_shapes=_ATTN_SCRATCH + [pltpu.VMEM((HEADS, TQ, LANES), BF16),
                                        pltpu.VMEM((SEL_LANES, AUG), F32)],
        compiler_params=params,
        name="moba_attention",
    )(q, k, vT, bias, rel_bias)


def _fox_kernel(q_ref, k_ref, vT_ref, cmax_ref, cmin_ref, bound_ref, o_ref, m_scr, acc_scr, pv_scr):
    i = pl.program_id(1)
    q0 = pl.multiple_of(i * TQ, TQ)

    ntiles = cmin_ref.shape[0]
    gap = cmax_ref[pl.ds(i, 1), :] - cmin_ref[...]
    limit = -(2.0 * bound_ref[0] + (UNDERFLOW_LOG2 + ROUNDING_MARGIN))
    head_lane = lax.broadcasted_iota(jnp.int32, (ntiles, LANES), 1) < FOX_HEADS
    live = jnp.max(jnp.where(head_lane & (gap > limit), 1, 0), axis=1, keepdims=True)
    tile = lax.broadcasted_iota(jnp.int32, (ntiles, 1), 0)
    first_live = jnp.min(jnp.where((live > 0) & (tile < i), tile, i))

    def scores(h, k0, tk=TK):
        kj, vTj = _key_tile(k_ref, vT_ref, h, k0, tk)
        return _dot_nt(kj, q_ref[:, _group(h)]), vTj

    key = lax.broadcasted_iota(jnp.int32, (TK, TQ), 0)
    qry = lax.broadcasted_iota(jnp.int32, (TK, TQ), 1)

    def diag_scores(h):
        sT, vTj = scores(h, q0)
        return jnp.where(key <= qry, sT, NEG), vTj

    _softmax_steps(diag_scores, m_scr, acc_scr, first=True)

    def past_step(k0, tk):
        _softmax_steps_lazy(lambda h: scores(h, k0, tk), m_scr, acc_scr, pv_scr)

    _past_tiles(first_live, i - first_live, past_step, FOX_MERGE)
    _write_heads(o_ref, acc_scr)


def _fox(q, k, vT, cmax, cmin, bound, batch, seq):
    grid, in_specs, out_spec, params = _attention_specs(batch, seq)
    ntiles = seq // TQ
    return pl.pallas_call(
        _fox_kernel,
        grid=grid,
        in_specs=in_specs + [pl.BlockSpec((ntiles, LANES), lambda b, i: (b, 0)),
                             pl.BlockSpec((ntiles, LANES), lambda b, i: (b, 0)),
                             pl.BlockSpec(memory_space=pltpu.SMEM)],
        out_specs=out_spec,
        out_shape=jax.ShapeDtypeStruct((q.shape[0], FOX_DIM), BF16),
        scratch_shapes=_ATTN_SCRATCH,
        compiler_params=params,
        name="fox_attention",
    )(q, k, vT, cmax, cmin, bound)


def _conv_rows(w_ref, b_ref, lg_ref, lb_ref, sbuf, r0):
    shift = CONV_HALO - (CONV_WIDTH - 1)
    acc = jnp.zeros((CONV_SUB, CONV_CH), F32) + b_ref[...]
    for j in range(CONV_WIDTH):
        rho, base = (j + shift) % SUBLANES, (j + shift) // SUBLANES * SUBLANES
        acc = acc + w_ref[j:j + 1, :] * sbuf[rho, r0 + base:r0 + base + CONV_SUB, :]
    mu = jnp.mean(acc, axis=-1, keepdims=True)
    d = acc - mu
    var = jnp.mean(d * d, axis=-1, keepdims=True)
    y = d * lax.rsqrt(var + LN_EPS) * lg_ref[...] + lb_ref[...]
    return (y * _sigmoid(y)).astype(BF16)


def _conv_kernel(nt, glu_ref, w_ref, b_ref, lg_ref, lb_ref, o_ref, hbuf, sbuf):
    t = pl.program_id(0) % nt

    @pl.when(t == 0)
    def _():
        hbuf[0:CONV_HALO, :] = jnp.zeros((CONV_HALO, CONV_CH), F32)

    @pl.when(t > 0)
    def _():
        hbuf[0:CONV_HALO, :] = hbuf[CONV_TC:CONV_TC + CONV_HALO, :]

    hbuf[CONV_HALO:, :] = glu_ref[...]
    for rho in range(SUBLANES):
        rows = CONV_TC + CONV_HALO - (SUBLANES if rho else 0)
        sbuf[rho, 0:rows, :] = hbuf[rho:rho + rows, :]
    for r0 in range(0, CONV_TC, CONV_SUB):
        o_ref[r0:r0 + CONV_SUB, :] = _conv_rows(w_ref, b_ref, lg_ref, lb_ref, sbuf, r0)


def _conv(glu, w, b, lg, lb, seq):
    n = glu.shape[0]
    const = lambda shape: pl.BlockSpec(shape, lambda r: (0,) * len(shape))
    return pl.pallas_call(
        functools.partial(_conv_kernel, seq // CONV_TC),
        grid=(n // CONV_TC,),
        in_specs=[pl.BlockSpec((CONV_TC, CONV_CH), lambda r: (r, 0)),
                  const((CONV_WIDTH, CONV_CH)), const((1, CONV_CH)), const((1, CONV_CH)), const((1, CONV_CH))],
        out_specs=pl.BlockSpec((CONV_TC, CONV_CH), lambda r: (r, 0)),
        out_shape=jax.ShapeDtypeStruct((n, CONV_CH), BF16),
        scratch_shapes=[pltpu.VMEM((CONV_HALO + CONV_TC, CONV_CH), F32),
                        pltpu.VMEM((SUBLANES, CONV_HALO + CONV_TC, CONV_CH), F32)],
        compiler_params=pltpu.CompilerParams(dimension_semantics=("arbitrary",)),
        name="conv_module",
    )(glu, w, b, lg, lb)


def _out_ffn_kernel(x_ref, ya_ref, yb_ref, yc_ref, wo_ref, g2_ref, wgu_ref, wd_ref, o_ref,
                    mix_scr, h2_scr, acc_scr):
    mix_scr[:, 0:CONV_CH] = ya_ref[...]
    mix_scr[:, CONV_CH:CONV_CH + MOBA_DIM] = yb_ref[...]
    mix_scr[:, CONV_CH + MOBA_DIM:] = yc_ref[...]
    x1 = x_ref[...] + _dot(mix_scr[...], wo_ref[0])
    ms = jnp.mean(x1 * x1, axis=-1, keepdims=True)
    h2_scr[...] = (x1 * lax.rsqrt(ms + RMS_EPS) * g2_ref[...]).astype(BF16)
    acc_scr[...] = x1
    for c0, c1 in FF_CHUNKS:
        g = _dot(h2_scr[...], wgu_ref[0, :, c0:c1])
        u = _dot(h2_scr[...], wgu_ref[0, :, D_FF + c0:D_FF + c1])
        act = (g * _sigmoid(g) * u).astype(BF16)
        acc_scr[...] += _dot(act, wd_ref[0, c0:c1, :])
    o_ref[...] = acc_scr[...]


def _out_ffn(x2, ya, yb, yc, wo, g2, wgu, wd, layer):
    n = x2.shape[0]
    const = lambda shape: pl.BlockSpec(shape, lambda r: (0,) * len(shape), pipeline_mode=pl.Buffered(1))
    of_layer = lambda shape: pl.BlockSpec((1,) + shape, lambda r: (layer, 0, 0), pipeline_mode=pl.Buffered(1))
    rows = lambda width: pl.BlockSpec((TM, width), lambda r: (r, 0))
    return pl.pallas_call(
        _out_ffn_kernel,
        grid=(n // TM,),
        in_specs=[rows(D_MODEL), rows(CONV_CH), rows(MOBA_DIM), rows(FOX_DIM),
                  of_layer((D_MODEL, D_MODEL)), const((1, D_MODEL)), of_layer((D_MODEL, 2 * D_FF)),
                  of_layer((D_FF, D_MODEL))],
        out_specs=rows(D_MODEL),
        out_shape=jax.ShapeDtypeStruct((n, D_MODEL), F32),
        scratch_shapes=[pltpu.VMEM((TM, D_MODEL), BF16), pltpu.VMEM((TM, D_MODEL), BF16),
                        pltpu.VMEM((TM, D_MODEL), F32)],
        compiler_params=pltpu.CompilerParams(dimension_semantics=("arbitrary",),
                                             vmem_limit_bytes=VMEM_LIMIT),
        name="out_ffn",
    )(x2, ya, yb, yc, wo, g2, wgu, wd)


def kernel(x, w_in, b_forget, conv_w, conv_b, conv_ln_g, conv_ln_b, moba_qn_g, moba_kn_g,
           fox_qn_g, fox_kn_g, rel_bias, w_out, norm1_g, norm2_g, w_gate_up, w_down):
    batch, seq, d_model = x.shape
    depth = w_in.shape[0]
    assert d_model == D_MODEL and seq % TM == 0 and seq % MOBA_BLOCK == 0 and TQ == MOBA_BLOCK == TK
    assert w_in.shape[2] == IN_MAIN + FOX_HEADS and MOBA_HEADS == FOX_HEADS <= SUBLANES
    assert seq // MOBA_BLOCK <= SEL_LANES and HEAD_DIM + 3 * SEL_LANES <= LANES

    hd = jnp.arange(HM) // HEAD_DIM
    head_mean = jnp.where(hd[:, None] == hd[None, :], 1.0 / HEAD_DIM, 0.0).astype(BF16)
    tri = (jnp.arange(TM)[:, None] >= jnp.arange(TM)[None, :]).astype(BF16)
    placement = _placement_constants()

    bias = _moba_bias(rel_bias.astype(F32))
    x2 = x.reshape(batch * seq, D_MODEL)
    w_all = _in_proj_weight(w_in)
    wo_all, wgu_all, wd_all = w_out.astype(BF16), w_gate_up.astype(BF16), w_down.astype(BF16)
    for l in range(depth):
        bf = jnp.pad(b_forget[l].astype(F32), (0, LANES - FOX_HEADS)).reshape(1, LANES)
        gains = jnp.stack([jnp.concatenate([gq[l], gk[l]]).astype(F32)
                           for gq, gk in ((moba_qn_g, moba_kn_g), (fox_qn_g, fox_kn_g))])
        bound = (HEAD_DIM * QSCALE * NORM_SLACK * jnp.max(jnp.abs(fox_qn_g[l])) * jnp.max(jnp.abs(fox_kn_g[l]))
                 ).astype(F32).reshape(1)
        glu, qb, kb, vbT, qc, kc, vcT, cmax, cmin = _in_proj(
            x2, norm1_g[l].reshape(1, D_MODEL), w_all, l, bf, gains, head_mean, tri, placement, batch, seq)
        ya = _conv(glu, conv_w[l], conv_b[l].reshape(1, CONV_CH), conv_ln_g[l].reshape(1, CONV_CH),
                   conv_ln_b[l].reshape(1, CONV_CH), seq)
        yb = _moba(qb, kb, vbT, bias, rel_bias.astype(F32), batch, seq)
        yc = _fox(qc, kc, vcT, cmax, cmin, bound, batch, seq)
        x2 = _out_ffn(x2, ya, yb, yc, wo_all, norm2_g[l].reshape(1, D_MODEL), wgu_all, wd_all, l)
    return x2.reshape(batch, seq, D_MODEL)
```

```python
import functools
import math

import numpy as np
import jax
import jax.numpy as jnp
from jax import lax
from jax.experimental import pallas as pl
from jax.experimental.pallas import tpu as pltpu

F32 = jnp.float32
BF16 = jnp.bfloat16

D_MODEL = 1024
HEAD_DIM = 64
CONV_CH = D_MODEL // 4
MOBA_DIM = 3 * D_MODEL // 8
FOX_DIM = D_MODEL - CONV_CH - MOBA_DIM
MOBA_HEADS = MOBA_DIM // HEAD_DIM
FOX_HEADS = FOX_DIM // HEAD_DIM
HEADS = MOBA_HEADS
CONV_WIDTH = 31
LN_EPS = 1e-5
RMS_EPS = 1e-6
MOBA_BLOCK = 256
MOBA_TOPK = 3
NUM_BUCKETS = 32
MAX_DISTANCE = 128
D_FF = -(-8 * D_MODEL // (3 * 256)) * 256
NEG = -1e30
LOG2E = math.log2(math.e)
QSCALE = HEAD_DIM ** -0.5 * LOG2E

LANES = 128
SUBLANES = 8
BF16_ROWS = 16
IN_MAIN = 2 * CONV_CH + 3 * MOBA_DIM + 3 * FOX_DIM
IN_PAD = IN_MAIN + LANES
AUG = HEADS * LANES
HM = 256
SEL_LANES = 16
V_ROWS = HEAD_DIM + BF16_ROWS

TM = 512
TQ = 256
TK = 256
HEAD_GROUP = 6
MOBA_MERGE = 8
FOX_MERGE = 4
LAZY_LIMIT = 64.0
UNDERFLOW_LOG2 = 160.0
ROUNDING_MARGIN = 2.0
NORM_SLACK = 1.01
CONV_TC = 256
CONV_HALO = 32
CONV_SUB = 64
FF_CHUNKS = ((0, 768), (768, 1536), (1536, 2304), (2304, 2816))
VMEM_LIMIT = 56 * 1024 * 1024


def _split3(x):
    hi = x.astype(BF16)
    r1 = x - hi.astype(F32)
    mid = r1.astype(BF16)
    lo = (r1 - mid.astype(F32)).astype(BF16)
    return hi, mid, lo


def _dot(a, b):
    return jnp.dot(a, b, preferred_element_type=F32)


def _dot_nt(a, b):
    return lax.dot_general(a, b, (((1,), (1,)), ((), ())), preferred_element_type=F32)


def _sigmoid(x):
    return 1.0 / (1.0 + jnp.exp(-x))


def _group(h):
    return slice(h * LANES, (h + 1) * LANES)


def _placement_constants():
    cq = np.zeros((LANES, AUG), np.float32)
    ck = np.zeros((LANES, AUG), np.float32)
    ones_q = np.zeros((1, AUG), np.float32)
    ones_k = np.zeros((1, AUG), np.float32)
    for h in range(FOX_HEADS):
        for j in range(3):
            cq[j * SUBLANES + h, h * LANES + HEAD_DIM + j] = 1.0
            ones_k[0, h * LANES + HEAD_DIM + j] = 1.0
            ck[j * SUBLANES + h, h * LANES + HEAD_DIM + 3 + j] = -1.0
            ones_q[0, h * LANES + HEAD_DIM + 3 + j] = 1.0
    return jnp.asarray(cq, BF16), jnp.asarray(ck, BF16), jnp.asarray(ones_q), jnp.asarray(ones_k)


def _in_proj_weight(w):
    return jnp.pad(w.astype(BF16), ((0, 0), (0, 0), (0, IN_PAD - w.shape[2])))


def _in_proj_kernel(nt, x_ref, g1_ref, w_ref, bf_ref, gains_ref, hm_ref, tri_ref,
                    cq_ref, ck_ref, onesq_ref, onesk_ref,
                    glu_ref, qb_ref, kb_ref, vbT_ref, qc_ref, kc_ref, vcT_ref, cmax_ref, cmin_ref,
                    h_scr, carry_scr):
    r = pl.program_id(0)

    @pl.when(r % nt == 0)
    def _():
        carry_scr[...] = jnp.zeros_like(carry_scr)

    x = x_ref[...]
    ms = jnp.mean(x * x, axis=-1, keepdims=True)
    h_scr[...] = (x * lax.rsqrt(ms + RMS_EPS) * g1_ref[...]).astype(BF16)
    lane = lax.broadcasted_iota(jnp.int32, (TM, LANES), 1)
    low = lane < HEAD_DIM

    def seg(c0, c1):
        return _dot(h_scr[...], w_ref[0, :, c0:c1])

    def store_heads(yq, yk, q_ref, k_ref, gains, upper_q, upper_k):
        y = jnp.concatenate([yq, yk], axis=1)
        sq = y * y
        hi = sq.astype(BF16)
        lo = (sq - hi.astype(F32)).astype(BF16)
        hm = hm_ref[...]
        msq = jnp.concatenate([_dot(hi[:, c:c + HM], hm) + _dot(lo[:, c:c + HM], hm) for c in range(0, AUG, HM)],
                              axis=1)
        y = y * lax.rsqrt(msq + RMS_EPS) * gains
        for ref, upper, first in ((q_ref, upper_q, 0), (k_ref, upper_k, HEADS // 2)):
            for pair in range(HEADS // 2):
                both = y[:, _group(first + pair)]
                ref[:, _group(2 * pair)] = jnp.where(low, both, upper(2 * pair)).astype(BF16)
                ref[:, _group(2 * pair + 1)] = jnp.where(
                    low, pltpu.roll(both, HEAD_DIM, axis=1), upper(2 * pair + 1)).astype(BF16)

    def pack_terms(t):
        hi, mid, lo = _split3(jnp.where(lane < FOX_HEADS, t, 0.0))
        return (hi.astype(F32) + pltpu.roll(mid.astype(F32), SUBLANES, axis=1)
                + pltpu.roll(lo.astype(F32), 2 * SUBLANES, axis=1)).astype(BF16)

    def unpack_terms(p):
        return p + pltpu.roll(p, LANES - SUBLANES, axis=1) + pltpu.roll(p, LANES - 2 * SUBLANES, axis=1)

    def store_vT(ref, v):
        vT = v.T.astype(BF16)
        row = lax.broadcasted_iota(jnp.int32, (BF16_ROWS, TM), 0)
        tail = jnp.where(row == 0, 1.0, 0.0).astype(BF16)
        for h in range(HEADS):
            ref[0, h * V_ROWS:h * V_ROWS + HEAD_DIM, :] = vT[h * HEAD_DIM:(h + 1) * HEAD_DIM, :]
            ref[0, h * V_ROWS + HEAD_DIM:(h + 1) * V_ROWS, :] = tail

    col_qk = 2 * CONV_CH
    col_vq, col_kv = col_qk + 2 * MOBA_DIM, col_qk + 3 * MOBA_DIM + FOX_DIM
    a = seg(0, col_qk)
    z = seg(IN_MAIN, IN_PAD) + bf_ref[...]
    y1 = seg(col_qk, col_vq)
    glu_ref[...] = a[:, :CONV_CH] * _sigmoid(a[:, CONV_CH:])

    logf = jnp.minimum(z, 0.0) - jnp.log1p(jnp.exp(-jnp.abs(z)))
    csum = unpack_terms(_dot(tri_ref[...], pack_terms(logf)))
    c = csum + carry_scr[...]
    carry_scr[...] = c[TM - 1:TM, :]
    y2 = seg(col_vq, col_kv)

    pos = (r % nt) * TM + lax.broadcasted_iota(jnp.int32, (TM, LANES), 0)
    in_sel = (lane >= HEAD_DIM) & (lane < HEAD_DIM + 3 * SEL_LANES)
    onehot = jnp.where(in_sel & ((lane & (SEL_LANES - 1)) == pos // MOBA_BLOCK), 1.0, 0.0)
    store_heads(y1[:, :MOBA_DIM], y1[:, MOBA_DIM:], qb_ref, kb_ref, gains_ref[0:1, :],
                lambda h: 0.0, lambda h: onehot)

    c2 = c * LOG2E
    for t in range(TM // TQ):
        part = c2[t * TQ:(t + 1) * TQ, :]
        cmax_ref[pl.ds(r * (TM // TQ) + t, 1), :] = jnp.max(part, axis=0, keepdims=True)
        cmin_ref[pl.ds(r * (TM // TQ) + t, 1), :] = jnp.min(part, axis=0, keepdims=True)
    packed = pack_terms(c2)
    upper_q = _dot(packed, cq_ref[...]) + onesq_ref[...]
    upper_k = _dot(packed, ck_ref[...]) + onesk_ref[...]
    y3 = seg(col_kv, IN_MAIN)
    store_heads(y2[:, MOBA_DIM:], y3[:, :FOX_DIM], qc_ref, kc_ref, gains_ref[1:2, :],
                lambda h: upper_q[:, _group(h)], lambda h: upper_k[:, _group(h)])
    store_vT(vbT_ref, y2[:, :MOBA_DIM])
    store_vT(vcT_ref, y3[:, FOX_DIM:])


def _in_proj(x2, g1, w, layer, bf, gains, hm, tri, placement, batch, seq):
    n = x2.shape[0]
    nt = seq // TM
    const = lambda shape: pl.BlockSpec(shape, lambda r: (0,) * len(shape), pipeline_mode=pl.Buffered(1))
    rows = lambda width: pl.BlockSpec((TM, width), lambda r: (r, 0))
    tposed = lambda height: pl.BlockSpec((1, height, TM), lambda r: (r // nt, 0, r % nt))
    out_shape = (
        jax.ShapeDtypeStruct((n, CONV_CH), F32),
        jax.ShapeDtypeStruct((n, AUG), BF16),
        jax.ShapeDtypeStruct((n, AUG), BF16),
        jax.ShapeDtypeStruct((batch, HEADS * V_ROWS, seq), BF16),
        jax.ShapeDtypeStruct((n, AUG), BF16),
        jax.ShapeDtypeStruct((n, AUG), BF16),
        jax.ShapeDtypeStruct((batch, HEADS * V_ROWS, seq), BF16),
        jax.ShapeDtypeStruct((n // TQ, LANES), F32),
        jax.ShapeDtypeStruct((n // TQ, LANES), F32),
    )
    whole = pl.BlockSpec((n // TQ, LANES), lambda r: (0, 0))
    return pl.pallas_call(
        functools.partial(_in_proj_kernel, nt),
        grid=(n // TM,),
        in_specs=[rows(D_MODEL), const((1, D_MODEL)),
                  pl.BlockSpec((1, D_MODEL, IN_PAD), lambda r: (layer, 0, 0), pipeline_mode=pl.Buffered(1)),
                  const((1, LANES)),
                  const((2, AUG)), const((HM, HM)), const((TM, TM)),
                  const((LANES, AUG)), const((LANES, AUG)), const((1, AUG)), const((1, AUG))],
        out_specs=(rows(CONV_CH), rows(AUG), rows(AUG), tposed(HEADS * V_ROWS),
                   rows(AUG), rows(AUG), tposed(HEADS * V_ROWS), whole, whole),
        out_shape=out_shape,
        scratch_shapes=[pltpu.VMEM((TM, D_MODEL), BF16), pltpu.VMEM((1, LANES), F32)],
        compiler_params=pltpu.CompilerParams(dimension_semantics=("arbitrary",),
                                             vmem_limit_bytes=VMEM_LIMIT),
        name="in_proj",
    )(x2, g1, w, bf, gains, hm, tri, *placement)


def _softmax_steps(score_fn, m_scr, acc_scr, first):
    for g0 in range(0, HEADS, HEAD_GROUP):
        hs = range(g0, min(g0 + HEAD_GROUP, HEADS))
        tiles = [score_fn(h) for h in hs]
        tile_max = [jnp.max(sT, axis=0, keepdims=True) for sT, _ in tiles]
        if first:
            m_new = tile_max
        else:
            m_old = [m_scr[h:h + 1, :] for h in hs]
            m_new = [jnp.maximum(a, b) for a, b in zip(m_old, tile_max)]
        pvs = [_dot(vT, jnp.exp2(sT - mn).astype(BF16)) for (sT, vT), mn in zip(tiles, m_new)]
        for n, h in enumerate(hs):
            if first:
                acc_scr[h] = pvs[n]
            else:
                acc_scr[h] = jnp.exp2(m_old[n] - m_new[n]) * acc_scr[h] + pvs[n]
            m_scr[h:h + 1, :] = m_new[n]


def _softmax_steps_lazy(score_fn, m_scr, acc_scr, pv_scr, seed_rows=None):
    first = seed_rows is not None
    worst = None
    tiles = [score_fn(h) for h in range(HEADS)]
    for h in range(HEADS):
        sT, vT = tiles[h]
        if first:
            m_old = jnp.max(sT[seed_rows[0]:seed_rows[1], :], axis=0, keepdims=True)
            m_scr[h:h + 1, :] = m_old
        else:
            m_old = m_scr[h:h + 1, :]
        tile_max = jnp.max(sT, axis=0, keepdims=True)
        pv_scr[h] = _dot(vT, jnp.exp2(sT - m_old).astype(BF16))
        m_scr[SUBLANES + h:SUBLANES + h + 1, :] = tile_max
        rise = tile_max - m_old
        worst = rise if worst is None else jnp.maximum(worst, rise)
    safe = jnp.max(worst) <= LAZY_LIMIT

    @pl.when(safe)
    def _():
        for h in range(HEADS):
            m_old = m_scr[h:h + 1, :]
            m_new = jnp.maximum(m_old, m_scr[SUBLANES + h:SUBLANES + h + 1, :])
            acc = pv_scr[h] if first else acc_scr[h] + pv_scr[h]
            acc_scr[h] = jnp.exp2(m_old - m_new) * acc
            m_scr[h:h + 1, :] = m_new

    @pl.when(jnp.logical_not(safe))
    def _():
        _softmax_steps(score_fn, m_scr, acc_scr, first=first)


def _write_heads(o_ref, acc_scr):
    for pair in range(HEADS // 2):
        both = []
        for h in (2 * pair, 2 * pair + 1):
            acc = acc_scr[h]
            both.append(acc[:HEAD_DIM, :] / acc[HEAD_DIM:HEAD_DIM + 1, :])
        o_ref[:, pair * LANES:(pair + 1) * LANES] = jnp.concatenate(both, axis=0).T.astype(BF16)


def _key_tile(k_ref, vT_ref, h, k0, tk=TK):
    return k_ref[pl.ds(k0, tk), _group(h)], vT_ref[0, h * V_ROWS:(h + 1) * V_ROWS, pl.ds(k0, tk)]


def _past_tiles(first, count, step, merge):
    big = merge * TK

    def body(j, carry):
        step(pl.multiple_of(first * TK + j * big, TK), big)
        return carry

    lax.fori_loop(0, count // merge, body, 0)
    done = first + (count // merge) * merge
    count = first + count
    part = merge // 2
    while part >= 1:
        rem = count - done

        @pl.when(rem >= part)
        def _(done=done, part=part):
            step(pl.multiple_of(done * TK, TK), part * TK)

        done = done + jnp.where(rem >= part, part, 0)
        part //= 2


_ATTN_SCRATCH = [
    pltpu.VMEM((2 * SUBLANES, TQ), F32),
    pltpu.VMEM((HEADS, V_ROWS, TQ), F32),
    pltpu.VMEM((HEADS, V_ROWS, TQ), F32),
]


def _attention_specs(batch, seq):
    nq = seq // TQ
    in_specs = [pl.BlockSpec((TQ, AUG), lambda b, i: (b * nq + i, 0)),
                pl.BlockSpec((seq, AUG), lambda b, i: (b, 0)),
                pl.BlockSpec((1, HEADS * V_ROWS, seq), lambda b, i: (b, 0, 0))]
    out_spec = pl.BlockSpec((TQ, MOBA_DIM), lambda b, i: (b * nq + i, 0))
    params = pltpu.CompilerParams(dimension_semantics=("arbitrary", "arbitrary"), vmem_limit_bytes=VMEM_LIMIT)
    return (batch, nq), in_specs, out_spec, params


def _t5_bucket(dist):
    n = jnp.maximum(dist, 0)
    max_exact = NUM_BUCKETS // 2
    nf = jnp.maximum(n, 1).astype(F32)
    large = max_exact + (jnp.log(nf / max_exact) / math.log(MAX_DISTANCE / max_exact)
                         * (NUM_BUCKETS - max_exact)).astype(jnp.int32)
    large = jnp.minimum(large, NUM_BUCKETS - 1)
    return jnp.where(n < max_exact, n, large)


def _bias_kernel(bucket_ref, rb_ref, o_ref):
    h = pl.program_id(0)
    bucket = bucket_ref[...]
    acc = jnp.zeros(bucket.shape, F32)
    for b in range(NUM_BUCKETS):
        acc = jnp.where(bucket == b, rb_ref[b, h], acc)
    key = lax.broadcasted_iota(jnp.int32, bucket.shape, 0)
    qry = lax.broadcasted_iota(jnp.int32, bucket.shape, 1)
    o_ref[0] = jnp.where((key >= MOBA_BLOCK) | (key <= qry), acc * LOG2E, NEG)


def _moba_bias(rel_bias):
    key = lax.broadcasted_iota(jnp.int32, (2 * MOBA_BLOCK, TQ), 0)
    qry = lax.broadcasted_iota(jnp.int32, (2 * MOBA_BLOCK, TQ), 1)
    bucket = _t5_bucket(jnp.where(key < MOBA_BLOCK, qry - key, qry + 2 * MOBA_BLOCK - key))
    return pl.pallas_call(
        _bias_kernel,
        grid=(MOBA_HEADS,),
        in_specs=[pl.BlockSpec((2 * MOBA_BLOCK, TQ), lambda h: (0, 0)),
                  pl.BlockSpec(memory_space=pltpu.SMEM)],
        out_specs=pl.BlockSpec((1, 2 * MOBA_BLOCK, TQ), lambda h: (h, 0, 0)),
        out_shape=jax.ShapeDtypeStruct((MOBA_HEADS, 2 * MOBA_BLOCK, TQ), F32),
        compiler_params=pltpu.CompilerParams(dimension_semantics=("arbitrary",)),
        name="moba_bias",
    )(bucket, rel_bias)


def _moba_kernel(nblk, q_ref, k_ref, vT_ref, bias_ref, rb_ref, o_ref, m_scr, acc_scr, pv_scr, qm_scr, kmean_scr):
    i = pl.program_id(1)

    @pl.when(i == 0)
    def _():
        kmean_scr[...] = jnp.zeros_like(kmean_scr)
        for n in range(nblk):
            blk = k_ref[n * MOBA_BLOCK:(n + 1) * MOBA_BLOCK, :].astype(F32)
            kmean_scr[n:n + 1, :] = jnp.mean(blk, axis=0, keepdims=True)

    q0 = pl.multiple_of(i * TQ, TQ)
    blk_row = lax.broadcasted_iota(jnp.int32, (SEL_LANES, TQ), 0)

    for h in range(HEADS):
        qg = q_ref[:, _group(h)]
        terms = _dot_nt(jnp.concatenate(_split3(kmean_scr[:, _group(h)]), axis=0), qg)
        gate = terms[:SEL_LANES] + terms[SEL_LANES:2 * SEL_LANES] + terms[2 * SEL_LANES:]
        gate = jnp.where(blk_row < i, gate, -jnp.inf)
        picked = jnp.zeros((SEL_LANES, TQ), F32)
        for _ in range(MOBA_TOPK):
            best = jnp.max(gate, axis=0, keepdims=True)
            first = jnp.min(jnp.where(gate == best, blk_row, SEL_LANES), axis=0, keepdims=True)
            pick = blk_row == first
            picked = jnp.where(pick, 1.0, picked)
            gate = jnp.where(pick, -jnp.inf, gate)
        chosen = (picked > 0.0) & (blk_row < i)
        far = jnp.where(blk_row == i - 1, 0.0, rb_ref[NUM_BUCKETS - 1, h] * LOG2E)
        term = jnp.where(chosen, far, jnp.where(blk_row < i, NEG, 0.0))
        t_hi, t_mid, t_lo = _split3(term)
        lanes = jnp.concatenate(
            [jnp.zeros((HEAD_DIM, TQ), F32), t_hi.astype(F32), t_mid.astype(F32), t_lo.astype(F32),
             jnp.zeros((LANES - HEAD_DIM - 3 * SEL_LANES, TQ), F32)], axis=0)
        qm_scr[h] = (qg.astype(F32) + lanes.T).astype(BF16)

    @pl.when(i == 0)
    def _():
        def own_scores(h):
            kj, vTj = _key_tile(k_ref, vT_ref, h, q0)
            return _dot_nt(kj, qm_scr[h]) + bias_ref[h, :MOBA_BLOCK, :], vTj

        _softmax_steps_lazy(own_scores, m_scr, acc_scr, pv_scr, seed_rows=(0, BF16_ROWS))

    @pl.when(i >= 1)
    def _():
        k0 = pl.multiple_of((i - 1) * TK, TK)

        def near_scores(h):
            k_own, vT_own = _key_tile(k_ref, vT_ref, h, q0)
            k_prev, vT_prev = _key_tile(k_ref, vT_ref, h, k0)
            sT = _dot_nt(jnp.concatenate([k_own, k_prev], axis=0), qm_scr[h]) + bias_ref[h]
            return sT, jnp.concatenate([vT_own, vT_prev], axis=1)

        _softmax_steps_lazy(near_scores, m_scr, acc_scr, pv_scr, seed_rows=(0, BF16_ROWS))

    def far_step(k0, tk):
        def far_scores(h):
            kj, vTj = _key_tile(k_ref, vT_ref, h, k0, tk)
            return _dot_nt(kj, qm_scr[h]), vTj

        _softmax_steps_lazy(far_scores, m_scr, acc_scr, pv_scr)

    _past_tiles(0, jnp.maximum(i - 1, 0), far_step, MOBA_MERGE)
    _write_heads(o_ref, acc_scr)


def _moba(q, k, vT, bias, rel_bias, batch, seq):
    nblk = seq // MOBA_BLOCK
    grid, in_specs, out_spec, params = _attention_specs(batch, seq)
    return pl.pallas_call(
        functools.partial(_moba_kernel, nblk),
        grid=grid,
        in_specs=in_specs + [pl.BlockSpec((MOBA_HEADS, 2 * MOBA_BLOCK, TQ), lambda b, i: (0, 0, 0)),
                             pl.BlockSpec(memory_space=pltpu.SMEM)],
        out_specs=out_spec,
        out_shape=jax.ShapeDtypeStruct((q.shape[0], MOBA_DIM), BF16),
        scratch_shapes=_ATTN_SCRATCH + [pltpu.VMEM((HEADS, TQ, LANES), BF16),
                                        pltpu.VMEM((SEL_LANES, AUG), F32)],
        compiler_params=params,
        name="moba_attention",
    )(q, k, vT, bias, rel_bias)


def _fox_kernel(q_ref, k_ref, vT_ref, cmax_ref, cmin_ref, bound_ref, o_ref, m_scr, acc_scr, pv_scr):
    i = pl.program_id(1)
    q0 = pl.multiple_of(i * TQ, TQ)

    ntiles = cmin_ref.shape[0]
    gap = cmax_ref[pl.ds(i, 1), :] - cmin_ref[...]
    limit = -(2.0 * bound_ref[0] + (UNDERFLOW_LOG2 + ROUNDING_MARGIN))
    head_lane = lax.broadcasted_iota(jnp.int32, (ntiles, LANES), 1) < FOX_HEADS
    live = jnp.max(jnp.where(head_lane & (gap > limit), 1, 0), axis=1, keepdims=True)
    tile = lax.broadcasted_iota(jnp.int32, (ntiles, 1), 0)
    first_live = jnp.min(jnp.where((live > 0) & (tile < i), tile, i))

    def scores(h, k0, tk=TK):
        kj, vTj = _key_tile(k_ref, vT_ref, h, k0, tk)
        return _dot_nt(kj, q_ref[:, _group(h)]), vTj

    key = lax.broadcasted_iota(jnp.int32, (TK, TQ), 0)
    qry = lax.broadcasted_iota(jnp.int32, (TK, TQ), 1)

    def diag_scores(h):
        sT, vTj = scores(h, q0)
        return jnp.where(key <= qry, sT, NEG), vTj

    _softmax_steps(diag_scores, m_scr, acc_scr, first=True)

    def past_step(k0, tk):
        _softmax_steps_lazy(lambda h: scores(h, k0, tk), m_scr, acc_scr, pv_scr)

    _past_tiles(first_live, i - first_live, past_step, FOX_MERGE)
    _write_heads(o_ref, acc_scr)


def _fox(q, k, vT, cmax, cmin, bound, batch, seq):
    grid, in_specs, out_spec, params = _attention_specs(batch, seq)
    ntiles = seq // TQ
    return pl.pallas_call(
        _fox_kernel,
        grid=grid,
        in_specs=in_specs + [pl.BlockSpec((ntiles, LANES), lambda b, i: (b, 0)),
                             pl.BlockSpec((ntiles, LANES), lambda b, i: (b, 0)),
                             pl.BlockSpec(memory_space=pltpu.SMEM)],
        out_specs=out_spec,
        out_shape=jax.ShapeDtypeStruct((q.shape[0], FOX_DIM), BF16),
        scratch_shapes=_ATTN_SCRATCH,
        compiler_params=params,
        name="fox_attention",
    )(q, k, vT, cmax, cmin, bound)


def _conv_rows(w_ref, b_ref, lg_ref, lb_ref, sbuf, r0):
    shift = CONV_HALO - (CONV_WIDTH - 1)
    acc = jnp.zeros((CONV_SUB, CONV_CH), F32) + b_ref[...]
    for j in range(CONV_WIDTH):
        rho, base = (j + shift) % SUBLANES, (j + shift) // SUBLANES * SUBLANES
        acc = acc + w_ref[j:j + 1, :] * sbuf[rho, r0 + base:r0 + base + CONV_SUB, :]
    mu = jnp.mean(acc, axis=-1, keepdims=True)
    d = acc - mu
    var = jnp.mean(d * d, axis=-1, keepdims=True)
    y = d * lax.rsqrt(var + LN_EPS) * lg_ref[...] + lb_ref[...]
    return (y * _sigmoid(y)).astype(BF16)


def _conv_kernel(nt, glu_ref, w_ref, b_ref, lg_ref, lb_ref, o_ref, hbuf, sbuf):
    t = pl.program_id(0) % nt

    @pl.when(t == 0)
    def _():
        hbuf[0:CONV_HALO, :] = jnp.zeros((CONV_HALO, CONV_CH), F32)

    @pl.when(t > 0)
    def _():
        hbuf[0:CONV_HALO, :] = hbuf[CONV_TC:CONV_TC + CONV_HALO, :]

    hbuf[CONV_HALO:, :] = glu_ref[...]
    for rho in range(SUBLANES):
        rows = CONV_TC + CONV_HALO - (SUBLANES if rho else 0)
        sbuf[rho, 0:rows, :] = hbuf[rho:rho + rows, :]
    for r0 in range(0, CONV_TC, CONV_SUB):
        o_ref[r0:r0 + CONV_SUB, :] = _conv_rows(w_ref, b_ref, lg_ref, lb_ref, sbuf, r0)


def _conv(glu, w, b, lg, lb, seq):
    n = glu.shape[0]
    const = lambda shape: pl.BlockSpec(shape, lambda r: (0,) * len(shape))
    return pl.pallas_call(
        functools.partial(_conv_kernel, seq // CONV_TC),
        grid=(n // CONV_TC,),
        in_specs=[pl.BlockSpec((CONV_TC, CONV_CH), lambda r: (r, 0)),
                  const((CONV_WIDTH, CONV_CH)), const((1, CONV_CH)), const((1, CONV_CH)), const((1, CONV_CH))],
        out_specs=pl.BlockSpec((CONV_TC, CONV_CH), lambda r: (r, 0)),
        out_shape=jax.ShapeDtypeStruct((n, CONV_CH), BF16),
        scratch_shapes=[pltpu.VMEM((CONV_HALO + CONV_TC, CONV_CH), F32),
                        pltpu.VMEM((SUBLANES, CONV_HALO + CONV_TC, CONV_CH), F32)],
        compiler_params=pltpu.CompilerParams(dimension_semantics=("arbitrary",)),
        name="conv_module",
    )(glu, w, b, lg, lb)


def _out_ffn_kernel(x_ref, ya_ref, yb_ref, yc_ref, wo_ref, g2_ref, wgu_ref, wd_ref, o_ref,
                    mix_scr, h2_scr, acc_scr):
    mix_scr[:, 0:CONV_CH] = ya_ref[...]
    mix_scr[:, CONV_CH:CONV_CH + MOBA_DIM] = yb_ref[...]
    mix_scr[:, CONV_CH + MOBA_DIM:] = yc_ref[...]
    x1 = x_ref[...] + _dot(mix_scr[...], wo_ref[0])
    ms = jnp.mean(x1 * x1, axis=-1, keepdims=True)
    h2_scr[...] = (x1 * lax.rsqrt(ms + RMS_EPS) * g2_ref[...]).astype(BF16)
    acc_scr[...] = x1
    for c0, c1 in FF_CHUNKS:
        g = _dot(h2_scr[...], wgu_ref[0, :, c0:c1])
        u = _dot(h2_scr[...], wgu_ref[0, :, D_FF + c0:D_FF + c1])
        act = (g * _sigmoid(g) * u).astype(BF16)
        acc_scr[...] += _dot(act, wd_ref[0, c0:c1, :])
    o_ref[...] = acc_scr[...]


def _out_ffn(x2, ya, yb, yc, wo, g2, wgu, wd, layer):
    n = x2.shape[0]
    const = lambda shape: pl.BlockSpec(shape, lambda r: (0,) * len(shape), pipeline_mode=pl.Buffered(1))
    of_layer = lambda shape: pl.BlockSpec((1,) + shape, lambda r: (layer, 0, 0), pipeline_mode=pl.Buffered(1))
    rows = lambda width: pl.BlockSpec((TM, width), lambda r: (r, 0))
    return pl.pallas_call(
        _out_ffn_kernel,
        grid=(n // TM,),
        in_specs=[rows(D_MODEL), rows(CONV_CH), rows(MOBA_DIM), rows(FOX_DIM),
                  of_layer((D_MODEL, D_MODEL)), const((1, D_MODEL)), of_layer((D_MODEL, 2 * D_FF)),
                  of_layer((D_FF, D_MODEL))],
        out_specs=rows(D_MODEL),
        out_shape=jax.ShapeDtypeStruct((n, D_MODEL), F32),
        scratch_shapes=[pltpu.VMEM((TM, D_MODEL), BF16), pltpu.VMEM((TM, D_MODEL), BF16),
                        pltpu.VMEM((TM, D_MODEL), F32)],
        compiler_params=pltpu.CompilerParams(dimension_semantics=("arbitrary",),
                                             vmem_limit_bytes=VMEM_LIMIT),
        name="out_ffn",
    )(x2, ya, yb, yc, wo, g2, wgu, wd)


def kernel(x, w_in, b_forget, conv_w, conv_b, conv_ln_g, conv_ln_b, moba_qn_g, moba_kn_g,
           fox_qn_g, fox_kn_g, rel_bias, w_out, norm1_g, norm2_g, w_gate_up, w_down):
    batch, seq, d_model = x.shape
    depth = w_in.shape[0]
    assert d_model == D_MODEL and seq % TM == 0 and seq % MOBA_BLOCK == 0 and TQ == MOBA_BLOCK == TK
    assert w_in.shape[2] == IN_MAIN + FOX_HEADS and MOBA_HEADS == FOX_HEADS <= SUBLANES
    assert seq // MOBA_BLOCK <= SEL_LANES and HEAD_DIM + 3 * SEL_LANES <= LANES

    hd = jnp.arange(HM) // HEAD_DIM
    head_mean = jnp.where(hd[:, None] == hd[None, :], 1.0 / HEAD_DIM, 0.0).astype(BF16)
    tri = (jnp.arange(TM)[:, None] >= jnp.arange(TM)[None, :]).astype(BF16)
    placement = _placement_constants()

    bias = _moba_bias(rel_bias.astype(F32))
    x2 = x.reshape(batch * seq, D_MODEL)
    w_all = _in_proj_weight(w_in)
    wo_all, wgu_all, wd_all = w_out.astype(BF16), w_gate_up.astype(BF16), w_down.astype(BF16)
    for l in range(depth):
        bf = jnp.pad(b_forget[l].astype(F32), (0, LANES - FOX_HEADS)).reshape(1, LANES)
        gains = jnp.stack([jnp.concatenate([jnp.tile(gq[l].astype(F32), HEADS) * QSCALE,
                                            jnp.tile(gk[l].astype(F32), HEADS)])
                           for gq, gk in ((moba_qn_g, moba_kn_g), (fox_qn_g, fox_kn_g))])
        bound = (HEAD_DIM * QSCALE * NORM_SLACK * jnp.max(jnp.abs(fox_qn_g[l])) * jnp.max(jnp.abs(fox_kn_g[l]))
                 ).astype(F32).reshape(1)
        glu, qb, kb, vbT, qc, kc, vcT, cmax, cmin = _in_proj(
            x2, norm1_g[l].reshape(1, D_MODEL), w_all, l, bf, gains, head_mean, tri, placement, batch, seq)
        ya = _conv(glu, conv_w[l], conv_b[l].reshape(1, CONV_CH), conv_ln_g[l].reshape(1, CONV_CH),
                   conv_ln_b[l].reshape(1, CONV_CH), seq)
        yb = _moba(qb, kb, vbT, bias, rel_bias.astype(F32), batch, seq)
        yc = _fox(qc, kc, vcT, cmax, cmin, bound, batch, seq)
        x2 = _out_ffn(x2, ya, yb, yc, wo_all, norm2_g[l].reshape(1, D_MODEL), wgu_all, wd_all, l)
    return x2.reshape(batch, seq, D_MODEL)
```

```python
import functools
import math

import numpy as np
import jax
import jax.numpy as jnp
from jax import lax
from jax.experimental import pallas as pl
from jax.experimental.pallas import tpu as pltpu

F32 = jnp.float32
BF16 = jnp.bfloat16

D_MODEL = 1024
HEAD_DIM = 64
CONV_CH = D_MODEL // 4
MOBA_DIM = 3 * D_MODEL // 8
FOX_DIM = D_MODEL - CONV_CH - MOBA_DIM
MOBA_HEADS = MOBA_DIM // HEAD_DIM
FOX_HEADS = FOX_DIM // HEAD_DIM
HEADS = MOBA_HEADS
CONV_WIDTH = 31
LN_EPS = 1e-5
RMS_EPS = 1e-6
MOBA_BLOCK = 256
MOBA_TOPK = 3
NUM_BUCKETS = 32
MAX_DISTANCE = 128
D_FF = -(-8 * D_MODEL // (3 * 256)) * 256
NEG = -1e30
LOG2E = math.log2(math.e)
QSCALE = HEAD_DIM ** -0.5 * LOG2E

LANES = 128
SUBLANES = 8
BF16_ROWS = 16
IN_MAIN = 2 * CONV_CH + 3 * MOBA_DIM + 3 * FOX_DIM
IN_PAD = IN_MAIN + LANES
AUG = HEADS * LANES
HM = 256
SEL_LANES = 16
V_ROWS = HEAD_DIM + BF16_ROWS

TM = 512
TQ = 256
TK = 256
HEAD_GROUP = 6
MOBA_MERGE = 8
FOX_MERGE = 4
LAZY_LIMIT = 64.0
UNDERFLOW_LOG2 = 160.0
ROUNDING_MARGIN = 2.0
NORM_SLACK = 1.01
CONV_TC = 512
CONV_HALO = 32
CONV_SUB = 64
FF_CHUNKS = ((0, 768), (768, 1536), (1536, 2304), (2304, 2816))
VMEM_LIMIT = 56 * 1024 * 1024


def _split3(x):
    hi = x.astype(BF16)
    r1 = x - hi.astype(F32)
    mid = r1.astype(BF16)
    lo = (r1 - mid.astype(F32)).astype(BF16)
    return hi, mid, lo


def _dot(a, b):
    return jnp.dot(a, b, preferred_element_type=F32)


def _dot_nt(a, b):
    return lax.dot_general(a, b, (((1,), (1,)), ((), ())), preferred_element_type=F32)


def _sigmoid(x):
    return 1.0 / (1.0 + jnp.exp(-x))


def _group(h):
    return slice(h * LANES, (h + 1) * LANES)


def _placement_constants():
    cq = np.zeros((LANES, AUG), np.float32)
    ck = np.zeros((LANES, AUG), np.float32)
    ones_q = np.zeros((1, AUG), np.float32)
    ones_k = np.zeros((1, AUG), np.float32)
    for h in range(FOX_HEADS):
        for j in range(3):
            cq[j * SUBLANES + h, h * LANES + HEAD_DIM + j] = 1.0
            ones_k[0, h * LANES + HEAD_DIM + j] = 1.0
            ck[j * SUBLANES + h, h * LANES + HEAD_DIM + 3 + j] = -1.0
            ones_q[0, h * LANES + HEAD_DIM + 3 + j] = 1.0
    return jnp.asarray(cq, BF16), jnp.asarray(ck, BF16), jnp.asarray(ones_q), jnp.asarray(ones_k)


def _in_proj_weight(w):
    return jnp.pad(w.astype(BF16), ((0, 0), (0, 0), (0, IN_PAD - w.shape[2])))


def _in_proj_kernel(nt, x_ref, g1_ref, w_ref, bf_ref, gains_ref, hm_ref, tri_ref,
                    cq_ref, ck_ref, onesq_ref, onesk_ref,
                    glu_ref, qb_ref, kb_ref, vbT_ref, qc_ref, kc_ref, vcT_ref, cmax_ref, cmin_ref,
                    h_scr, carry_scr):
    r = pl.program_id(0)

    @pl.when(r % nt == 0)
    def _():
        carry_scr[...] = jnp.zeros_like(carry_scr)

    x = x_ref[...]
    ms = jnp.mean(x * x, axis=-1, keepdims=True)
    h_scr[...] = (x * lax.rsqrt(ms + RMS_EPS) * g1_ref[...]).astype(BF16)
    lane = lax.broadcasted_iota(jnp.int32, (TM, LANES), 1)
    low = lane < HEAD_DIM

    def seg(c0, c1):
        return _dot(h_scr[...], w_ref[0, :, c0:c1])

    def store_heads(yq, yk, q_ref, k_ref, gains, upper_q, upper_k):
        y = jnp.concatenate([yq, yk], axis=1)
        sq = y * y
        hi = sq.astype(BF16)
        lo = (sq - hi.astype(F32)).astype(BF16)
        hm = hm_ref[...]
        msq = jnp.concatenate([_dot(hi[:, c:c + HM], hm) + _dot(lo[:, c:c + HM], hm) for c in range(0, AUG, HM)],
                              axis=1)
        y = y * lax.rsqrt(msq + RMS_EPS) * gains
        for ref, upper, first in ((q_ref, upper_q, 0), (k_ref, upper_k, HEADS // 2)):
            for pair in range(HEADS // 2):
                both = y[:, _group(first + pair)]
                ref[:, _group(2 * pair)] = jnp.where(low, both, upper(2 * pair)).astype(BF16)
                ref[:, _group(2 * pair + 1)] = jnp.where(
                    low, pltpu.roll(both, HEAD_DIM, axis=1), upper(2 * pair + 1)).astype(BF16)

    def pack_terms(t):
        hi, mid, lo = _split3(jnp.where(lane < FOX_HEADS, t, 0.0))
        return (hi.astype(F32) + pltpu.roll(mid.astype(F32), SUBLANES, axis=1)
                + pltpu.roll(lo.astype(F32), 2 * SUBLANES, axis=1)).astype(BF16)

    def unpack_terms(p):
        return p + pltpu.roll(p, LANES - SUBLANES, axis=1) + pltpu.roll(p, LANES - 2 * SUBLANES, axis=1)

    def store_vT(ref, v):
        vT = v.T.astype(BF16)
        row = lax.broadcasted_iota(jnp.int32, (BF16_ROWS, TM), 0)
        tail = jnp.where(row == 0, 1.0, 0.0).astype(BF16)
        for h in range(HEADS):
            ref[0, h * V_ROWS:h * V_ROWS + HEAD_DIM, :] = vT[h * HEAD_DIM:(h + 1) * HEAD_DIM, :]
            ref[0, h * V_ROWS + HEAD_DIM:(h + 1) * V_ROWS, :] = tail

    col_qk = 2 * CONV_CH
    col_vq, col_kv = col_qk + 2 * MOBA_DIM, col_qk + 3 * MOBA_DIM + FOX_DIM
    z = seg(IN_MAIN, IN_PAD) + bf_ref[...]
    y2 = seg(col_vq, col_kv)

    logf = jnp.minimum(z, 0.0) - jnp.log1p(jnp.exp(-jnp.abs(z)))
    csum = unpack_terms(_dot(tri_ref[...], pack_terms(logf)))
    c = csum + carry_scr[...]
    carry_scr[...] = c[TM - 1:TM, :]
    y3 = seg(col_kv, IN_MAIN)

    c2 = c * LOG2E
    for t in range(TM // TQ):
        part = c2[t * TQ:(t + 1) * TQ, :]
        cmax_ref[pl.ds(r * (TM // TQ) + t, 1), :] = jnp.max(part, axis=0, keepdims=True)
        cmin_ref[pl.ds(r * (TM // TQ) + t, 1), :] = jnp.min(part, axis=0, keepdims=True)
    packed = pack_terms(c2)
    upper_q = _dot(packed, cq_ref[...]) + onesq_ref[...]
    upper_k = _dot(packed, ck_ref[...]) + onesk_ref[...]
    y1 = seg(col_qk, col_vq)
    store_heads(y2[:, MOBA_DIM:], y3[:, :FOX_DIM], qc_ref, kc_ref, gains_ref[1:2, :],
                lambda h: upper_q[:, _group(h)], lambda h: upper_k[:, _group(h)])
    store_vT(vbT_ref, y2[:, :MOBA_DIM])
    store_vT(vcT_ref, y3[:, FOX_DIM:])
    a = seg(0, col_qk)

    pos = (r % nt) * TM + lax.broadcasted_iota(jnp.int32, (TM, LANES), 0)
    in_sel = (lane >= HEAD_DIM) & (lane < HEAD_DIM + 3 * SEL_LANES)
    onehot = jnp.where(in_sel & ((lane & (SEL_LANES - 1)) == pos // MOBA_BLOCK), 1.0, 0.0)
    store_heads(y1[:, :MOBA_DIM], y1[:, MOBA_DIM:], qb_ref, kb_ref, gains_ref[0:1, :],
                lambda h: 0.0, lambda h: onehot)
    glu_ref[...] = a[:, :CONV_CH] * _sigmoid(a[:, CONV_CH:])


def _in_proj(x2, g1, w, layer, bf, gains, hm, tri, placement, batch, seq):
    n = x2.shape[0]
    nt = seq // TM
    const = lambda shape: pl.BlockSpec(shape, lambda r: (0,) * len(shape), pipeline_mode=pl.Buffered(1))
    rows = lambda width: pl.BlockSpec((TM, width), lambda r: (r, 0))
    tposed = lambda height: pl.BlockSpec((1, height, TM), lambda r: (r // nt, 0, r % nt))
    out_shape = (
        jax.ShapeDtypeStruct((n, CONV_CH), F32),
        jax.ShapeDtypeStruct((n, AUG), BF16),
        jax.ShapeDtypeStruct((n, AUG), BF16),
        jax.ShapeDtypeStruct((batch, HEADS * V_ROWS, seq), BF16),
        jax.ShapeDtypeStruct((n, AUG), BF16),
        jax.ShapeDtypeStruct((n, AUG), BF16),
        jax.ShapeDtypeStruct((batch, HEADS * V_ROWS, seq), BF16),
        jax.ShapeDtypeStruct((n // TQ, LANES), F32),
        jax.ShapeDtypeStruct((n // TQ, LANES), F32),
    )
    whole = pl.BlockSpec((n // TQ, LANES), lambda r: (0, 0))
    return pl.pallas_call(
        functools.partial(_in_proj_kernel, nt),
        grid=(n // TM,),
        in_specs=[rows(D_MODEL), const((1, D_MODEL)),
                  pl.BlockSpec((1, D_MODEL, IN_PAD), lambda r: (layer, 0, 0), pipeline_mode=pl.Buffered(1)),
                  const((1, LANES)),
                  const((2, AUG)), const((HM, HM)), const((TM, TM)),
                  const((LANES, AUG)), const((LANES, AUG)), const((1, AUG)), const((1, AUG))],
        out_specs=(rows(CONV_CH), rows(AUG), rows(AUG), tposed(HEADS * V_ROWS),
                   rows(AUG), rows(AUG), tposed(HEADS * V_ROWS), whole, whole),
        out_shape=out_shape,
        scratch_shapes=[pltpu.VMEM((TM, D_MODEL), BF16), pltpu.VMEM((1, LANES), F32)],
        compiler_params=pltpu.CompilerParams(dimension_semantics=("arbitrary",),
                                             vmem_limit_bytes=VMEM_LIMIT),
        name="in_proj",
    )(x2, g1, w, bf, gains, hm, tri, *placement)


def _softmax_steps(score_fn, m_scr, acc_scr, first):
    for g0 in range(0, HEADS, HEAD_GROUP):
        hs = range(g0, min(g0 + HEAD_GROUP, HEADS))
        tiles = [score_fn(h) for h in hs]
        tile_max = [jnp.max(sT, axis=0, keepdims=True) for sT, _ in tiles]
        if first:
            m_new = tile_max
        else:
            m_old = [m_scr[h:h + 1, :] for h in hs]
            m_new = [jnp.maximum(a, b) for a, b in zip(m_old, tile_max)]
        pvs = [_dot(vT, jnp.exp2(sT - mn).astype(BF16)) for (sT, vT), mn in zip(tiles, m_new)]
        for n, h in enumerate(hs):
            if first:
                acc_scr[h] = pvs[n]
            else:
                acc_scr[h] = jnp.exp2(m_old[n] - m_new[n]) * acc_scr[h] + pvs[n]
            m_scr[h:h + 1, :] = m_new[n]


def _softmax_steps_lazy(score_fn, m_scr, acc_scr, pv_scr, seed_rows=None):
    first = seed_rows is not None
    worst = None
    tiles = [score_fn(h) for h in range(HEADS)]
    for h in range(HEADS):
        sT, vT = tiles[h]
        if first:
            m_old = jnp.max(sT[seed_rows[0]:seed_rows[1], :], axis=0, keepdims=True)
            m_scr[h:h + 1, :] = m_old
        else:
            m_old = m_scr[h:h + 1, :]
        tile_max = jnp.max(sT, axis=0, keepdims=True)
        pv_scr[h] = _dot(vT, jnp.exp2(sT - m_old).astype(BF16))
        m_scr[SUBLANES + h:SUBLANES + h + 1, :] = tile_max
        rise = tile_max - m_old
        worst = rise if worst is None else jnp.maximum(worst, rise)
    safe = jnp.max(worst) <= LAZY_LIMIT

    @pl.when(safe)
    def _():
        for h in range(HEADS):
            m_old = m_scr[h:h + 1, :]
            m_new = jnp.maximum(m_old, m_scr[SUBLANES + h:SUBLANES + h + 1, :])
            acc = pv_scr[h] if first else acc_scr[h] + pv_scr[h]
            acc_scr[h] = jnp.exp2(m_old - m_new) * acc
            m_scr[h:h + 1, :] = m_new

    @pl.when(jnp.logical_not(safe))
    def _():
        _softmax_steps(score_fn, m_scr, acc_scr, first=first)


def _write_heads(o_ref, acc_scr):
    for pair in range(HEADS // 2):
        both = []
        for h in (2 * pair, 2 * pair + 1):
            acc = acc_scr[h]
            both.append(acc[:HEAD_DIM, :] / acc[HEAD_DIM:HEAD_DIM + 1, :])
        o_ref[:, pair * LANES:(pair + 1) * LANES] = jnp.concatenate(both, axis=0).T.astype(BF16)


def _key_tile(k_ref, vT_ref, h, k0, tk=TK):
    return k_ref[pl.ds(k0, tk), _group(h)], vT_ref[0, h * V_ROWS:(h + 1) * V_ROWS, pl.ds(k0, tk)]


def _past_tiles(first, count, step, merge):
    big = merge * TK

    def body(j, carry):
        step(pl.multiple_of(first * TK + j * big, TK), big)
        return carry

    lax.fori_loop(0, count // merge, body, 0)
    done = first + (count // merge) * merge
    count = first + count
    part = merge // 2
    while part >= 1:
        rem = count - done

        @pl.when(rem >= part)
        def _(done=done, part=part):
            step(pl.multiple_of(done * TK, TK), part * TK)

        done = done + jnp.where(rem >= part, part, 0)
        part //= 2


_ATTN_SCRATCH = [
    pltpu.VMEM((2 * SUBLANES, TQ), F32),
    pltpu.VMEM((HEADS, V_ROWS, TQ), F32),
    pltpu.VMEM((HEADS, V_ROWS, TQ), F32),
]


def _attention_specs(batch, seq):
    nq = seq // TQ
    in_specs = [pl.BlockSpec((TQ, AUG), lambda b, i: (b * nq + i, 0)),
                pl.BlockSpec((seq, AUG), lambda b, i: (b, 0)),
                pl.BlockSpec((1, HEADS * V_ROWS, seq), lambda b, i: (b, 0, 0))]
    out_spec = pl.BlockSpec((TQ, MOBA_DIM), lambda b, i: (b * nq + i, 0))
    params = pltpu.CompilerParams(dimension_semantics=("arbitrary", "arbitrary"), vmem_limit_bytes=VMEM_LIMIT)
    return (batch, nq), in_specs, out_spec, params


def _t5_bucket(dist):
    n = jnp.maximum(dist, 0)
    max_exact = NUM_BUCKETS // 2
    nf = jnp.maximum(n, 1).astype(F32)
    large = max_exact + (jnp.log(nf / max_exact) / math.log(MAX_DISTANCE / max_exact)
                         * (NUM_BUCKETS - max_exact)).astype(jnp.int32)
    large = jnp.minimum(large, NUM_BUCKETS - 1)
    return jnp.where(n < max_exact, n, large)


def _bias_kernel(bucket_ref, rb_ref, o_ref):
    h = pl.program_id(0)
    bucket = bucket_ref[...]
    acc = jnp.zeros(bucket.shape, F32)
    for b in range(NUM_BUCKETS):
        acc = jnp.where(bucket == b, rb_ref[b, h], acc)
    key = lax.broadcasted_iota(jnp.int32, bucket.shape, 0)
    qry = lax.broadcasted_iota(jnp.int32, bucket.shape, 1)
    o_ref[0] = jnp.where((key >= MOBA_BLOCK) | (key <= qry), acc * LOG2E, NEG)


def _moba_bias(rel_bias):
    key = lax.broadcasted_iota(jnp.int32, (2 * MOBA_BLOCK, TQ), 0)
    qry = lax.broadcasted_iota(jnp.int32, (2 * MOBA_BLOCK, TQ), 1)
    bucket = _t5_bucket(jnp.where(key < MOBA_BLOCK, qry - key, qry + 2 * MOBA_BLOCK - key))
    return pl.pallas_call(
        _bias_kernel,
        grid=(MOBA_HEADS,),
        in_specs=[pl.BlockSpec((2 * MOBA_BLOCK, TQ), lambda h: (0, 0)),
                  pl.BlockSpec(memory_space=pltpu.SMEM)],
        out_specs=pl.BlockSpec((1, 2 * MOBA_BLOCK, TQ), lambda h: (h, 0, 0)),
        out_shape=jax.ShapeDtypeStruct((MOBA_HEADS, 2 * MOBA_BLOCK, TQ), F32),
        compiler_params=pltpu.CompilerParams(dimension_semantics=("arbitrary",)),
        name="moba_bias",
    )(bucket, rel_bias)


def _moba_kernel(nblk, q_ref, k_ref, vT_ref, bias_ref, rb_ref, o_ref, m_scr, acc_scr, pv_scr, qm_scr, kmean_scr):
    i = pl.program_id(1)

    @pl.when(i == 0)
    def _():
        kmean_scr[...] = jnp.zeros_like(kmean_scr)
        for n in range(nblk):
            blk = k_ref[n * MOBA_BLOCK:(n + 1) * MOBA_BLOCK, :].astype(F32)
            kmean_scr[n:n + 1, :] = jnp.mean(blk, axis=0, keepdims=True)

    q0 = pl.multiple_of(i * TQ, TQ)
    blk_row = lax.broadcasted_iota(jnp.int32, (SEL_LANES, TQ), 0)

    for h in range(HEADS):
        qg = q_ref[:, _group(h)]
        terms = _dot_nt(jnp.concatenate(_split3(kmean_scr[:, _group(h)]), axis=0), qg)
        gate = terms[:SEL_LANES] + terms[SEL_LANES:2 * SEL_LANES] + terms[2 * SEL_LANES:]
        gate = jnp.where(blk_row < i, gate, -jnp.inf)
        picked = jnp.zeros((SEL_LANES, TQ), F32)
        for _ in range(MOBA_TOPK):
            best = jnp.max(gate, axis=0, keepdims=True)
            first = jnp.min(jnp.where(gate == best, blk_row, SEL_LANES), axis=0, keepdims=True)
            pick = blk_row == first
            picked = jnp.where(pick, 1.0, picked)
            gate = jnp.where(pick, -jnp.inf, gate)
        chosen = (picked > 0.0) & (blk_row < i)
        far = jnp.where(blk_row == i - 1, 0.0, rb_ref[NUM_BUCKETS - 1, h] * LOG2E)
        term = jnp.where(chosen, far, jnp.where(blk_row < i, NEG, 0.0))
        t_hi, t_mid, t_lo = _split3(term)
        lanes = jnp.concatenate(
            [jnp.zeros((HEAD_DIM, TQ), F32), t_hi.astype(F32), t_mid.astype(F32), t_lo.astype(F32),
             jnp.zeros((LANES - HEAD_DIM - 3 * SEL_LANES, TQ), F32)], axis=0)
        qm_scr[h] = (qg.astype(F32) + lanes.T).astype(BF16)

    @pl.when(i == 0)
    def _():
        def own_scores(h):
            kj, vTj = _key_tile(k_ref, vT_ref, h, q0)
            return _dot_nt(kj, qm_scr[h]) + bias_ref[h, :MOBA_BLOCK, :], vTj

        _softmax_steps_lazy(own_scores, m_scr, acc_scr, pv_scr, seed_rows=(0, BF16_ROWS))

    @pl.when(i >= 1)
    def _():
        k0 = pl.multiple_of((i - 1) * TK, TK)

        def near_scores(h):
            k_own, vT_own = _key_tile(k_ref, vT_ref, h, q0)
            k_prev, vT_prev = _key_tile(k_ref, vT_ref, h, k0)
            sT = _dot_nt(jnp.concatenate([k_own, k_prev], axis=0), qm_scr[h]) + bias_ref[h]
            return sT, jnp.concatenate([vT_own, vT_prev], axis=1)

        _softmax_steps_lazy(near_scores, m_scr, acc_scr, pv_scr, seed_rows=(0, BF16_ROWS))

    def far_step(k0, tk):
        def far_scores(h):
            kj, vTj = _key_tile(k_ref, vT_ref, h, k0, tk)
            return _dot_nt(kj, qm_scr[h]), vTj

        _softmax_steps_lazy(far_scores, m_scr, acc_scr, pv_scr)

    _past_tiles(0, jnp.maximum(i - 1, 0), far_step, MOBA_MERGE)
    _write_heads(o_ref, acc_scr)


def _moba(q, k, vT, bias, rel_bias, batch, seq):
    nblk = seq // MOBA_BLOCK
    grid, in_specs, out_spec, params = _attention_specs(batch, seq)
    return pl.pallas_call(
        functools.partial(_moba_kernel, nblk),
        grid=grid,
        in_specs=in_specs + [pl.BlockSpec((MOBA_HEADS, 2 * MOBA_BLOCK, TQ), lambda b, i: (0, 0, 0)),
                             pl.BlockSpec(memory_space=pltpu.SMEM)],
        out_specs=out_spec,
        out_shape=jax.ShapeDtypeStruct((q.shape[0], MOBA_DIM), BF16),
        scratch_shapes=_ATTN_SCRATCH + [pltpu.VMEM((HEADS, TQ, LANES), BF16),
                                        pltpu.VMEM((SEL_LANES, AUG), F32)],
        compiler_params=params,
        name="moba_attention",
    )(q, k, vT, bias, rel_bias)


def _fox_kernel(q_ref, k_ref, vT_ref, cmax_ref, cmin_ref, bound_ref, o_ref, m_scr, acc_scr, pv_scr):
    i = pl.program_id(1)
    q0 = pl.multiple_of(i * TQ, TQ)

    ntiles = cmin_ref.shape[0]
    gap = cmax_ref[pl.ds(i, 1), :] - cmin_ref[...]
    limit = -(2.0 * bound_ref[0] + (UNDERFLOW_LOG2 + ROUNDING_MARGIN))
    head_lane = lax.broadcasted_iota(jnp.int32, (ntiles, LANES), 1) < FOX_HEADS
    live = jnp.max(jnp.where(head_lane & (gap > limit), 1, 0), axis=1, keepdims=True)
    tile = lax.broadcasted_iota(jnp.int32, (ntiles, 1), 0)
    first_live = jnp.min(jnp.where((live > 0) & (tile < i), tile, i))

    def scores(h, k0, tk=TK):
        kj, vTj = _key_tile(k_ref, vT_ref, h, k0, tk)
        return _dot_nt(kj, q_ref[:, _group(h)]), vTj

    key = lax.broadcasted_iota(jnp.int32, (TK, TQ), 0)
    qry = lax.broadcasted_iota(jnp.int32, (TK, TQ), 1)

    def diag_scores(h):
        sT, vTj = scores(h, q0)
        return jnp.where(key <= qry, sT, NEG), vTj

    _softmax_steps(diag_scores, m_scr, acc_scr, first=True)

    def past_step(k0, tk):
        _softmax_steps_lazy(lambda h: scores(h, k0, tk), m_scr, acc_scr, pv_scr)

    _past_tiles(first_live, i - first_live, past_step, FOX_MERGE)
    _write_heads(o_ref, acc_scr)


def _fox(q, k, vT, cmax, cmin, bound, batch, seq):
    grid, in_specs, out_spec, params = _attention_specs(batch, seq)
    ntiles = seq // TQ
    return pl.pallas_call(
        _fox_kernel,
        grid=grid,
        in_specs=in_specs + [pl.BlockSpec((ntiles, LANES), lambda b, i: (b, 0)),
                             pl.BlockSpec((ntiles, LANES), lambda b, i: (b, 0)),
                             pl.BlockSpec(memory_space=pltpu.SMEM)],
        out_specs=out_spec,
        out_shape=jax.ShapeDtypeStruct((q.shape[0], FOX_DIM), BF16),
        scratch_shapes=_ATTN_SCRATCH,
        compiler_params=params,
        name="fox_attention",
    )(q, k, vT, cmax, cmin, bound)


def _conv_rows(w_ref, b_ref, lg_ref, lb_ref, sbuf, r0):
    shift = CONV_HALO - (CONV_WIDTH - 1)
    acc = jnp.zeros((CONV_SUB, CONV_CH), F32) + b_ref[...]
    for j in range(CONV_WIDTH):
        rho, base = (j + shift) % SUBLANES, (j + shift) // SUBLANES * SUBLANES
        acc = acc + w_ref[j:j + 1, :] * sbuf[rho, r0 + base:r0 + base + CONV_SUB, :]
    mu = jnp.mean(acc, axis=-1, keepdims=True)
    d = acc - mu
    var = jnp.mean(d * d, axis=-1, keepdims=True)
    y = d * lax.rsqrt(var + LN_EPS) * lg_ref[...] + lb_ref[...]
    return (y * _sigmoid(y)).astype(BF16)


def _conv_kernel(nt, glu_ref, w_ref, b_ref, lg_ref, lb_ref, o_ref, hbuf, sbuf):
    t = pl.program_id(0) % nt

    @pl.when(t == 0)
    def _():
        hbuf[0:CONV_HALO, :] = jnp.zeros((CONV_HALO, CONV_CH), F32)

    @pl.when(t > 0)
    def _():
        hbuf[0:CONV_HALO, :] = hbuf[CONV_TC:CONV_TC + CONV_HALO, :]

    hbuf[CONV_HALO:, :] = glu_ref[...]
    for rho in range(SUBLANES):
        rows = CONV_TC + CONV_HALO - (SUBLANES if rho else 0)
        sbuf[rho, 0:rows, :] = hbuf[rho:rho + rows, :]
    for r0 in range(0, CONV_TC, CONV_SUB):
        o_ref[r0:r0 + CONV_SUB, :] = _conv_rows(w_ref, b_ref, lg_ref, lb_ref, sbuf, r0)


def _conv(glu, w, b, lg, lb, seq):
    n = glu.shape[0]
    const = lambda shape: pl.BlockSpec(shape, lambda r: (0,) * len(shape))
    return pl.pallas_call(
        functools.partial(_conv_kernel, seq // CONV_TC),
        grid=(n // CONV_TC,),
        in_specs=[pl.BlockSpec((CONV_TC, CONV_CH), lambda r: (r, 0)),
                  const((CONV_WIDTH, CONV_CH)), const((1, CONV_CH)), const((1, CONV_CH)), const((1, CONV_CH))],
        out_specs=pl.BlockSpec((CONV_TC, CONV_CH), lambda r: (r, 0)),
        out_shape=jax.ShapeDtypeStruct((n, CONV_CH), BF16),
        scratch_shapes=[pltpu.VMEM((CONV_HALO + CONV_TC, CONV_CH), F32),
                        pltpu.VMEM((SUBLANES, CONV_HALO + CONV_TC, CONV_CH), F32)],
        compiler_params=pltpu.CompilerParams(dimension_semantics=("arbitrary",)),
        name="conv_module",
    )(glu, w, b, lg, lb)


def _out_ffn_kernel(x_ref, ya_ref, yb_ref, yc_ref, wo_ref, g2_ref, wgu_ref, wd_ref, o_ref,
                    mix_scr, h2_scr, acc_scr):
    mix_scr[:, 0:CONV_CH] = ya_ref[...]
    mix_scr[:, CONV_CH:CONV_CH + MOBA_DIM] = yb_ref[...]
    mix_scr[:, CONV_CH + MOBA_DIM:] = yc_ref[...]
    x1 = x_ref[...] + _dot(mix_scr[...], wo_ref[0])
    ms = jnp.mean(x1 * x1, axis=-1, keepdims=True)
    h2_scr[...] = (x1 * lax.rsqrt(ms + RMS_EPS) * g2_ref[...]).astype(BF16)
    acc_scr[...] = x1
    for c0, c1 in FF_CHUNKS:
        g = _dot(h2_scr[...], wgu_ref[0, :, c0:c1])
        u = _dot(h2_scr[...], wgu_ref[0, :, D_FF + c0:D_FF + c1])
        act = (g * _sigmoid(g) * u).astype(BF16)
        acc_scr[...] += _dot(act, wd_ref[0, c0:c1, :])
    o_ref[...] = acc_scr[...]


def _out_ffn(x2, ya, yb, yc, wo, g2, wgu, wd, layer):
    n = x2.shape[0]
    const = lambda shape: pl.BlockSpec(shape, lambda r: (0,) * len(shape), pipeline_mode=pl.Buffered(1))
    of_layer = lambda shape: pl.BlockSpec((1,) + shape, lambda r: (layer, 0, 0), pipeline_mode=pl.Buffered(1))
    rows = lambda width: pl.BlockSpec((TM, width), lambda r: (r, 0))
    return pl.pallas_call(
        _out_ffn_kernel,
        grid=(n // TM,),
        in_specs=[rows(D_MODEL), rows(CONV_CH), rows(MOBA_DIM), rows(FOX_DIM),
                  of_layer((D_MODEL, D_MODEL)), const((1, D_MODEL)), of_layer((D_MODEL, 2 * D_FF)),
                  of_layer((D_FF, D_MODEL))],
        out_specs=rows(D_MODEL),
        out_shape=jax.ShapeDtypeStruct((n, D_MODEL), F32),
        scratch_shapes=[pltpu.VMEM((TM, D_MODEL), BF16), pltpu.VMEM((TM, D_MODEL), BF16),
                        pltpu.VMEM((TM, D_MODEL), F32)],
        compiler_params=pltpu.CompilerParams(dimension_semantics=("arbitrary",),
                                             vmem_limit_bytes=VMEM_LIMIT),
        name="out_ffn",
    )(x2, ya, yb, yc, wo, g2, wgu, wd)


def kernel(x, w_in, b_forget, conv_w, conv_b, conv_ln_g, conv_ln_b, moba_qn_g, moba_kn_g,
           fox_qn_g, fox_kn_g, rel_bias, w_out, norm1_g, norm2_g, w_gate_up, w_down):
    batch, seq, d_model = x.shape
    depth = w_in.shape[0]
    assert d_model == D_MODEL and seq % TM == 0 and seq % MOBA_BLOCK == 0 and TQ == MOBA_BLOCK == TK
    assert w_in.shape[2] == IN_MAIN + FOX_HEADS and MOBA_HEADS == FOX_HEADS <= SUBLANES
    assert seq // MOBA_BLOCK <= SEL_LANES and HEAD_DIM + 3 * SEL_LANES <= LANES

    hd = jnp.arange(HM) // HEAD_DIM
    head_mean = jnp.where(hd[:, None] == hd[None, :], 1.0 / HEAD_DIM, 0.0).astype(BF16)
    tri = (jnp.arange(TM)[:, None] >= jnp.arange(TM)[None, :]).astype(BF16)
    placement = _placement_constants()

    bias = _moba_bias(rel_bias.astype(F32))
    x2 = x.reshape(batch * seq, D_MODEL)
    w_all = _in_proj_weight(w_in)
    wo_all, wgu_all, wd_all = w_out.astype(BF16), w_gate_up.astype(BF16), w_down.astype(BF16)
    for l in range(depth):
        bf = jnp.pad(b_forget[l].astype(F32), (0, LANES - FOX_HEADS)).reshape(1, LANES)
        gains = jnp.stack([jnp.concatenate([jnp.tile(gq[l].astype(F32), HEADS) * QSCALE,
                                            jnp.tile(gk[l].astype(F32), HEADS)])
                           for gq, gk in ((moba_qn_g, moba_kn_g), (fox_qn_g, fox_kn_g))])
        bound = (HEAD_DIM * QSCALE * NORM_SLACK * jnp.max(jnp.abs(fox_qn_g[l])) * jnp.max(jnp.abs(fox_kn_g[l]))
                 ).astype(F32).reshape(1)
        glu, qb, kb, vbT, qc, kc, vcT, cmax, cmin = _in_proj(
            x2, norm1_g[l].reshape(1, D_MODEL), w_all, l, bf, gains, head_mean, tri, placement, batch, seq)
        ya = _conv(glu, conv_w[l], conv_b[l].reshape(1, CONV_CH), conv_ln_g[l].reshape(1, CONV_CH),
                   conv_ln_b[l].reshape(1, CONV_CH), seq)
        yb = _moba(qb, kb, vbT, bias, rel_bias.astype(F32), batch, seq)
        yc = _fox(qc, kc, vcT, cmax, cmin, bound, batch, seq)
        x2 = _out_ffn(x2, ya, yb, yc, wo_all, norm2_g[l].reshape(1, D_MODEL), wgu_all, wd_all, l)
    return x2.reshape(batch, seq, D_MODEL)
```

```python
import functools
import math

import numpy as np
import jax
import jax.numpy as jnp
from jax import lax
from jax.experimental import pallas as pl
from jax.experimental.pallas import tpu as pltpu

F32 = jnp.float32
BF16 = jnp.bfloat16

D_MODEL = 1024
HEAD_DIM = 64
CONV_CH = D_MODEL // 4
MOBA_DIM = 3 * D_MODEL // 8
FOX_DIM = D_MODEL - CONV_CH - MOBA_DIM
MOBA_HEADS = MOBA_DIM // HEAD_DIM
FOX_HEADS = FOX_DIM // HEAD_DIM
HEADS = MOBA_HEADS
CONV_WIDTH = 31
LN_EPS = 1e-5
RMS_EPS = 1e-6
MOBA_BLOCK = 256
MOBA_TOPK = 3
NUM_BUCKETS = 32
MAX_DISTANCE = 128
D_FF = -(-8 * D_MODEL // (3 * 256)) * 256
NEG = -1e30
LOG2E = math.log2(math.e)
QSCALE = HEAD_DIM ** -0.5 * LOG2E

LANES = 128
SUBLANES = 8
BF16_ROWS = 16
IN_MAIN = 2 * CONV_CH + 3 * MOBA_DIM + 3 * FOX_DIM
IN_PAD = IN_MAIN + LANES
AUG = HEADS * LANES
HM = 256
SEL_LANES = 16
V_ROWS = HEAD_DIM + BF16_ROWS

TM = 512
TQ = 256
TK = 256
HEAD_GROUP = 6
MOBA_MERGE = 8
FOX_MERGE = 4
EXACT_MERGE = 2
LAZY_LIMIT = 64.0
UNDERFLOW_LOG2 = 160.0
ROUNDING_MARGIN = 2.0
NORM_SLACK = 1.01
CONV_TC = 512
CONV_HALO = 32
CONV_SUB = 64
FF_CHUNK = 768
FF_CHUNKS = tuple((c, min(c + FF_CHUNK, D_FF)) for c in range(0, D_FF, FF_CHUNK))
VMEM_LIMIT = 56 * 1024 * 1024


def _split3(x):
    hi = x.astype(BF16)
    r1 = x - hi.astype(F32)
    mid = r1.astype(BF16)
    lo = (r1 - mid.astype(F32)).astype(BF16)
    return hi, mid, lo


def _dot(a, b):
    return jnp.dot(a, b, preferred_element_type=F32)


def _dot_nt(a, b):
    return lax.dot_general(a, b, (((1,), (1,)), ((), ())), preferred_element_type=F32)


def _sigmoid(x):
    return 1.0 / (1.0 + jnp.exp(-x))


def _group(h):
    return slice(h * LANES, (h + 1) * LANES)


def _placement_constants():
    cq = np.zeros((LANES, AUG), np.float32)
    ck = np.zeros((LANES, AUG), np.float32)
    ones_q = np.zeros((1, AUG), np.float32)
    ones_k = np.zeros((1, AUG), np.float32)
    for h in range(FOX_HEADS):
        for j in range(3):
            cq[j * SUBLANES + h, h * LANES + HEAD_DIM + j] = 1.0
            ones_k[0, h * LANES + HEAD_DIM + j] = 1.0
            ck[j * SUBLANES + h, h * LANES + HEAD_DIM + 3 + j] = -1.0
            ones_q[0, h * LANES + HEAD_DIM + 3 + j] = 1.0
    return jnp.asarray(cq, BF16), jnp.asarray(ck, BF16), jnp.asarray(ones_q), jnp.asarray(ones_k)


def _in_proj_weight(w):
    return jnp.pad(w.astype(BF16), ((0, 0), (0, 0), (0, IN_PAD - w.shape[2])))


def _in_proj_kernel(nt, x_ref, g1_ref, w_ref, bf_ref, gains_ref, hm_ref, tri_ref,
                    cq_ref, ck_ref, onesq_ref, onesk_ref,
                    glu_ref, qb_ref, kb_ref, vbT_ref, qc_ref, kc_ref, vcT_ref, cmax_ref, cmin_ref,
                    h_scr, carry_scr):
    r = pl.program_id(0)

    @pl.when(r % nt == 0)
    def _():
        carry_scr[...] = jnp.zeros_like(carry_scr)

    x = x_ref[...]
    ms = jnp.mean(x * x, axis=-1, keepdims=True)
    h_scr[...] = (x * lax.rsqrt(ms + RMS_EPS) * g1_ref[...]).astype(BF16)
    lane = lax.broadcasted_iota(jnp.int32, (TM, LANES), 1)
    low = lane < HEAD_DIM

    def seg(c0, c1):
        return _dot(h_scr[...], w_ref[0, :, c0:c1])

    def store_heads(yq, yk, q_ref, k_ref, gains, upper_q, upper_k):
        y = jnp.concatenate([yq, yk], axis=1)
        sq = y * y
        hi = sq.astype(BF16)
        lo = (sq - hi.astype(F32)).astype(BF16)
        hm = hm_ref[...]
        msq = jnp.concatenate([_dot(hi[:, c:c + HM], hm) + _dot(lo[:, c:c + HM], hm) for c in range(0, AUG, HM)],
                              axis=1)
        y = y * lax.rsqrt(msq + RMS_EPS) * gains
        for ref, upper, first in ((q_ref, upper_q, 0), (k_ref, upper_k, HEADS // 2)):
            for pair in range(HEADS // 2):
                both = y[:, _group(first + pair)]
                ref[:, _group(2 * pair)] = jnp.where(low, both, upper(2 * pair)).astype(BF16)
                ref[:, _group(2 * pair + 1)] = jnp.where(
                    low, pltpu.roll(both, HEAD_DIM, axis=1), upper(2 * pair + 1)).astype(BF16)

    def pack_terms(t):
        hi, mid, lo = _split3(jnp.where(lane < FOX_HEADS, t, 0.0))
        return (hi.astype(F32) + pltpu.roll(mid.astype(F32), SUBLANES, axis=1)
                + pltpu.roll(lo.astype(F32), 2 * SUBLANES, axis=1)).astype(BF16)

    def unpack_terms(p):
        return p + pltpu.roll(p, LANES - SUBLANES, axis=1) + pltpu.roll(p, LANES - 2 * SUBLANES, axis=1)

    def store_vT(ref, v):
        vT = v.T.astype(BF16)
        row = lax.broadcasted_iota(jnp.int32, (BF16_ROWS, TM), 0)
        tail = jnp.where(row == 0, 1.0, 0.0).astype(BF16)
        for h in range(HEADS):
            ref[0, h * V_ROWS:h * V_ROWS + HEAD_DIM, :] = vT[h * HEAD_DIM:(h + 1) * HEAD_DIM, :]
            ref[0, h * V_ROWS + HEAD_DIM:(h + 1) * V_ROWS, :] = tail

    col_qk = 2 * CONV_CH
    col_vq, col_kv = col_qk + 2 * MOBA_DIM, col_qk + 3 * MOBA_DIM + FOX_DIM
    z = seg(IN_MAIN, IN_PAD) + bf_ref[...]
    y2 = seg(col_vq, col_kv)

    logf = jnp.minimum(z, 0.0) - jnp.log1p(jnp.exp(-jnp.abs(z)))
    csum = unpack_terms(_dot(tri_ref[...], pack_terms(logf)))
    c = csum + carry_scr[...]
    carry_scr[...] = c[TM - 1:TM, :]
    y3 = seg(col_kv, IN_MAIN)

    c2 = c * LOG2E
    for t in range(TM // TQ):
        part = c2[t * TQ:(t + 1) * TQ, :]
        cmax_ref[pl.ds(r * (TM // TQ) + t, 1), :] = jnp.max(part, axis=0, keepdims=True)
        cmin_ref[pl.ds(r * (TM // TQ) + t, 1), :] = jnp.min(part, axis=0, keepdims=True)
    packed = pack_terms(c2)
    upper_q = _dot(packed, cq_ref[...]) + onesq_ref[...]
    upper_k = _dot(packed, ck_ref[...]) + onesk_ref[...]
    y1 = seg(col_qk, col_vq)
    store_heads(y2[:, MOBA_DIM:], y3[:, :FOX_DIM], qc_ref, kc_ref, gains_ref[1:2, :],
                lambda h: upper_q[:, _group(h)], lambda h: upper_k[:, _group(h)])
    store_vT(vbT_ref, y2[:, :MOBA_DIM])
    store_vT(vcT_ref, y3[:, FOX_DIM:])
    a = seg(0, col_qk)

    pos = (r % nt) * TM + lax.broadcasted_iota(jnp.int32, (TM, LANES), 0)
    in_sel = (lane >= HEAD_DIM) & (lane < HEAD_DIM + 3 * SEL_LANES)
    onehot = jnp.where(in_sel & ((lane & (SEL_LANES - 1)) == pos // MOBA_BLOCK), 1.0, 0.0)
    store_heads(y1[:, :MOBA_DIM], y1[:, MOBA_DIM:], qb_ref, kb_ref, gains_ref[0:1, :],
                lambda h: 0.0, lambda h: onehot)
    glu_ref[...] = a[:, :CONV_CH] * _sigmoid(a[:, CONV_CH:])


def _in_proj(x2, g1, w, layer, bf, gains, hm, tri, placement, batch, seq):
    n = x2.shape[0]
    nt = seq // TM
    const = lambda shape: pl.BlockSpec(shape, lambda r: (0,) * len(shape), pipeline_mode=pl.Buffered(1))
    rows = lambda width: pl.BlockSpec((TM, width), lambda r: (r, 0))
    tposed = lambda height: pl.BlockSpec((1, height, TM), lambda r: (r // nt, 0, r % nt))
    out_shape = (
        jax.ShapeDtypeStruct((n, CONV_CH), F32),
        jax.ShapeDtypeStruct((n, AUG), BF16),
        jax.ShapeDtypeStruct((n, AUG), BF16),
        jax.ShapeDtypeStruct((batch, HEADS * V_ROWS, seq), BF16),
        jax.ShapeDtypeStruct((n, AUG), BF16),
        jax.ShapeDtypeStruct((n, AUG), BF16),
        jax.ShapeDtypeStruct((batch, HEADS * V_ROWS, seq), BF16),
        jax.ShapeDtypeStruct((n // TQ, LANES), F32),
        jax.ShapeDtypeStruct((n // TQ, LANES), F32),
    )
    whole = pl.BlockSpec((n // TQ, LANES), lambda r: (0, 0))
    return pl.pallas_call(
        functools.partial(_in_proj_kernel, nt),
        grid=(n // TM,),
        in_specs=[rows(D_MODEL), const((1, D_MODEL)),
                  pl.BlockSpec((1, D_MODEL, IN_PAD), lambda r: (layer, 0, 0), pipeline_mode=pl.Buffered(1)),
                  const((1, LANES)),
                  const((2, AUG)), const((HM, HM)), const((TM, TM)),
                  const((LANES, AUG)), const((LANES, AUG)), const((1, AUG)), const((1, AUG))],
        out_specs=(rows(CONV_CH), rows(AUG), rows(AUG), tposed(HEADS * V_ROWS),
                   rows(AUG), rows(AUG), tposed(HEADS * V_ROWS), whole, whole),
        out_shape=out_shape,
        scratch_shapes=[pltpu.VMEM((TM, D_MODEL), BF16), pltpu.VMEM((1, LANES), F32)],
        compiler_params=pltpu.CompilerParams(dimension_semantics=("arbitrary",),
                                             vmem_limit_bytes=VMEM_LIMIT),
        name="in_proj",
    )(x2, g1, w, bf, gains, hm, tri, *placement)


def _softmax_steps(score_fn, m_scr, acc_scr, first):
    for g0 in range(0, HEADS, HEAD_GROUP):
        hs = range(g0, min(g0 + HEAD_GROUP, HEADS))
        tiles = [score_fn(h) for h in hs]
        tile_max = [jnp.max(sT, axis=0, keepdims=True) for sT, _ in tiles]
        if first:
            m_new = tile_max
        else:
            m_old = [m_scr[h:h + 1, :] for h in hs]
            m_new = [jnp.maximum(a, b) for a, b in zip(m_old, tile_max)]
        pvs = [_dot(vT, jnp.exp2(sT - mn).astype(BF16)) for (sT, vT), mn in zip(tiles, m_new)]
        for n, h in enumerate(hs):
            if first:
                acc_scr[h] = pvs[n]
            else:
                acc_scr[h] = jnp.exp2(m_old[n] - m_new[n]) * acc_scr[h] + pvs[n]
            m_scr[h:h + 1, :] = m_new[n]


def _softmax_steps_lazy(score_fn, m_scr, acc_scr, seed_rows=None):
    first = seed_rows is not None
    tiles = [score_fn(h) for h in range(HEADS)]
    for h in range(HEADS):
        sT, vT = tiles[h]
        if first:
            m_old = jnp.max(sT[seed_rows[0]:seed_rows[1], :], axis=0, keepdims=True)
        else:
            m_old = m_scr[h:h + 1, :]
        tile_max = jnp.max(sT, axis=0, keepdims=True)
        pv = _dot(vT, jnp.exp2(sT - m_old).astype(BF16))
        m_new = jnp.maximum(m_old, tile_max)
        acc_scr[h] = jnp.exp2(m_old - m_new) * (pv if first else acc_scr[h] + pv)
        m_scr[h:h + 1, :] = m_new
        rise = tile_max - m_old
        m_scr[SUBLANES + h:SUBLANES + h + 1, :] = (
            rise if first else jnp.maximum(m_scr[SUBLANES + h:SUBLANES + h + 1, :], rise))


def _lazy_overflowed(m_scr):
    return jnp.max(m_scr[SUBLANES:SUBLANES + HEADS, :]) > LAZY_LIMIT


def _write_heads(o_ref, acc_scr):
    for pair in range(HEADS // 2):
        both = []
        for h in (2 * pair, 2 * pair + 1):
            acc = acc_scr[h]
            both.append(acc[:HEAD_DIM, :] / acc[HEAD_DIM:HEAD_DIM + 1, :])
        o_ref[:, pair * LANES:(pair + 1) * LANES] = jnp.concatenate(both, axis=0).T.astype(BF16)


def _key_tile(k_ref, vT_ref, h, k0, tk=TK):
    return k_ref[pl.ds(k0, tk), _group(h)], vT_ref[0, h * V_ROWS:(h + 1) * V_ROWS, pl.ds(k0, tk)]


def _past_tiles(first, count, step, merge):
    big = merge * TK

    def body(j, carry):
        step(pl.multiple_of(first * TK + j * big, TK), big)
        return carry

    lax.fori_loop(0, count // merge, body, 0)
    done = first + (count // merge) * merge
    count = first + count
    part = merge // 2
    while part >= 1:
        rem = count - done

        @pl.when(rem >= part)
        def _(done=done, part=part):
            step(pl.multiple_of(done * TK, TK), part * TK)

        done = done + jnp.where(rem >= part, part, 0)
        part //= 2


_ATTN_SCRATCH = [
    pltpu.VMEM((2 * SUBLANES, TQ), F32),
    pltpu.VMEM((HEADS, V_ROWS, TQ), F32),
]


def _attention_specs(batch, seq):
    nq = seq // TQ
    in_specs = [pl.BlockSpec((TQ, AUG), lambda b, i: (b * nq + i, 0)),
                pl.BlockSpec((seq, AUG), lambda b, i: (b, 0)),
                pl.BlockSpec((1, HEADS * V_ROWS, seq), lambda b, i: (b, 0, 0))]
    out_spec = pl.BlockSpec((TQ, MOBA_DIM), lambda b, i: (b * nq + i, 0))
    params = pltpu.CompilerParams(dimension_semantics=("arbitrary", "arbitrary"), vmem_limit_bytes=VMEM_LIMIT)
    return (batch, nq), in_specs, out_spec, params


def _t5_bucket(dist):
    n = jnp.maximum(dist, 0)
    max_exact = NUM_BUCKETS // 2
    nf = jnp.maximum(n, 1).astype(F32)
    large = max_exact + (jnp.log(nf / max_exact) / math.log(MAX_DISTANCE / max_exact)
                         * (NUM_BUCKETS - max_exact)).astype(jnp.int32)
    large = jnp.minimum(large, NUM_BUCKETS - 1)
    return jnp.where(n < max_exact, n, large)


def _bias_kernel(bucket_ref, rb_ref, o_ref):
    h = pl.program_id(0)
    bucket = bucket_ref[...]
    acc = jnp.zeros(bucket.shape, F32)
    for b in range(NUM_BUCKETS):
        acc = jnp.where(bucket == b, rb_ref[b, h], acc)
    key = lax.broadcasted_iota(jnp.int32, bucket.shape, 0)
    qry = lax.broadcasted_iota(jnp.int32, bucket.shape, 1)
    o_ref[0] = jnp.where((key >= MOBA_BLOCK) | (key <= qry), acc * LOG2E, NEG)


def _moba_bias(rel_bias):
    key = lax.broadcasted_iota(jnp.int32, (2 * MOBA_BLOCK, TQ), 0)
    qry = lax.broadcasted_iota(jnp.int32, (2 * MOBA_BLOCK, TQ), 1)
    bucket = _t5_bucket(jnp.where(key < MOBA_BLOCK, qry - key, qry + 2 * MOBA_BLOCK - key))
    return pl.pallas_call(
        _bias_kernel,
        grid=(MOBA_HEADS,),
        in_specs=[pl.BlockSpec((2 * MOBA_BLOCK, TQ), lambda h: (0, 0)),
                  pl.BlockSpec(memory_space=pltpu.SMEM)],
        out_specs=pl.BlockSpec((1, 2 * MOBA_BLOCK, TQ), lambda h: (h, 0, 0)),
        out_shape=jax.ShapeDtypeStruct((MOBA_HEADS, 2 * MOBA_BLOCK, TQ), F32),
        compiler_params=pltpu.CompilerParams(dimension_semantics=("arbitrary",)),
        name="moba_bias",
    )(bucket, rel_bias)


def _moba_kernel(nblk, q_ref, k_ref, vT_ref, bias_ref, rb_ref, o_ref, m_scr, acc_scr, qm_scr, kmean_scr):
    i = pl.program_id(1)

    @pl.when(i == 0)
    def _():
        kmean_scr[...] = jnp.zeros_like(kmean_scr)
        for n in range(nblk):
            blk = k_ref[n * MOBA_BLOCK:(n + 1) * MOBA_BLOCK, :].astype(F32)
            kmean_scr[n:n + 1, :] = jnp.mean(blk, axis=0, keepdims=True)

    q0 = pl.multiple_of(i * TQ, TQ)
    blk_row = lax.broadcasted_iota(jnp.int32, (SEL_LANES, TQ), 0)

    for h in range(HEADS):
        qg = q_ref[:, _group(h)]
        terms = _dot_nt(jnp.concatenate(_split3(kmean_scr[:, _group(h)]), axis=0), qg)
        gate = terms[:SEL_LANES] + terms[SEL_LANES:2 * SEL_LANES] + terms[2 * SEL_LANES:]
        gate = jnp.where(blk_row < i, gate, -jnp.inf)
        picked = jnp.zeros((SEL_LANES, TQ), F32)
        for _ in range(MOBA_TOPK):
            best = jnp.max(gate, axis=0, keepdims=True)
            first = jnp.min(jnp.where(gate == best, blk_row, SEL_LANES), axis=0, keepdims=True)
            pick = blk_row == first
            picked = jnp.where(pick, 1.0, picked)
            gate = jnp.where(pick, -jnp.inf, gate)
        chosen = (picked > 0.0) & (blk_row < i)
        far = jnp.where(blk_row == i - 1, 0.0, rb_ref[NUM_BUCKETS - 1, h] * LOG2E)
        term = jnp.where(chosen, far, jnp.where(blk_row < i, NEG, 0.0))
        t_hi, t_mid, t_lo = _split3(term)
        lanes = jnp.concatenate(
            [jnp.zeros((HEAD_DIM, TQ), F32), t_hi.astype(F32), t_mid.astype(F32), t_lo.astype(F32),
             jnp.zeros((LANES - HEAD_DIM - 3 * SEL_LANES, TQ), F32)], axis=0)
        qm_scr[h] = (qg.astype(F32) + lanes.T).astype(BF16)

    def attend(lazy):
        def first_step(scores):
            if lazy:
                _softmax_steps_lazy(scores, m_scr, acc_scr, seed_rows=(0, BF16_ROWS))
            else:
                _softmax_steps(scores, m_scr, acc_scr, first=True)

        @pl.when(i == 0)
        def _():
            def own_scores(h):
                kj, vTj = _key_tile(k_ref, vT_ref, h, q0)
                return _dot_nt(kj, qm_scr[h]) + bias_ref[h, :MOBA_BLOCK, :], vTj

            first_step(own_scores)

        @pl.when(i >= 1)
        def _():
            k0 = pl.multiple_of((i - 1) * TK, TK)

            def near_scores(h):
                k_own, vT_own = _key_tile(k_ref, vT_ref, h, q0)
                k_prev, vT_prev = _key_tile(k_ref, vT_ref, h, k0)
                sT = _dot_nt(jnp.concatenate([k_own, k_prev], axis=0), qm_scr[h]) + bias_ref[h]
                return sT, jnp.concatenate([vT_own, vT_prev], axis=1)

            first_step(near_scores)

        def far_step(k0, tk):
            def far_scores(h):
                kj, vTj = _key_tile(k_ref, vT_ref, h, k0, tk)
                return _dot_nt(kj, qm_scr[h]), vTj

            if lazy:
                _softmax_steps_lazy(far_scores, m_scr, acc_scr)
            else:
                _softmax_steps(far_scores, m_scr, acc_scr, first=False)

        _past_tiles(0, jnp.maximum(i - 1, 0), far_step, MOBA_MERGE if lazy else EXACT_MERGE)

    attend(lazy=True)

    @pl.when(_lazy_overflowed(m_scr))
    def _():
        attend(lazy=False)

    _write_heads(o_ref, acc_scr)


def _moba(q, k, vT, bias, rel_bias, batch, seq):
    nblk = seq // MOBA_BLOCK
    grid, in_specs, out_spec, params = _attention_specs(batch, seq)
    return pl.pallas_call(
        functools.partial(_moba_kernel, nblk),
        grid=grid,
        in_specs=in_specs + [pl.BlockSpec((MOBA_HEADS, 2 * MOBA_BLOCK, TQ), lambda b, i: (0, 0, 0)),
                             pl.BlockSpec(memory_space=pltpu.SMEM)],
        out_specs=out_spec,
        out_shape=jax.ShapeDtypeStruct((q.shape[0], MOBA_DIM), BF16),
        scratch_shapes=_ATTN_SCRATCH + [pltpu.VMEM((HEADS, TQ, LANES), BF16),
                                        pltpu.VMEM((SEL_LANES, AUG), F32)],
        compiler_params=params,
        name="moba_attention",
    )(q, k, vT, bias, rel_bias)


def _fox_kernel(q_ref, k_ref, vT_ref, cmax_ref, cmin_ref, bound_ref, o_ref, m_scr, acc_scr):
    i = pl.program_id(1)
    q0 = pl.multiple_of(i * TQ, TQ)

    ntiles = cmin_ref.shape[0]
    gap = cmax_ref[pl.ds(i, 1), :] - cmin_ref[...]
    limit = -(2.0 * bound_ref[0] + (UNDERFLOW_LOG2 + ROUNDING_MARGIN))
    head_lane = lax.broadcasted_iota(jnp.int32, (ntiles, LANES), 1) < FOX_HEADS
    live = jnp.max(jnp.where(head_lane & (gap > limit), 1, 0), axis=1, keepdims=True)
    tile = lax.broadcasted_iota(jnp.int32, (ntiles, 1), 0)
    first_live = jnp.min(jnp.where((live > 0) & (tile < i), tile, i))

    def scores(h, k0, tk=TK):
        kj, vTj = _key_tile(k_ref, vT_ref, h, k0, tk)
        return _dot_nt(kj, q_ref[:, _group(h)]), vTj

    key = lax.broadcasted_iota(jnp.int32, (TK, TQ), 0)
    qry = lax.broadcasted_iota(jnp.int32, (TK, TQ), 1)

    def diag_scores(h):
        sT, vTj = scores(h, q0)
        return jnp.where(key <= qry, sT, NEG), vTj

    def attend(lazy):
        _softmax_steps(diag_scores, m_scr, acc_scr, first=True)

        def past_step(k0, tk):
            if lazy:
                _softmax_steps_lazy(lambda h: scores(h, k0, tk), m_scr, acc_scr)
            else:
                _softmax_steps(lambda h: scores(h, k0, tk), m_scr, acc_scr, first=False)

        _past_tiles(first_live, i - first_live, past_step, FOX_MERGE if lazy else EXACT_MERGE)

    m_scr[SUBLANES:, :] = jnp.zeros((SUBLANES, TQ), F32)
    attend(lazy=True)

    @pl.when(_lazy_overflowed(m_scr))
    def _():
        attend(lazy=False)

    _write_heads(o_ref, acc_scr)


def _fox(q, k, vT, cmax, cmin, bound, batch, seq):
    grid, in_specs, out_spec, params = _attention_specs(batch, seq)
    ntiles = seq // TQ
    return pl.pallas_call(
        _fox_kernel,
        grid=grid,
        in_specs=in_specs + [pl.BlockSpec((ntiles, LANES), lambda b, i: (b, 0)),
                             pl.BlockSpec((ntiles, LANES), lambda b, i: (b, 0)),
                             pl.BlockSpec(memory_space=pltpu.SMEM)],
        out_specs=out_spec,
        out_shape=jax.ShapeDtypeStruct((q.shape[0], FOX_DIM), BF16),
        scratch_shapes=_ATTN_SCRATCH,
        compiler_params=params,
        name="fox_attention",
    )(q, k, vT, cmax, cmin, bound)


def _conv_rows(w_ref, b_ref, lg_ref, lb_ref, sbuf, r0):
    shift = CONV_HALO - (CONV_WIDTH - 1)
    acc = jnp.zeros((CONV_SUB, CONV_CH), F32) + b_ref[...]
    for j in range(CONV_WIDTH):
        rho, base = (j + shift) % SUBLANES, (j + shift) // SUBLANES * SUBLANES
        acc = acc + w_ref[j:j + 1, :] * sbuf[rho, r0 + base:r0 + base + CONV_SUB, :]
    mu = jnp.mean(acc, axis=-1, keepdims=True)
    d = acc - mu
    var = jnp.mean(d * d, axis=-1, keepdims=True)
    y = d * lax.rsqrt(var + LN_EPS) * lg_ref[...] + lb_ref[...]
    return (y * _sigmoid(y)).astype(BF16)


def _conv_kernel(nt, glu_ref, w_ref, b_ref, lg_ref, lb_ref, o_ref, hbuf, sbuf):
    t = pl.program_id(0) % nt

    @pl.when(t == 0)
    def _():
        hbuf[0:CONV_HALO, :] = jnp.zeros((CONV_HALO, CONV_CH), F32)

    @pl.when(t > 0)
    def _():
        hbuf[0:CONV_HALO, :] = hbuf[CONV_TC:CONV_TC + CONV_HALO, :]

    hbuf[CONV_HALO:, :] = glu_ref[...]
    for rho in range(SUBLANES):
        rows = CONV_TC + CONV_HALO - (SUBLANES if rho else 0)
        sbuf[rho, 0:rows, :] = hbuf[rho:rho + rows, :]
    for r0 in range(0, CONV_TC, CONV_SUB):
        o_ref[r0:r0 + CONV_SUB, :] = _conv_rows(w_ref, b_ref, lg_ref, lb_ref, sbuf, r0)


def _conv(glu, w, b, lg, lb, seq):
    n = glu.shape[0]
    const = lambda shape: pl.BlockSpec(shape, lambda r: (0,) * len(shape))
    return pl.pallas_call(
        functools.partial(_conv_kernel, seq // CONV_TC),
        grid=(n // CONV_TC,),
        in_specs=[pl.BlockSpec((CONV_TC, CONV_CH), lambda r: (r, 0)),
                  const((CONV_WIDTH, CONV_CH)), const((1, CONV_CH)), const((1, CONV_CH)), const((1, CONV_CH))],
        out_specs=pl.BlockSpec((CONV_TC, CONV_CH), lambda r: (r, 0)),
        out_shape=jax.ShapeDtypeStruct((n, CONV_CH), BF16),
        scratch_shapes=[pltpu.VMEM((CONV_HALO + CONV_TC, CONV_CH), F32),
                        pltpu.VMEM((SUBLANES, CONV_HALO + CONV_TC, CONV_CH), F32)],
        compiler_params=pltpu.CompilerParams(dimension_semantics=("arbitrary",)),
        name="conv_module",
    )(glu, w, b, lg, lb)


def _out_ffn_kernel(x_ref, ya_ref, yb_ref, yc_ref, wo_ref, g2_ref, wgu_ref, wd_ref, o_ref,
                    mix_scr, h2_scr, acc_scr):
    mix_scr[:, 0:CONV_CH] = ya_ref[...]
    mix_scr[:, CONV_CH:CONV_CH + MOBA_DIM] = yb_ref[...]
    mix_scr[:, CONV_CH + MOBA_DIM:] = yc_ref[...]
    x1 = x_ref[...] + _dot(mix_scr[...], wo_ref[0])
    ms = jnp.mean(x1 * x1, axis=-1, keepdims=True)
    h2_scr[...] = (x1 * lax.rsqrt(ms + RMS_EPS) * g2_ref[...]).astype(BF16)
    acc_scr[...] = x1
    for c0, c1 in FF_CHUNKS:
        g = _dot(h2_scr[...], wgu_ref[0, :, c0:c1])
        u = _dot(h2_scr[...], wgu_ref[0, :, D_FF + c0:D_FF + c1])
        act = (g * _sigmoid(g) * u).astype(BF16)
        acc_scr[...] += _dot(act, wd_ref[0, c0:c1, :])
    o_ref[...] = acc_scr[...]


def _out_ffn(x2, ya, yb, yc, wo, g2, wgu, wd, layer):
    n = x2.shape[0]
    const = lambda shape: pl.BlockSpec(shape, lambda r: (0,) * len(shape), pipeline_mode=pl.Buffered(1))
    of_layer = lambda shape: pl.BlockSpec((1,) + shape, lambda r: (layer, 0, 0), pipeline_mode=pl.Buffered(1))
    rows = lambda width: pl.BlockSpec((TM, width), lambda r: (r, 0))
    return pl.pallas_call(
        _out_ffn_kernel,
        grid=(n // TM,),
        in_specs=[rows(D_MODEL), rows(CONV_CH), rows(MOBA_DIM), rows(FOX_DIM),
                  of_layer((D_MODEL, D_MODEL)), const((1, D_MODEL)), of_layer((D_MODEL, 2 * D_FF)),
                  of_layer((D_FF, D_MODEL))],
        out_specs=rows(D_MODEL),
        out_shape=jax.ShapeDtypeStruct((n, D_MODEL), F32),
        scratch_shapes=[pltpu.VMEM((TM, D_MODEL), BF16), pltpu.VMEM((TM, D_MODEL), BF16),
                        pltpu.VMEM((TM, D_MODEL), F32)],
        compiler_params=pltpu.CompilerParams(dimension_semantics=("arbitrary",),
                                             vmem_limit_bytes=VMEM_LIMIT),
        name="out_ffn",
    )(x2, ya, yb, yc, wo, g2, wgu, wd)


def kernel(x, w_in, b_forget, conv_w, conv_b, conv_ln_g, conv_ln_b, moba_qn_g, moba_kn_g,
           fox_qn_g, fox_kn_g, rel_bias, w_out, norm1_g, norm2_g, w_gate_up, w_down):
    batch, seq, d_model = x.shape
    depth = w_in.shape[0]
    assert d_model == D_MODEL and seq % TM == 0 and seq % MOBA_BLOCK == 0 and TQ == MOBA_BLOCK == TK
    assert w_in.shape[2] == IN_MAIN + FOX_HEADS and MOBA_HEADS == FOX_HEADS <= SUBLANES
    assert seq // MOBA_BLOCK <= SEL_LANES and HEAD_DIM + 3 * SEL_LANES <= LANES

    hd = jnp.arange(HM) // HEAD_DIM
    head_mean = jnp.where(hd[:, None] == hd[None, :], 1.0 / HEAD_DIM, 0.0).astype(BF16)
    tri = (jnp.arange(TM)[:, None] >= jnp.arange(TM)[None, :]).astype(BF16)
    placement = _placement_constants()

    bias = _moba_bias(rel_bias.astype(F32))
    x2 = x.reshape(batch * seq, D_MODEL)
    w_all = _in_proj_weight(w_in)
    wo_all, wgu_all, wd_all = w_out.astype(BF16), w_gate_up.astype(BF16), w_down.astype(BF16)
    for l in range(depth):
        bf = jnp.pad(b_forget[l].astype(F32), (0, LANES - FOX_HEADS)).reshape(1, LANES)
        gains = jnp.stack([jnp.concatenate([jnp.tile(gq[l].astype(F32), HEADS) * QSCALE,
                                            jnp.tile(gk[l].astype(F32), HEADS)])
                           for gq, gk in ((moba_qn_g, moba_kn_g), (fox_qn_g, fox_kn_g))])
        bound = (HEAD_DIM * QSCALE * NORM_SLACK * jnp.max(jnp.abs(fox_qn_g[l])) * jnp.max(jnp.abs(fox_kn_g[l]))
                 ).astype(F32).reshape(1)
        glu, qb, kb, vbT, qc, kc, vcT, cmax, cmin = _in_proj(
            x2, norm1_g[l].reshape(1, D_MODEL), w_all, l, bf, gains, head_mean, tri, placement, batch, seq)
        ya = _conv(glu, conv_w[l], conv_b[l].reshape(1, CONV_CH), conv_ln_g[l].reshape(1, CONV_CH),
                   conv_ln_b[l].reshape(1, CONV_CH), seq)
        yb = _moba(qb, kb, vbT, bias, rel_bias.astype(F32), batch, seq)
        yc = _fox(qc, kc, vcT, cmax, cmin, bound, batch, seq)
        x2 = _out_ffn(x2, ya, yb, yc, wo_all, norm2_g[l].reshape(1, D_MODEL), wgu_all, wd_all, l)
    return x2.reshape(batch, seq, D_MODEL)
```

```python
import functools
import math

import numpy as np
import jax
import jax.numpy as jnp
from jax import lax
from jax.experimental import pallas as pl
from jax.experimental.pallas import tpu as pltpu

F32 = jnp.float32
BF16 = jnp.bfloat16

D_MODEL = 1024
HEAD_DIM = 64
CONV_CH = D_MODEL // 4
MOBA_DIM = 3 * D_MODEL // 8
FOX_DIM = D_MODEL - CONV_CH - MOBA_DIM
MOBA_HEADS = MOBA_DIM // HEAD_DIM
FOX_HEADS = FOX_DIM // HEAD_DIM
HEADS = MOBA_HEADS
CONV_WIDTH = 31
LN_EPS = 1e-5
RMS_EPS = 1e-6
MOBA_BLOCK = 256
MOBA_TOPK = 3
NUM_BUCKETS = 32
MAX_DISTANCE = 128
D_FF = -(-8 * D_MODEL // (3 * 256)) * 256
NEG = -1e30
LOG2E = math.log2(math.e)
QSCALE = HEAD_DIM ** -0.5 * LOG2E

LANES = 128
SUBLANES = 8
BF16_ROWS = 16
IN_MAIN = 2 * CONV_CH + 3 * MOBA_DIM + 3 * FOX_DIM
IN_PAD = IN_MAIN + LANES
AUG = HEADS * LANES
HM = 256
SEL_LANES = 16
V_ROWS = HEAD_DIM + BF16_ROWS

TM = 512
TQ = 256
TK = 256
HEAD_GROUP = 6
MOBA_MERGE = 8
FOX_MERGE = 4
LAZY_LIMIT = 64.0
UNDERFLOW_LOG2 = 160.0
ROUNDING_MARGIN = 2.0
NORM_SLACK = 1.01
CONV_TC = 512
CONV_HALO = 32
CONV_SUB = 64
FF_CHUNK = 768
FF_CHUNKS = tuple((c, min(c + FF_CHUNK, D_FF)) for c in range(0, D_FF, FF_CHUNK))
VMEM_LIMIT = 56 * 1024 * 1024


def _split3(x):
    hi = x.astype(BF16)
    r1 = x - hi.astype(F32)
    mid = r1.astype(BF16)
    lo = (r1 - mid.astype(F32)).astype(BF16)
    return hi, mid, lo


def _dot(a, b):
    return jnp.dot(a, b, preferred_element_type=F32)


def _dot_nt(a, b):
    return lax.dot_general(a, b, (((1,), (1,)), ((), ())), preferred_element_type=F32)


def _sigmoid(x):
    return 1.0 / (1.0 + jnp.exp(-x))


def _group(h):
    return slice(h * LANES, (h + 1) * LANES)


def _placement_constants():
    cq = np.zeros((LANES, AUG), np.float32)
    ck = np.zeros((LANES, AUG), np.float32)
    ones_q = np.zeros((1, AUG), np.float32)
    ones_k = np.zeros((1, AUG), np.float32)
    for h in range(FOX_HEADS):
        for j in range(3):
            cq[j * SUBLANES + h, h * LANES + HEAD_DIM + j] = 1.0
            ones_k[0, h * LANES + HEAD_DIM + j] = 1.0
            ck[j * SUBLANES + h, h * LANES + HEAD_DIM + 3 + j] = -1.0
            ones_q[0, h * LANES + HEAD_DIM + 3 + j] = 1.0
    return jnp.asarray(cq, BF16), jnp.asarray(ck, BF16), jnp.asarray(ones_q), jnp.asarray(ones_k)


def _in_proj_weight(w):
    return jnp.pad(w.astype(BF16), ((0, 0), (0, 0), (0, IN_PAD - w.shape[2])))


def _in_proj_kernel(nt, x_ref, g1_ref, w_ref, bf_ref, gains_ref, hm_ref, tri_ref,
                    cq_ref, ck_ref, onesq_ref, onesk_ref,
                    glu_ref, qb_ref, kb_ref, vbT_ref, qc_ref, kc_ref, vcT_ref, cmax_ref, cmin_ref,
                    h_scr, carry_scr):
    r = pl.program_id(0)

    @pl.when(r % nt == 0)
    def _():
        carry_scr[...] = jnp.zeros_like(carry_scr)

    x = x_ref[...]
    ms = jnp.mean(x * x, axis=-1, keepdims=True)
    h_scr[...] = (x * lax.rsqrt(ms + RMS_EPS) * g1_ref[...]).astype(BF16)
    lane = lax.broadcasted_iota(jnp.int32, (TM, LANES), 1)
    low = lane < HEAD_DIM

    def seg(c0, c1):
        return _dot(h_scr[...], w_ref[0, :, c0:c1])

    def store_heads(yq, yk, q_ref, k_ref, gains, upper_q, upper_k):
        y = jnp.concatenate([yq, yk], axis=1)
        sq = y * y
        hi = sq.astype(BF16)
        lo = (sq - hi.astype(F32)).astype(BF16)
        hm = hm_ref[...]
        msq = jnp.concatenate([_dot(hi[:, c:c + HM], hm) + _dot(lo[:, c:c + HM], hm) for c in range(0, AUG, HM)],
                              axis=1)
        y = y * lax.rsqrt(msq + RMS_EPS) * gains
        for ref, upper, first in ((q_ref, upper_q, 0), (k_ref, upper_k, HEADS // 2)):
            for pair in range(HEADS // 2):
                both = y[:, _group(first + pair)]
                ref[:, _group(2 * pair)] = jnp.where(low, both, upper(2 * pair)).astype(BF16)
                ref[:, _group(2 * pair + 1)] = jnp.where(
                    low, pltpu.roll(both, HEAD_DIM, axis=1), upper(2 * pair + 1)).astype(BF16)

    def pack_terms(t):
        hi, mid, lo = _split3(jnp.where(lane < FOX_HEADS, t, 0.0))
        return (hi.astype(F32) + pltpu.roll(mid.astype(F32), SUBLANES, axis=1)
                + pltpu.roll(lo.astype(F32), 2 * SUBLANES, axis=1)).astype(BF16)

    def unpack_terms(p):
        return p + pltpu.roll(p, LANES - SUBLANES, axis=1) + pltpu.roll(p, LANES - 2 * SUBLANES, axis=1)

    def store_vT(ref, v):
        vT = v.T.astype(BF16)
        row = lax.broadcasted_iota(jnp.int32, (BF16_ROWS, TM), 0)
        tail = jnp.where(row == 0, 1.0, 0.0).astype(BF16)
        for h in range(HEADS):
            ref[0, h * V_ROWS:h * V_ROWS + HEAD_DIM, :] = vT[h * HEAD_DIM:(h + 1) * HEAD_DIM, :]
            ref[0, h * V_ROWS + HEAD_DIM:(h + 1) * V_ROWS, :] = tail

    col_qk = 2 * CONV_CH
    col_vq, col_kv = col_qk + 2 * MOBA_DIM, col_qk + 3 * MOBA_DIM + FOX_DIM
    z = seg(IN_MAIN, IN_PAD) + bf_ref[...]
    y2 = seg(col_vq, col_kv)

    logf = jnp.minimum(z, 0.0) - jnp.log1p(jnp.exp(-jnp.abs(z)))
    csum = unpack_terms(_dot(tri_ref[...], pack_terms(logf)))
    c = csum + carry_scr[...]
    carry_scr[...] = c[TM - 1:TM, :]
    y3 = seg(col_kv, IN_MAIN)

    c2 = c * LOG2E
    for t in range(TM // TQ):
        part = c2[t * TQ:(t + 1) * TQ, :]
        cmax_ref[pl.ds(r * (TM // TQ) + t, 1), :] = jnp.max(part, axis=0, keepdims=True)
        cmin_ref[pl.ds(r * (TM // TQ) + t, 1), :] = jnp.min(part, axis=0, keepdims=True)
    packed = pack_terms(c2)
    upper_q = _dot(packed, cq_ref[...]) + onesq_ref[...]
    upper_k = _dot(packed, ck_ref[...]) + onesk_ref[...]
    y1 = seg(col_qk, col_vq)
    store_heads(y2[:, MOBA_DIM:], y3[:, :FOX_DIM], qc_ref, kc_ref, gains_ref[1:2, :],
                lambda h: upper_q[:, _group(h)], lambda h: upper_k[:, _group(h)])
    store_vT(vbT_ref, y2[:, :MOBA_DIM])
    store_vT(vcT_ref, y3[:, FOX_DIM:])
    a = seg(0, col_qk)

    pos = (r % nt) * TM + lax.broadcasted_iota(jnp.int32, (TM, LANES), 0)
    in_sel = (lane >= HEAD_DIM) & (lane < HEAD_DIM + 3 * SEL_LANES)
    onehot = jnp.where(in_sel & ((lane & (SEL_LANES - 1)) == pos // MOBA_BLOCK), 1.0, 0.0)
    store_heads(y1[:, :MOBA_DIM], y1[:, MOBA_DIM:], qb_ref, kb_ref, gains_ref[0:1, :],
                lambda h: 0.0, lambda h: onehot)
    glu_ref[...] = a[:, :CONV_CH] * _sigmoid(a[:, CONV_CH:])


def _in_proj(x2, g1, w, layer, bf, gains, hm, tri, placement, batch, seq):
    n = x2.shape[0]
    nt = seq // TM
    const = lambda shape: pl.BlockSpec(shape, lambda r: (0,) * len(shape), pipeline_mode=pl.Buffered(1))
    rows = lambda width: pl.BlockSpec((TM, width), lambda r: (r, 0))
    tposed = lambda height: pl.BlockSpec((1, height, TM), lambda r: (r // nt, 0, r % nt))
    out_shape = (
        jax.ShapeDtypeStruct((n, CONV_CH), F32),
        jax.ShapeDtypeStruct((n, AUG), BF16),
        jax.ShapeDtypeStruct((n, AUG), BF16),
        jax.ShapeDtypeStruct((batch, HEADS * V_ROWS, seq), BF16),
        jax.ShapeDtypeStruct((n, AUG), BF16),
        jax.ShapeDtypeStruct((n, AUG), BF16),
        jax.ShapeDtypeStruct((batch, HEADS * V_ROWS, seq), BF16),
        jax.ShapeDtypeStruct((n // TQ, LANES), F32),
        jax.ShapeDtypeStruct((n // TQ, LANES), F32),
    )
    whole = pl.BlockSpec((n // TQ, LANES), lambda r: (0, 0))
    return pl.pallas_call(
        functools.partial(_in_proj_kernel, nt),
        grid=(n // TM,),
        in_specs=[rows(D_MODEL), const((1, D_MODEL)),
                  pl.BlockSpec((1, D_MODEL, IN_PAD), lambda r: (layer, 0, 0), pipeline_mode=pl.Buffered(1)),
                  const((1, LANES)),
                  const((2, AUG)), const((HM, HM)), const((TM, TM)),
                  const((LANES, AUG)), const((LANES, AUG)), const((1, AUG)), const((1, AUG))],
        out_specs=(rows(CONV_CH), rows(AUG), rows(AUG), tposed(HEADS * V_ROWS),
                   rows(AUG), rows(AUG), tposed(HEADS * V_ROWS), whole, whole),
        out_shape=out_shape,
        scratch_shapes=[pltpu.VMEM((TM, D_MODEL), BF16), pltpu.VMEM((1, LANES), F32)],
        compiler_params=pltpu.CompilerParams(dimension_semantics=("arbitrary",),
                                             vmem_limit_bytes=VMEM_LIMIT),
        name="in_proj",
    )(x2, g1, w, bf, gains, hm, tri, *placement)


def _softmax_steps(score_fn, m_scr, acc_scr, first):
    for g0 in range(0, HEADS, HEAD_GROUP):
        hs = range(g0, min(g0 + HEAD_GROUP, HEADS))
        tiles = [score_fn(h) for h in hs]
        tile_max = [jnp.max(sT, axis=0, keepdims=True) for sT, _ in tiles]
        if first:
            m_new = tile_max
        else:
            m_old = [m_scr[h:h + 1, :] for h in hs]
            m_new = [jnp.maximum(a, b) for a, b in zip(m_old, tile_max)]
        pvs = [_dot(vT, jnp.exp2(sT - mn).astype(BF16)) for (sT, vT), mn in zip(tiles, m_new)]
        for n, h in enumerate(hs):
            if first:
                acc_scr[h] = pvs[n]
            else:
                acc_scr[h] = jnp.exp2(m_old[n] - m_new[n]) * acc_scr[h] + pvs[n]
            m_scr[h:h + 1, :] = m_new[n]


def _softmax_steps_lazy(score_fn, m_scr, acc_scr, pv_scr, seed_rows=None):
    first = seed_rows is not None
    worst = None
    tiles = [score_fn(h) for h in range(HEADS)]
    for h in range(HEADS):
        sT, vT = tiles[h]
        if first:
            m_old = jnp.max(sT[seed_rows[0]:seed_rows[1], :], axis=0, keepdims=True)
            m_scr[h:h + 1, :] = m_old
        else:
            m_old = m_scr[h:h + 1, :]
        tile_max = jnp.max(sT, axis=0, keepdims=True)
        pv_scr[h] = _dot(vT, jnp.exp2(sT - m_old).astype(BF16))
        m_scr[SUBLANES + h:SUBLANES + h + 1, :] = tile_max
        rise = tile_max - m_old
        worst = rise if worst is None else jnp.maximum(worst, rise)
    safe = jnp.max(worst) <= LAZY_LIMIT

    @pl.when(safe)
    def _():
        for h in range(HEADS):
            m_old = m_scr[h:h + 1, :]
            m_new = jnp.maximum(m_old, m_scr[SUBLANES + h:SUBLANES + h + 1, :])
            acc = pv_scr[h] if first else acc_scr[h] + pv_scr[h]
            acc_scr[h] = jnp.exp2(m_old - m_new) * acc
            m_scr[h:h + 1, :] = m_new

    @pl.when(jnp.logical_not(safe))
    def _():
        _softmax_steps(score_fn, m_scr, acc_scr, first=first)


def _write_heads(o_ref, acc_scr):
    for pair in range(HEADS // 2):
        both = []
        for h in (2 * pair, 2 * pair + 1):
            acc = acc_scr[h]
            both.append(acc[:HEAD_DIM, :] / acc[HEAD_DIM:HEAD_DIM + 1, :])
        o_ref[:, pair * LANES:(pair + 1) * LANES] = jnp.concatenate(both, axis=0).T.astype(BF16)


def _key_tile(k_ref, vT_ref, h, k0, tk=TK):
    return k_ref[pl.ds(k0, tk), _group(h)], vT_ref[0, h * V_ROWS:(h + 1) * V_ROWS, pl.ds(k0, tk)]


def _past_tiles(first, count, step, merge):
    big = merge * TK

    def body(j, carry):
        step(pl.multiple_of(first * TK + j * big, TK), big)
        return carry

    lax.fori_loop(0, count // merge, body, 0)
    done = first + (count // merge) * merge
    count = first + count
    part = merge // 2
    while part >= 1:
        rem = count - done

        @pl.when(rem >= part)
        def _(done=done, part=part):
            step(pl.multiple_of(done * TK, TK), part * TK)

        done = done + jnp.where(rem >= part, part, 0)
        part //= 2


_ATTN_SCRATCH = [
    pltpu.VMEM((2 * SUBLANES, TQ), F32),
    pltpu.VMEM((HEADS, V_ROWS, TQ), F32),
    pltpu.VMEM((HEADS, V_ROWS, TQ), F32),
]


def _attention_specs(batch, seq):
    nq = seq // TQ
    in_specs = [pl.BlockSpec((TQ, AUG), lambda b, i: (b * nq + i, 0)),
                pl.BlockSpec((seq, AUG), lambda b, i: (b, 0)),
                pl.BlockSpec((1, HEADS * V_ROWS, seq), lambda b, i: (b, 0, 0))]
    out_spec = pl.BlockSpec((TQ, MOBA_DIM), lambda b, i: (b * nq + i, 0))
    params = pltpu.CompilerParams(dimension_semantics=("arbitrary", "arbitrary"), vmem_limit_bytes=VMEM_LIMIT)
    return (batch, nq), in_specs, out_spec, params


def _t5_bucket(dist):
    n = jnp.maximum(dist, 0)
    max_exact = NUM_BUCKETS // 2
    nf = jnp.maximum(n, 1).astype(F32)
    large = max_exact + (jnp.log(nf / max_exact) / math.log(MAX_DISTANCE / max_exact)
                         * (NUM_BUCKETS - max_exact)).astype(jnp.int32)
    large = jnp.minimum(large, NUM_BUCKETS - 1)
    return jnp.where(n < max_exact, n, large)


def _bias_kernel(bucket_ref, rb_ref, o_ref):
    h = pl.program_id(0)
    bucket = bucket_ref[...]
    acc = jnp.zeros(bucket.shape, F32)
    for b in range(NUM_BUCKETS):
        acc = jnp.where(bucket == b, rb_ref[b, h], acc)
    key = lax.broadcasted_iota(jnp.int32, bucket.shape, 0)
    qry = lax.broadcasted_iota(jnp.int32, bucket.shape, 1)
    o_ref[0] = jnp.where((key >= MOBA_BLOCK) | (key <= qry), acc * LOG2E, NEG)


def _moba_bias(rel_bias):
    key = lax.broadcasted_iota(jnp.int32, (2 * MOBA_BLOCK, TQ), 0)
    qry = lax.broadcasted_iota(jnp.int32, (2 * MOBA_BLOCK, TQ), 1)
    bucket = _t5_bucket(jnp.where(key < MOBA_BLOCK, qry - key, qry + 2 * MOBA_BLOCK - key))
    return pl.pallas_call(
        _bias_kernel,
        grid=(MOBA_HEADS,),
        in_specs=[pl.BlockSpec((2 * MOBA_BLOCK, TQ), lambda h: (0, 0)),
                  pl.BlockSpec(memory_space=pltpu.SMEM)],
        out_specs=pl.BlockSpec((1, 2 * MOBA_BLOCK, TQ), lambda h: (h, 0, 0)),
        out_shape=jax.ShapeDtypeStruct((MOBA_HEADS, 2 * MOBA_BLOCK, TQ), F32),
        compiler_params=pltpu.CompilerParams(dimension_semantics=("arbitrary",)),
        name="moba_bias",
    )(bucket, rel_bias)


def _moba_kernel(nblk, q_ref, k_ref, vT_ref, bias_ref, rb_ref, o_ref, m_scr, acc_scr, pv_scr, qm_scr, kmean_scr):
    i = pl.program_id(1)

    @pl.when(i == 0)
    def _():
        kmean_scr[...] = jnp.zeros_like(kmean_scr)
        for n in range(nblk):
            blk = k_ref[n * MOBA_BLOCK:(n + 1) * MOBA_BLOCK, :].astype(F32)
            kmean_scr[n:n + 1, :] = jnp.mean(blk, axis=0, keepdims=True)

    q0 = pl.multiple_of(i * TQ, TQ)
    blk_row = lax.broadcasted_iota(jnp.int32, (SEL_LANES, TQ), 0)

    for h in range(HEADS):
        qg = q_ref[:, _group(h)]
        terms = _dot_nt(jnp.concatenate(_split3(kmean_scr[:, _group(h)]), axis=0), qg)
        gate = terms[:SEL_LANES] + terms[SEL_LANES:2 * SEL_LANES] + terms[2 * SEL_LANES:]
        gate = jnp.where(blk_row < i, gate, -jnp.inf)
        picked = jnp.zeros((SEL_LANES, TQ), F32)
        for _ in range(MOBA_TOPK):
            best = jnp.max(gate, axis=0, keepdims=True)
            first = jnp.min(jnp.where(gate == best, blk_row, SEL_LANES), axis=0, keepdims=True)
            pick = blk_row == first
            picked = jnp.where(pick, 1.0, picked)
            gate = jnp.where(pick, -jnp.inf, gate)
        chosen = (picked > 0.0) & (blk_row < i)
        far = jnp.where(blk_row == i - 1, 0.0, rb_ref[NUM_BUCKETS - 1, h] * LOG2E)
        term = jnp.where(chosen, far, jnp.where(blk_row < i, NEG, 0.0))
        t_hi, t_mid, t_lo = _split3(term)
        lanes = jnp.concatenate(
            [jnp.zeros((HEAD_DIM, TQ), F32), t_hi.astype(F32), t_mid.astype(F32), t_lo.astype(F32),
             jnp.zeros((LANES - HEAD_DIM - 3 * SEL_LANES, TQ), F32)], axis=0)
        qm_scr[h] = (qg.astype(F32) + lanes.T).astype(BF16)

    @pl.when(i == 0)
    def _():
        def own_scores(h):
            kj, vTj = _key_tile(k_ref, vT_ref, h, q0)
            return _dot_nt(kj, qm_scr[h]) + bias_ref[h, :MOBA_BLOCK, :], vTj

        _softmax_steps_lazy(own_scores, m_scr, acc_scr, pv_scr, seed_rows=(0, BF16_ROWS))

    @pl.when(i >= 1)
    def _():
        k0 = pl.multiple_of((i - 1) * TK, TK)

        def near_scores(h):
            k_own, vT_own = _key_tile(k_ref, vT_ref, h, q0)
            k_prev, vT_prev = _key_tile(k_ref, vT_ref, h, k0)
            sT = _dot_nt(jnp.concatenate([k_own, k_prev], axis=0), qm_scr[h]) + bias_ref[h]
            return sT, jnp.concatenate([vT_own, vT_prev], axis=1)

        _softmax_steps_lazy(near_scores, m_scr, acc_scr, pv_scr, seed_rows=(0, BF16_ROWS))

    def far_step(k0, tk):
        def far_scores(h):
            kj, vTj = _key_tile(k_ref, vT_ref, h, k0, tk)
            return _dot_nt(kj, qm_scr[h]), vTj

        _softmax_steps_lazy(far_scores, m_scr, acc_scr, pv_scr)

    _past_tiles(0, jnp.maximum(i - 1, 0), far_step, MOBA_MERGE)
    _write_heads(o_ref, acc_scr)


def _moba(q, k, vT, bias, rel_bias, batch, seq):
    nblk = seq // MOBA_BLOCK
    grid, in_specs, out_spec, params = _attention_specs(batch, seq)
    return pl.pallas_call(
        functools.partial(_moba_kernel, nblk),
        grid=grid,
        in_specs=in_specs + [pl.BlockSpec((MOBA_HEADS, 2 * MOBA_BLOCK, TQ), lambda b, i: (0, 0, 0)),
                             pl.BlockSpec(memory_space=pltpu.SMEM)],
        out_specs=out_spec,
        out_shape=jax.ShapeDtypeStruct((q.shape[0], MOBA_DIM), BF16),
        scratch_shapes=_ATTN_SCRATCH + [pltpu.VMEM((HEADS, TQ, LANES), BF16),
                                        pltpu.VMEM((SEL_LANES, AUG), F32)],
        compiler_params=params,
        name="moba_attention",
    )(q, k, vT, bias, rel_bias)


def _fox_kernel(q_ref, k_ref, vT_ref, cmax_ref, cmin_ref, bound_ref, o_ref, m_scr, acc_scr, pv_scr):
    i = pl.program_id(1)
    q0 = pl.multiple_of(i * TQ, TQ)

    ntiles = cmin_ref.shape[0]
    gap = cmax_ref[pl.ds(i, 1), :] - cmin_ref[...]
    limit = -(2.0 * bound_ref[0] + (UNDERFLOW_LOG2 + ROUNDING_MARGIN))
    head_lane = lax.broadcasted_iota(jnp.int32, (ntiles, LANES), 1) < FOX_HEADS
    live = jnp.max(jnp.where(head_lane & (gap > limit), 1, 0), axis=1, keepdims=True)
    tile = lax.broadcasted_iota(jnp.int32, (ntiles, 1), 0)
    first_live = jnp.min(jnp.where((live > 0) & (tile < i), tile, i))

    def scores(h, k0, tk=TK):
        kj, vTj = _key_tile(k_ref, vT_ref, h, k0, tk)
        return _dot_nt(kj, q_ref[:, _group(h)]), vTj

    key = lax.broadcasted_iota(jnp.int32, (TK, TQ), 0)
    qry = lax.broadcasted_iota(jnp.int32, (TK, TQ), 1)

    def diag_scores(h):
        sT, vTj = scores(h, q0)
        return jnp.where(key <= qry, sT, NEG), vTj

    _softmax_steps(diag_scores, m_scr, acc_scr, first=True)

    def past_step(k0, tk):
        _softmax_steps_lazy(lambda h: scores(h, k0, tk), m_scr, acc_scr, pv_scr)

    _past_tiles(first_live, i - first_live, past_step, FOX_MERGE)
    _write_heads(o_ref, acc_scr)


def _fox(q, k, vT, cmax, cmin, bound, batch, seq):
    grid, in_specs, out_spec, params = _attention_specs(batch, seq)
    ntiles = seq // TQ
    return pl.pallas_call(
        _fox_kernel,
        grid=grid,
        in_specs=in_specs + [pl.BlockSpec((ntiles, LANES), lambda b, i: (b, 0)),
                             pl.BlockSpec((ntiles, LANES), lambda b, i: (b, 0)),
                             pl.BlockSpec(memory_space=pltpu.SMEM)],
        out_specs=out_spec,
        out_shape=jax.ShapeDtypeStruct((q.shape[0], FOX_DIM), BF16),
        scratch_shapes=_ATTN_SCRATCH,
        compiler_params=params,
        name="fox_attention",
    )(q, k, vT, cmax, cmin, bound)


def _conv_rows(w_ref, b_ref, lg_ref, lb_ref, sbuf, r0):
    shift = CONV_HALO - (CONV_WIDTH - 1)
    acc = jnp.zeros((CONV_SUB, CONV_CH), F32) + b_ref[...]
    for j in range(CONV_WIDTH):
        rho, base = (j + shift) % SUBLANES, (j + shift) // SUBLANES * SUBLANES
        acc = acc + w_ref[j:j + 1, :] * sbuf[rho, r0 + base:r0 + base + CONV_SUB, :]
    mu = jnp.mean(acc, axis=-1, keepdims=True)
    d = acc - mu
    var = jnp.mean(d * d, axis=-1, keepdims=True)
    y = d * lax.rsqrt(var + LN_EPS) * lg_ref[...] + lb_ref[...]
    return (y * _sigmoid(y)).astype(BF16)


def _conv_kernel(nt, glu_ref, w_ref, b_ref, lg_ref, lb_ref, o_ref, hbuf, sbuf):
    t = pl.program_id(0) % nt

    @pl.when(t == 0)
    def _():
        hbuf[0:CONV_HALO, :] = jnp.zeros((CONV_HALO, CONV_CH), F32)

    @pl.when(t > 0)
    def _():
        hbuf[0:CONV_HALO, :] = hbuf[CONV_TC:CONV_TC + CONV_HALO, :]

    hbuf[CONV_HALO:, :] = glu_ref[...]
    for rho in range(SUBLANES):
        rows = CONV_TC + CONV_HALO - (SUBLANES if rho else 0)
        sbuf[rho, 0:rows, :] = hbuf[rho:rho + rows, :]
    for r0 in range(0, CONV_TC, CONV_SUB):
        o_ref[r0:r0 + CONV_SUB, :] = _conv_rows(w_ref, b_ref, lg_ref, lb_ref, sbuf, r0)


def _conv(glu, w, b, lg, lb, seq):
    n = glu.shape[0]
    const = lambda shape: pl.BlockSpec(shape, lambda r: (0,) * len(shape))
    return pl.pallas_call(
        functools.partial(_conv_kernel, seq // CONV_TC),
        grid=(n // CONV_TC,),
        in_specs=[pl.BlockSpec((CONV_TC, CONV_CH), lambda r: (r, 0)),
                  const((CONV_WIDTH, CONV_CH)), const((1, CONV_CH)), const((1, CONV_CH)), const((1, CONV_CH))],
        out_specs=pl.BlockSpec((CONV_TC, CONV_CH), lambda r: (r, 0)),
        out_shape=jax.ShapeDtypeStruct((n, CONV_CH), BF16),
        scratch_shapes=[pltpu.VMEM((CONV_HALO + CONV_TC, CONV_CH), F32),
                        pltpu.VMEM((SUBLANES, CONV_HALO + CONV_TC, CONV_CH), F32)],
        compiler_params=pltpu.CompilerParams(dimension_semantics=("arbitrary",)),
        name="conv_module",
    )(glu, w, b, lg, lb)


def _out_ffn_kernel(x_ref, ya_ref, yb_ref, yc_ref, wo_ref, g2_ref, wgu_ref, wd_ref, o_ref,
                    mix_scr, h2_scr, acc_scr):
    mix_scr[:, 0:CONV_CH] = ya_ref[...]
    mix_scr[:, CONV_CH:CONV_CH + MOBA_DIM] = yb_ref[...]
    mix_scr[:, CONV_CH + MOBA_DIM:] = yc_ref[...]
    x1 = x_ref[...] + _dot(mix_scr[...], wo_ref[0])
    ms = jnp.mean(x1 * x1, axis=-1, keepdims=True)
    h2_scr[...] = (x1 * lax.rsqrt(ms + RMS_EPS) * g2_ref[...]).astype(BF16)
    acc_scr[...] = x1
    for c0, c1 in FF_CHUNKS:
        g = _dot(h2_scr[...], wgu_ref[0, :, c0:c1])
        u = _dot(h2_scr[...], wgu_ref[0, :, D_FF + c0:D_FF + c1])
        act = (g * _sigmoid(g) * u).astype(BF16)
        acc_scr[...] += _dot(act, wd_ref[0, c0:c1, :])
    o_ref[...] = acc_scr[...]


def _out_ffn(x2, ya, yb, yc, wo, g2, wgu, wd, layer):
    n = x2.shape[0]
    const = lambda shape: pl.BlockSpec(shape, lambda r: (0,) * len(shape), pipeline_mode=pl.Buffered(1))
    of_layer = lambda shape: pl.BlockSpec((1,) + shape, lambda r: (layer, 0, 0), pipeline_mode=pl.Buffered(1))
    rows = lambda width: pl.BlockSpec((TM, width), lambda r: (r, 0))
    return pl.pallas_call(
        _out_ffn_kernel,
        grid=(n // TM,),
        in_specs=[rows(D_MODEL), rows(CONV_CH), rows(MOBA_DIM), rows(FOX_DIM),
                  of_layer((D_MODEL, D_MODEL)), const((1, D_MODEL)), of_layer((D_MODEL, 2 * D_FF)),
                  of_layer((D_FF, D_MODEL))],
        out_specs=rows(D_MODEL),
        out_shape=jax.ShapeDtypeStruct((n, D_MODEL), F32),
        scratch_shapes=[pltpu.VMEM((TM, D_MODEL), BF16), pltpu.VMEM((TM, D_MODEL), BF16),
                        pltpu.VMEM((TM, D_MODEL), F32)],
        compiler_params=pltpu.CompilerParams(dimension_semantics=("arbitrary",),
                                             vmem_limit_bytes=VMEM_LIMIT),
        name="out_ffn",
    )(x2, ya, yb, yc, wo, g2, wgu, wd)


def kernel(x, w_in, b_forget, conv_w, conv_b, conv_ln_g, conv_ln_b, moba_qn_g, moba_kn_g,
           fox_qn_g, fox_kn_g, rel_bias, w_out, norm1_g, norm2_g, w_gate_up, w_down):
    batch, seq, d_model = x.shape
    depth = w_in.shape[0]
    assert d_model == D_MODEL and seq % TM == 0 and seq % MOBA_BLOCK == 0 and TQ == MOBA_BLOCK == TK
    assert w_in.shape[2] == IN_MAIN + FOX_HEADS and MOBA_HEADS == FOX_HEADS <= SUBLANES
    assert seq // MOBA_BLOCK <= SEL_LANES and HEAD_DIM + 3 * SEL_LANES <= LANES

    hd = jnp.arange(HM) // HEAD_DIM
    head_mean = jnp.where(hd[:, None] == hd[None, :], 1.0 / HEAD_DIM, 0.0).astype(BF16)
    tri = (jnp.arange(TM)[:, None] >= jnp.arange(TM)[None, :]).astype(BF16)
    placement = _placement_constants()

    bias = _moba_bias(rel_bias.astype(F32))
    x2 = x.reshape(batch * seq, D_MODEL)
    w_all = _in_proj_weight(w_in)
    wo_all, wgu_all, wd_all = w_out.astype(BF16), w_gate_up.astype(BF16), w_down.astype(BF16)
    for l in range(depth):
        bf = jnp.pad(b_forget[l].astype(F32), (0, LANES - FOX_HEADS)).reshape(1, LANES)
        gains = jnp.stack([jnp.concatenate([jnp.tile(gq[l].astype(F32), HEADS) * QSCALE,
                                            jnp.tile(gk[l].astype(F32), HEADS)])
                           for gq, gk in ((moba_qn_g, moba_kn_g), (fox_qn_g, fox_kn_g))])
        bound = (HEAD_DIM * QSCALE * NORM_SLACK * jnp.max(jnp.abs(fox_qn_g[l])) * jnp.max(jnp.abs(fox_kn_g[l]))
                 ).astype(F32).reshape(1)
        glu, qb, kb, vbT, qc, kc, vcT, cmax, cmin = _in_proj(
            x2, norm1_g[l].reshape(1, D_MODEL), w_all, l, bf, gains, head_mean, tri, placement, batch, seq)
        ya = _conv(glu, conv_w[l], conv_b[l].reshape(1, CONV_CH), conv_ln_g[l].reshape(1, CONV_CH),
                   conv_ln_b[l].reshape(1, CONV_CH), seq)
        yb = _moba(qb, kb, vbT, bias, rel_bias.astype(F32), batch, seq)
        yc = _fox(qc, kc, vcT, cmax, cmin, bound, batch, seq)
        x2 = _out_ffn(x2, ya, yb, yc, wo_all, norm2_g[l].reshape(1, D_MODEL), wgu_all, wd_all, l)
    return x2.reshape(batch, seq, D_MODEL)
```

```python
import functools
import math

import numpy as np
import jax
import jax.numpy as jnp
from jax import lax
from jax.experimental import pallas as pl
from jax.experimental.pallas import tpu as pltpu

F32 = jnp.float32
BF16 = jnp.bfloat16

D_MODEL = 1024
HEAD_DIM = 64
CONV_CH = D_MODEL // 4
MOBA_DIM = 3 * D_MODEL // 8
FOX_DIM = D_MODEL - CONV_CH - MOBA_DIM
MOBA_HEADS = MOBA_DIM // HEAD_DIM
FOX_HEADS = FOX_DIM // HEAD_DIM
HEADS = MOBA_HEADS
CONV_WIDTH = 31
LN_EPS = 1e-5
RMS_EPS = 1e-6
MOBA_BLOCK = 256
MOBA_TOPK = 3
NUM_BUCKETS = 32
MAX_DISTANCE = 128
D_FF = -(-8 * D_MODEL // (3 * 256)) * 256
NEG = -1e30
LOG2E = math.log2(math.e)
QSCALE = HEAD_DIM ** -0.5 * LOG2E

LANES = 128
SUBLANES = 8
BF16_ROWS = 16
IN_MAIN = 2 * CONV_CH + 3 * MOBA_DIM + 3 * FOX_DIM
IN_PAD = IN_MAIN + LANES
AUG = HEADS * LANES
HM = 256
SEL_LANES = 16
V_ROWS = HEAD_DIM + BF16_ROWS

TM = 512
TQ = 256
TK = 256
HEAD_GROUP = 6
MOBA_MERGE = 8
FOX_MERGE = 4
LAZY_LIMIT = 64.0
UNDERFLOW_LOG2 = 160.0
ROUNDING_MARGIN = 2.0
NORM_SLACK = 1.01
CONV_TC = 512
CONV_HALO = 32
CONV_SUB = 64
FF_CHUNK = 768
FF_CHUNKS = tuple((c, min(c + FF_CHUNK, D_FF)) for c in range(0, D_FF, FF_CHUNK))
VMEM_LIMIT = 56 * 1024 * 1024


def _split3(x):
    hi = x.astype(BF16)
    r1 = x - hi.astype(F32)
    mid = r1.astype(BF16)
    lo = (r1 - mid.astype(F32)).astype(BF16)
    return hi, mid, lo


def _dot(a, b):
    return jnp.dot(a, b, preferred_element_type=F32)


def _dot_nt(a, b):
    return lax.dot_general(a, b, (((1,), (1,)), ((), ())), preferred_element_type=F32)


def _sigmoid(x):
    return 1.0 / (1.0 + jnp.exp(-x))


def _group(h):
    return slice(h * LANES, (h + 1) * LANES)


def _placement_constants():
    cq = np.zeros((LANES, AUG), np.float32)
    ck = np.zeros((LANES, AUG), np.float32)
    ones_q = np.zeros((1, AUG), np.float32)
    ones_k = np.zeros((1, AUG), np.float32)
    for h in range(FOX_HEADS):
        for j in range(3):
            cq[j * SUBLANES + h, h * LANES + HEAD_DIM + j] = 1.0
            ones_k[0, h * LANES + HEAD_DIM + j] = 1.0
            ck[j * SUBLANES + h, h * LANES + HEAD_DIM + 3 + j] = -1.0
            ones_q[0, h * LANES + HEAD_DIM + 3 + j] = 1.0
    return jnp.asarray(cq, BF16), jnp.asarray(ck, BF16), jnp.asarray(ones_q), jnp.asarray(ones_k)


def _in_proj_weight(w):
    return jnp.pad(w.astype(BF16), ((0, 0), (0, 0), (0, IN_PAD - w.shape[2])))


def _in_proj_kernel(nt, x_ref, g1_ref, w_ref, bf_ref, gains_ref, hm_ref, tri_ref,
                    cq_ref, ck_ref, onesq_ref, onesk_ref,
                    glu_ref, qb_ref, kb_ref, vbT_ref, qc_ref, kc_ref, vcT_ref, cmax_ref, cmin_ref,
                    h_scr, carry_scr):
    r = pl.program_id(0)

    @pl.when(r % nt == 0)
    def _():
        carry_scr[...] = jnp.zeros_like(carry_scr)

    x = x_ref[...]
    ms = jnp.mean(x * x, axis=-1, keepdims=True)
    h_scr[...] = (x * lax.rsqrt(ms + RMS_EPS) * g1_ref[...]).astype(BF16)
    lane = lax.broadcasted_iota(jnp.int32, (TM, LANES), 1)
    low = lane < HEAD_DIM

    def seg(c0, c1):
        return _dot(h_scr[...], w_ref[0, :, c0:c1])

    def store_heads(yq, yk, q_ref, k_ref, gains, upper_q, upper_k):
        y = jnp.concatenate([yq, yk], axis=1)
        sq = y * y
        hi = sq.astype(BF16)
        lo = (sq - hi.astype(F32)).astype(BF16)
        hm = hm_ref[...]
        msq = jnp.concatenate([_dot(hi[:, c:c + HM], hm) + _dot(lo[:, c:c + HM], hm) for c in range(0, AUG, HM)],
                              axis=1)
        y = y * lax.rsqrt(msq + RMS_EPS) * gains
        for ref, upper, first in ((q_ref, upper_q, 0), (k_ref, upper_k, HEADS // 2)):
            for pair in range(HEADS // 2):
                both = y[:, _group(first + pair)]
                ref[:, _group(2 * pair)] = jnp.where(low, both, upper(2 * pair)).astype(BF16)
                ref[:, _group(2 * pair + 1)] = jnp.where(
                    low, pltpu.roll(both, HEAD_DIM, axis=1), upper(2 * pair + 1)).astype(BF16)

    def pack_terms(t):
        hi, mid, lo = _split3(jnp.where(lane < FOX_HEADS, t, 0.0))
        return (hi.astype(F32) + pltpu.roll(mid.astype(F32), SUBLANES, axis=1)
                + pltpu.roll(lo.astype(F32), 2 * SUBLANES, axis=1)).astype(BF16)

    def unpack_terms(p):
        return p + pltpu.roll(p, LANES - SUBLANES, axis=1) + pltpu.roll(p, LANES - 2 * SUBLANES, axis=1)

    def store_vT(ref, v):
        vT = v.T.astype(BF16)
        row = lax.broadcasted_iota(jnp.int32, (BF16_ROWS, TM), 0)
        tail = jnp.where(row == 0, 1.0, 0.0).astype(BF16)
        for h in range(HEADS):
            ref[0, h * V_ROWS:h * V_ROWS + HEAD_DIM, :] = vT[h * HEAD_DIM:(h + 1) * HEAD_DIM, :]
            ref[0, h * V_ROWS + HEAD_DIM:(h + 1) * V_ROWS, :] = tail

    col_qk = 2 * CONV_CH
    col_vq, col_kv = col_qk + 2 * MOBA_DIM, col_qk + 3 * MOBA_DIM + FOX_DIM
    z = seg(IN_MAIN, IN_PAD) + bf_ref[...]
    y2 = seg(col_vq, col_kv)

    logf = jnp.minimum(z, 0.0) - jnp.log1p(jnp.exp(-jnp.abs(z)))
    csum = unpack_terms(_dot(tri_ref[...], pack_terms(logf)))
    c = csum + carry_scr[...]
    carry_scr[...] = c[TM - 1:TM, :]
    y3 = seg(col_kv, IN_MAIN)

    c2 = c * LOG2E
    for t in range(TM // TQ):
        part = c2[t * TQ:(t + 1) * TQ, :]
        cmax_ref[pl.ds(r * (TM // TQ) + t, 1), :] = jnp.max(part, axis=0, keepdims=True)
        cmin_ref[pl.ds(r * (TM // TQ) + t, 1), :] = jnp.min(part, axis=0, keepdims=True)
    packed = pack_terms(c2)
    upper_q = _dot(packed, cq_ref[...]) + onesq_ref[...]
    upper_k = _dot(packed, ck_ref[...]) + onesk_ref[...]
    y1 = seg(col_qk, col_vq)
    store_heads(y2[:, MOBA_DIM:], y3[:, :FOX_DIM], qc_ref, kc_ref, gains_ref[1:2, :],
                lambda h: upper_q[:, _group(h)], lambda h: upper_k[:, _group(h)])
    store_vT(vbT_ref, y2[:, :MOBA_DIM])
    store_vT(vcT_ref, y3[:, FOX_DIM:])
    a = seg(0, col_qk)

    pos = (r % nt) * TM + lax.broadcasted_iota(jnp.int32, (TM, LANES), 0)
    in_sel = (lane >= HEAD_DIM) & (lane < HEAD_DIM + 3 * SEL_LANES)
    onehot = jnp.where(in_sel & ((lane & (SEL_LANES - 1)) == pos // MOBA_BLOCK), 1.0, 0.0)
    store_heads(y1[:, :MOBA_DIM], y1[:, MOBA_DIM:], qb_ref, kb_ref, gains_ref[0:1, :],
                lambda h: 0.0, lambda h: onehot)
    glu_ref[...] = a[:, :CONV_CH] * _sigmoid(a[:, CONV_CH:])


def _in_proj(x2, g1, w, layer, bf, gains, hm, tri, placement, batch, seq):
    n = x2.shape[0]
    nt = seq // TM
    const = lambda shape: pl.BlockSpec(shape, lambda r: (0,) * len(shape), pipeline_mode=pl.Buffered(1))
    rows = lambda width: pl.BlockSpec((TM, width), lambda r: (r, 0))
    tposed = lambda height: pl.BlockSpec((1, height, TM), lambda r: (r // nt, 0, r % nt))
    out_shape = (
        jax.ShapeDtypeStruct((n, CONV_CH), F32),
        jax.ShapeDtypeStruct((n, AUG), BF16),
        jax.ShapeDtypeStruct((n, AUG), BF16),
        jax.ShapeDtypeStruct((batch, HEADS * V_ROWS, seq), BF16),
        jax.ShapeDtypeStruct((n, AUG), BF16),
        jax.ShapeDtypeStruct((n, AUG), BF16),
        jax.ShapeDtypeStruct((batch, HEADS * V_ROWS, seq), BF16),
        jax.ShapeDtypeStruct((n // TQ, LANES), F32),
        jax.ShapeDtypeStruct((n // TQ, LANES), F32),
    )
    whole = pl.BlockSpec((n // TQ, LANES), lambda r: (0, 0))
    return pl.pallas_call(
        functools.partial(_in_proj_kernel, nt),
        grid=(n // TM,),
        in_specs=[rows(D_MODEL), const((1, D_MODEL)),
                  pl.BlockSpec((1, D_MODEL, IN_PAD), lambda r: (layer, 0, 0), pipeline_mode=pl.Buffered(1)),
                  const((1, LANES)),
                  const((2, AUG)), const((HM, HM)), const((TM, TM)),
                  const((LANES, AUG)), const((LANES, AUG)), const((1, AUG)), const((1, AUG))],
        out_specs=(rows(CONV_CH), rows(AUG), rows(AUG), tposed(HEADS * V_ROWS),
                   rows(AUG), rows(AUG), tposed(HEADS * V_ROWS), whole, whole),
        out_shape=out_shape,
        scratch_shapes=[pltpu.VMEM((TM, D_MODEL), BF16), pltpu.VMEM((1, LANES), F32)],
        compiler_params=pltpu.CompilerParams(dimension_semantics=("arbitrary",),
                                             vmem_limit_bytes=VMEM_LIMIT),
        name="in_proj",
    )(x2, g1, w, bf, gains, hm, tri, *placement)


def _softmax_steps(score_fn, m_scr, acc_scr, first):
    for g0 in range(0, HEADS, HEAD_GROUP):
        hs = range(g0, min(g0 + HEAD_GROUP, HEADS))
        tiles = [score_fn(h) for h in hs]
        tile_max = [jnp.max(sT, axis=0, keepdims=True) for sT, _ in tiles]
        if first:
            m_new = tile_max
        else:
            m_old = [m_scr[h:h + 1, :] for h in hs]
            m_new = [jnp.maximum(a, b) for a, b in zip(m_old, tile_max)]
        pvs = [_dot(vT, jnp.exp2(sT - mn).astype(BF16)) for (sT, vT), mn in zip(tiles, m_new)]
        for n, h in enumerate(hs):
            if first:
                acc_scr[h] = pvs[n]
            else:
                acc_scr[h] = jnp.exp2(m_old[n] - m_new[n]) * acc_scr[h] + pvs[n]
            m_scr[h:h + 1, :] = m_new[n]


def _softmax_steps_lazy(score_fn, m_scr, acc_scr, pv_scr, seed_rows=None, chunk_fn=None, nchunks=1):
    first = seed_rows is not None
    worst = None
    if chunk_fn is None:
        chunk_fn, nchunks = (lambda h, c: score_fn(h)), 1
    tile_max, pv = [None] * HEADS, [None] * HEADS
    for c in range(nchunks):
        tiles = [chunk_fn(h, c) for h in range(HEADS)]
        for h in range(HEADS):
            sT, vT = tiles[h]
            if first:
                m_old = jnp.max(sT[seed_rows[0]:seed_rows[1], :], axis=0, keepdims=True)
                m_scr[h:h + 1, :] = m_old
            else:
                m_old = m_scr[h:h + 1, :]
            chunk_max = jnp.max(sT, axis=0, keepdims=True)
            chunk_pv = _dot(vT, jnp.exp2(sT - m_old).astype(BF16))
            tile_max[h] = chunk_max if c == 0 else jnp.maximum(tile_max[h], chunk_max)
            pv[h] = chunk_pv if c == 0 else pv[h] + chunk_pv
    for h in range(HEADS):
        pv_scr[h] = pv[h]
        m_scr[SUBLANES + h:SUBLANES + h + 1, :] = tile_max[h]
        rise = tile_max[h] - m_scr[h:h + 1, :]
        worst = rise if worst is None else jnp.maximum(worst, rise)
    safe = jnp.max(worst) <= LAZY_LIMIT

    @pl.when(safe)
    def _():
        for h in range(HEADS):
            m_old = m_scr[h:h + 1, :]
            m_new = jnp.maximum(m_old, m_scr[SUBLANES + h:SUBLANES + h + 1, :])
            acc = pv_scr[h] if first else acc_scr[h] + pv_scr[h]
            acc_scr[h] = jnp.exp2(m_old - m_new) * acc
            m_scr[h:h + 1, :] = m_new

    @pl.when(jnp.logical_not(safe))
    def _():
        _softmax_steps(score_fn, m_scr, acc_scr, first=first)


def _write_heads(o_ref, acc_scr):
    for pair in range(HEADS // 2):
        both = []
        for h in (2 * pair, 2 * pair + 1):
            acc = acc_scr[h]
            both.append(acc[:HEAD_DIM, :] / acc[HEAD_DIM:HEAD_DIM + 1, :])
        o_ref[:, pair * LANES:(pair + 1) * LANES] = jnp.concatenate(both, axis=0).T.astype(BF16)


def _key_tile(k_ref, vT_ref, h, k0, tk=TK):
    return k_ref[pl.ds(k0, tk), _group(h)], vT_ref[0, h * V_ROWS:(h + 1) * V_ROWS, pl.ds(k0, tk)]


def _past_tiles(first, count, step, merge):
    big = merge * TK

    def body(j, carry):
        step(pl.multiple_of(first * TK + j * big, TK), big)
        return carry

    lax.fori_loop(0, count // merge, body, 0)
    done = first + (count // merge) * merge
    count = first + count
    part = merge // 2
    while part >= 1:
        rem = count - done

        @pl.when(rem >= part)
        def _(done=done, part=part):
            step(pl.multiple_of(done * TK, TK), part * TK)

        done = done + jnp.where(rem >= part, part, 0)
        part //= 2


_ATTN_SCRATCH = [
    pltpu.VMEM((2 * SUBLANES, TQ), F32),
    pltpu.VMEM((HEADS, V_ROWS, TQ), F32),
    pltpu.VMEM((HEADS, V_ROWS, TQ), F32),
]


def _attention_specs(batch, seq):
    nq = seq // TQ
    in_specs = [pl.BlockSpec((TQ, AUG), lambda b, i: (b * nq + i, 0)),
                pl.BlockSpec((seq, AUG), lambda b, i: (b, 0)),
                pl.BlockSpec((1, HEADS * V_ROWS, seq), lambda b, i: (b, 0, 0))]
    out_spec = pl.BlockSpec((TQ, MOBA_DIM), lambda b, i: (b * nq + i, 0))
    params = pltpu.CompilerParams(dimension_semantics=("arbitrary", "arbitrary"), vmem_limit_bytes=VMEM_LIMIT)
    return (batch, nq), in_specs, out_spec, params


def _t5_bucket(dist):
    n = jnp.maximum(dist, 0)
    max_exact = NUM_BUCKETS // 2
    nf = jnp.maximum(n, 1).astype(F32)
    large = max_exact + (jnp.log(nf / max_exact) / math.log(MAX_DISTANCE / max_exact)
                         * (NUM_BUCKETS - max_exact)).astype(jnp.int32)
    large = jnp.minimum(large, NUM_BUCKETS - 1)
    return jnp.where(n < max_exact, n, large)


def _bias_kernel(bucket_ref, rb_ref, o_ref):
    h = pl.program_id(0)
    bucket = bucket_ref[...]
    acc = jnp.zeros(bucket.shape, F32)
    for b in range(NUM_BUCKETS):
        acc = jnp.where(bucket == b, rb_ref[b, h], acc)
    key = lax.broadcasted_iota(jnp.int32, bucket.shape, 0)
    qry = lax.broadcasted_iota(jnp.int32, bucket.shape, 1)
    o_ref[0] = jnp.where((key >= MOBA_BLOCK) | (key <= qry), acc * LOG2E, NEG)


def _moba_bias(rel_bias):
    key = lax.broadcasted_iota(jnp.int32, (2 * MOBA_BLOCK, TQ), 0)
    qry = lax.broadcasted_iota(jnp.int32, (2 * MOBA_BLOCK, TQ), 1)
    bucket = _t5_bucket(jnp.where(key < MOBA_BLOCK, qry - key, qry + 2 * MOBA_BLOCK - key))
    return pl.pallas_call(
        _bias_kernel,
        grid=(MOBA_HEADS,),
        in_specs=[pl.BlockSpec((2 * MOBA_BLOCK, TQ), lambda h: (0, 0)),
                  pl.BlockSpec(memory_space=pltpu.SMEM)],
        out_specs=pl.BlockSpec((1, 2 * MOBA_BLOCK, TQ), lambda h: (h, 0, 0)),
        out_shape=jax.ShapeDtypeStruct((MOBA_HEADS, 2 * MOBA_BLOCK, TQ), F32),
        compiler_params=pltpu.CompilerParams(dimension_semantics=("arbitrary",)),
        name="moba_bias",
    )(bucket, rel_bias)


def _moba_kernel(nblk, q_ref, k_ref, vT_ref, bias_ref, rb_ref, o_ref, m_scr, acc_scr, pv_scr, qm_scr, kmean_scr):
    i = pl.program_id(1)

    @pl.when(i == 0)
    def _():
        kmean_scr[...] = jnp.zeros_like(kmean_scr)
        for n in range(nblk):
            blk = k_ref[n * MOBA_BLOCK:(n + 1) * MOBA_BLOCK, :].astype(F32)
            kmean_scr[n:n + 1, :] = jnp.mean(blk, axis=0, keepdims=True)

    q0 = pl.multiple_of(i * TQ, TQ)
    blk_row = lax.broadcasted_iota(jnp.int32, (SEL_LANES, TQ), 0)

    for h in range(HEADS):
        qg = q_ref[:, _group(h)]
        terms = _dot_nt(jnp.concatenate(_split3(kmean_scr[:, _group(h)]), axis=0), qg)
        gate = terms[:SEL_LANES] + terms[SEL_LANES:2 * SEL_LANES] + terms[2 * SEL_LANES:]
        gate = jnp.where(blk_row < i, gate, -jnp.inf)
        picked = jnp.zeros((SEL_LANES, TQ), F32)
        for _ in range(MOBA_TOPK):
            best = jnp.max(gate, axis=0, keepdims=True)
            first = jnp.min(jnp.where(gate == best, blk_row, SEL_LANES), axis=0, keepdims=True)
            pick = blk_row == first
            picked = jnp.where(pick, 1.0, picked)
            gate = jnp.where(pick, -jnp.inf, gate)
        chosen = (picked > 0.0) & (blk_row < i)
        far = jnp.where(blk_row == i - 1, 0.0, rb_ref[NUM_BUCKETS - 1, h] * LOG2E)
        term = jnp.where(chosen, far, jnp.where(blk_row < i, NEG, 0.0))
        t_hi, t_mid, t_lo = _split3(term)
        lanes = jnp.concatenate(
            [jnp.zeros((HEAD_DIM, TQ), F32), t_hi.astype(F32), t_mid.astype(F32), t_lo.astype(F32),
             jnp.zeros((LANES - HEAD_DIM - 3 * SEL_LANES, TQ), F32)], axis=0)
        qm_scr[h] = (qg.astype(F32) + lanes.T).astype(BF16)

    @pl.when(i == 0)
    def _():
        def own_scores(h):
            kj, vTj = _key_tile(k_ref, vT_ref, h, q0)
            return _dot_nt(kj, qm_scr[h]) + bias_ref[h, :MOBA_BLOCK, :], vTj

        _softmax_steps_lazy(own_scores, m_scr, acc_scr, pv_scr, seed_rows=(0, BF16_ROWS))

    @pl.when(i >= 1)
    def _():
        k0 = pl.multiple_of((i - 1) * TK, TK)

        def near_scores(h):
            k_own, vT_own = _key_tile(k_ref, vT_ref, h, q0)
            k_prev, vT_prev = _key_tile(k_ref, vT_ref, h, k0)
            sT = _dot_nt(jnp.concatenate([k_own, k_prev], axis=0), qm_scr[h]) + bias_ref[h]
            return sT, jnp.concatenate([vT_own, vT_prev], axis=1)

        _softmax_steps_lazy(near_scores, m_scr, acc_scr, pv_scr, seed_rows=(0, BF16_ROWS))

    def far_step(k0, tk):
        def far_scores(h):
            kj, vTj = _key_tile(k_ref, vT_ref, h, k0, tk)
            return _dot_nt(kj, qm_scr[h]), vTj

        def far_chunk(h, c):
            kj, vTj = _key_tile(k_ref, vT_ref, h, pl.multiple_of(k0 + c * TK, TK))
            return _dot_nt(kj, qm_scr[h]), vTj

        _softmax_steps_lazy(far_scores, m_scr, acc_scr, pv_scr, chunk_fn=far_chunk, nchunks=tk // TK)

    _past_tiles(0, jnp.maximum(i - 1, 0), far_step, MOBA_MERGE)
    _write_heads(o_ref, acc_scr)


def _moba(q, k, vT, bias, rel_bias, batch, seq):
    nblk = seq // MOBA_BLOCK
    grid, in_specs, out_spec, params = _attention_specs(batch, seq)
    return pl.pallas_call(
        functools.partial(_moba_kernel, nblk),
        grid=grid,
        in_specs=in_specs + [pl.BlockSpec((MOBA_HEADS, 2 * MOBA_BLOCK, TQ), lambda b, i: (0, 0, 0)),
                             pl.BlockSpec(memory_space=pltpu.SMEM)],
        out_specs=out_spec,
        out_shape=jax.ShapeDtypeStruct((q.shape[0], MOBA_DIM), BF16),
        scratch_shapes=_ATTN_SCRATCH + [pltpu.VMEM((HEADS, TQ, LANES), BF16),
                                        pltpu.VMEM((SEL_LANES, AUG), F32)],
        compiler_params=params,
        name="moba_attention",
    )(q, k, vT, bias, rel_bias)


def _fox_kernel(q_ref, k_ref, vT_ref, cmax_ref, cmin_ref, bound_ref, o_ref, m_scr, acc_scr, pv_scr):
    i = pl.program_id(1)
    q0 = pl.multiple_of(i * TQ, TQ)

    ntiles = cmin_ref.shape[0]
    gap = cmax_ref[pl.ds(i, 1), :] - cmin_ref[...]
    limit = -(2.0 * bound_ref[0] + (UNDERFLOW_LOG2 + ROUNDING_MARGIN))
    head_lane = lax.broadcasted_iota(jnp.int32, (ntiles, LANES), 1) < FOX_HEADS
    live = jnp.max(jnp.where(head_lane & (gap > limit), 1, 0), axis=1, keepdims=True)
    tile = lax.broadcasted_iota(jnp.int32, (ntiles, 1), 0)
    first_live = jnp.min(jnp.where((live > 0) & (tile < i), tile, i))

    def scores(h, k0, tk=TK):
        kj, vTj = _key_tile(k_ref, vT_ref, h, k0, tk)
        return _dot_nt(kj, q_ref[:, _group(h)]), vTj

    key = lax.broadcasted_iota(jnp.int32, (TK, TQ), 0)
    qry = lax.broadcasted_iota(jnp.int32, (TK, TQ), 1)

    def diag_scores(h):
        sT, vTj = scores(h, q0)
        return jnp.where(key <= qry, sT, NEG), vTj

    _softmax_steps(diag_scores, m_scr, acc_scr, first=True)

    def past_step(k0, tk):
        _softmax_steps_lazy(lambda h: scores(h, k0, tk), m_scr, acc_scr, pv_scr,
                            chunk_fn=lambda h, c: scores(h, pl.multiple_of(k0 + c * TK, TK)), nchunks=tk // TK)

    _past_tiles(first_live, i - first_live, past_step, FOX_MERGE)
    _write_heads(o_ref, acc_scr)


def _fox(q, k, vT, cmax, cmin, bound, batch, seq):
    grid, in_specs, out_spec, params = _attention_specs(batch, seq)
    ntiles = seq // TQ
    return pl.pallas_call(
        _fox_kernel,
        grid=grid,
        in_specs=in_specs + [pl.BlockSpec((ntiles, LANES), lambda b, i: (b, 0)),
                             pl.BlockSpec((ntiles, LANES), lambda b, i: (b, 0)),
                             pl.BlockSpec(memory_space=pltpu.SMEM)],
        out_specs=out_spec,
        out_shape=jax.ShapeDtypeStruct((q.shape[0], FOX_DIM), BF16),
        scratch_shapes=_ATTN_SCRATCH,
        compiler_params=params,
        name="fox_attention",
    )(q, k, vT, cmax, cmin, bound)


def _conv_rows(w_ref, b_ref, lg_ref, lb_ref, sbuf, r0):
    shift = CONV_HALO - (CONV_WIDTH - 1)
    acc = jnp.zeros((CONV_SUB, CONV_CH), F32) + b_ref[...]
    for j in range(CONV_WIDTH):
        rho, base = (j + shift) % SUBLANES, (j + shift) // SUBLANES * SUBLANES
        acc = acc + w_ref[j:j + 1, :] * sbuf[rho, r0 + base:r0 + base + CONV_SUB, :]
    mu = jnp.mean(acc, axis=-1, keepdims=True)
    d = acc - mu
    var = jnp.mean(d * d, axis=-1, keepdims=True)
    y = d * lax.rsqrt(var + LN_EPS) * lg_ref[...] + lb_ref[...]
    return (y * _sigmoid(y)).astype(BF16)


def _conv_kernel(nt, glu_ref, w_ref, b_ref, lg_ref, lb_ref, o_ref, hbuf, sbuf):
    t = pl.program_id(0) % nt

    @pl.when(t == 0)
    def _():
        hbuf[0:CONV_HALO, :] = jnp.zeros((CONV_HALO, CONV_CH), F32)

    @pl.when(t > 0)
    def _():
        hbuf[0:CONV_HALO, :] = hbuf[CONV_TC:CONV_TC + CONV_HALO, :]

    hbuf[CONV_HALO:, :] = glu_ref[...]
    for rho in range(SUBLANES):
        rows = CONV_TC + CONV_HALO - (SUBLANES if rho else 0)
        sbuf[rho, 0:rows, :] = hbuf[rho:rho + rows, :]
    for r0 in range(0, CONV_TC, CONV_SUB):
        o_ref[r0:r0 + CONV_SUB, :] = _conv_rows(w_ref, b_ref, lg_ref, lb_ref, sbuf, r0)


def _conv(glu, w, b, lg, lb, seq):
    n = glu.shape[0]
    const = lambda shape: pl.BlockSpec(shape, lambda r: (0,) * len(shape))
    return pl.pallas_call(
        functools.partial(_conv_kernel, seq // CONV_TC),
        grid=(n // CONV_TC,),
        in_specs=[pl.BlockSpec((CONV_TC, CONV_CH), lambda r: (r, 0)),
                  const((CONV_WIDTH, CONV_CH)), const((1, CONV_CH)), const((1, CONV_CH)), const((1, CONV_CH))],
        out_specs=pl.BlockSpec((CONV_TC, CONV_CH), lambda r: (r, 0)),
        out_shape=jax.ShapeDtypeStruct((n, CONV_CH), BF16),
        scratch_shapes=[pltpu.VMEM((CONV_HALO + CONV_TC, CONV_CH), F32),
                        pltpu.VMEM((SUBLANES, CONV_HALO + CONV_TC, CONV_CH), F32)],
        compiler_params=pltpu.CompilerParams(dimension_semantics=("arbitrary",)),
        name="conv_module",
    )(glu, w, b, lg, lb)


def _out_ffn_kernel(x_ref, ya_ref, yb_ref, yc_ref, wo_ref, g2_ref, wgu_ref, wd_ref, o_ref,
                    mix_scr, h2_scr, acc_scr):
    mix_scr[:, 0:CONV_CH] = ya_ref[...]
    mix_scr[:, CONV_CH:CONV_CH + MOBA_DIM] = yb_ref[...]
    mix_scr[:, CONV_CH + MOBA_DIM:] = yc_ref[...]
    x1 = x_ref[...] + _dot(mix_scr[...], wo_ref[0])
    ms = jnp.mean(x1 * x1, axis=-1, keepdims=True)
    h2_scr[...] = (x1 * lax.rsqrt(ms + RMS_EPS) * g2_ref[...]).astype(BF16)
    acc_scr[...] = x1
    for c0, c1 in FF_CHUNKS:
        g = _dot(h2_scr[...], wgu_ref[0, :, c0:c1])
        u = _dot(h2_scr[...], wgu_ref[0, :, D_FF + c0:D_FF + c1])
        act = (g * _sigmoid(g) * u).astype(BF16)
        acc_scr[...] += _dot(act, wd_ref[0, c0:c1, :])
    o_ref[...] = acc_scr[...]


def _out_ffn(x2, ya, yb, yc, wo, g2, wgu, wd, layer):
    n = x2.shape[0]
    const = lambda shape: pl.BlockSpec(shape, lambda r: (0,) * len(shape), pipeline_mode=pl.Buffered(1))
    of_layer = lambda shape: pl.BlockSpec((1,) + shape, lambda r: (layer, 0, 0), pipeline_mode=pl.Buffered(1))
    rows = lambda width: pl.BlockSpec((TM, width), lambda r: (r, 0))
    return pl.pallas_call(
        _out_ffn_kernel,
        grid=(n // TM,),
        in_specs=[rows(D_MODEL), rows(CONV_CH), rows(MOBA_DIM), rows(FOX_DIM),
                  of_layer((D_MODEL, D_MODEL)), const((1, D_MODEL)), of_layer((D_MODEL, 2 * D_FF)),
                  of_layer((D_FF, D_MODEL))],
        out_specs=rows(D_MODEL),
        out_shape=jax.ShapeDtypeStruct((n, D_MODEL), F32),
        scratch_shapes=[pltpu.VMEM((TM, D_MODEL), BF16), pltpu.VMEM((TM, D_MODEL), BF16),
                        pltpu.VMEM((TM, D_MODEL), F32)],
        compiler_params=pltpu.CompilerParams(dimension_semantics=("arbitrary",),
                                             vmem_limit_bytes=VMEM_LIMIT),
        name="out_ffn",
    )(x2, ya, yb, yc, wo, g2, wgu, wd)


def kernel(x, w_in, b_forget, conv_w, conv_b, conv_ln_g, conv_ln_b, moba_qn_g, moba_kn_g,
           fox_qn_g, fox_kn_g, rel_bias, w_out, norm1_g, norm2_g, w_gate_up, w_down):
    batch, seq, d_model = x.shape
    depth = w_in.shape[0]
    assert d_model == D_MODEL and seq % TM == 0 and seq % MOBA_BLOCK == 0 and TQ == MOBA_BLOCK == TK
    assert w_in.shape[2] == IN_MAIN + FOX_HEADS and MOBA_HEADS == FOX_HEADS <= SUBLANES
    assert seq // MOBA_BLOCK <= SEL_LANES and HEAD_DIM + 3 * SEL_LANES <= LANES

    hd = jnp.arange(HM) // HEAD_DIM
    head_mean = jnp.where(hd[:, None] == hd[None, :], 1.0 / HEAD_DIM, 0.0).astype(BF16)
    tri = (jnp.arange(TM)[:, None] >= jnp.arange(TM)[None, :]).astype(BF16)
    placement = _placement_constants()

    bias = _moba_bias(rel_bias.astype(F32))
    x2 = x.reshape(batch * seq, D_MODEL)
    w_all = _in_proj_weight(w_in)
    wo_all, wgu_all, wd_all = w_out.astype(BF16), w_gate_up.astype(BF16), w_down.astype(BF16)
    for l in range(depth):
        bf = jnp.pad(b_forget[l].astype(F32), (0, LANES - FOX_HEADS)).reshape(1, LANES)
        gains = jnp.stack([jnp.concatenate([jnp.tile(gq[l].astype(F32), HEADS) * QSCALE,
                                            jnp.tile(gk[l].astype(F32), HEADS)])
                           for gq, gk in ((moba_qn_g, moba_kn_g), (fox_qn_g, fox_kn_g))])
        bound = (HEAD_DIM * QSCALE * NORM_SLACK * jnp.max(jnp.abs(fox_qn_g[l])) * jnp.max(jnp.abs(fox_kn_g[l]))
                 ).astype(F32).reshape(1)
        glu, qb, kb, vbT, qc, kc, vcT, cmax, cmin = _in_proj(
            x2, norm1_g[l].reshape(1, D_MODEL), w_all, l, bf, gains, head_mean, tri, placement, batch, seq)
        ya = _conv(glu, conv_w[l], conv_b[l].reshape(1, CONV_CH), conv_ln_g[l].reshape(1, CONV_CH),
                   conv_ln_b[l].reshape(1, CONV_CH), seq)
        yb = _moba(qb, kb, vbT, bias, rel_bias.astype(F32), batch, seq)
        yc = _fox(qc, kc, vcT, cmax, cmin, bound, batch, seq)
        x2 = _out_ffn(x2, ya, yb, yc, wo_all, norm2_g[l].reshape(1, D_MODEL), wgu_all, wd_all, l)
    return x2.reshape(batch, seq, D_MODEL)
```
